```python
import jax, jax.numpy as jnp
from jax import lax
import numpy as np

D_MODEL = 1024
BATCH = 2
SEQ = 8192
DEPTH = 4
DEC_BATCH = 32
DEC_SEQ = 4
PAST_LEN = 8192
PAGE_SIZE = 128

D_MIX = D_MODEL
W_A = D_MIX // 4
W_B = D_MIX // 4
W_C = D_MIX // 4
W_D = D_MIX // 4
N_HEADS_B = 4
HEAD_DIM_B = W_B // N_HEADS_B
ROT_DIM = HEAD_DIM_B // 4
ROPE_THETA = 500000.0
MOBA_BLOCK = 256
MOBA_TOPK = 3
Q_BLOCK = 128
CONV_A_WIDTH = 3
CONV_C_WIDTH = 31
POOL_WINDOWS = (2, 4, 8, 16)
N_POOL_GROUPS = 4
POOL_GROUP = W_D // N_POOL_GROUPS
POOL_STATE = 15
RMS_EPS = 1e-6
LN_EPS = 1e-5
BRANCH_WIDTHS = (W_A, W_A, W_A, W_A, W_B, W_B, W_B, W_B, W_C, W_C, W_C, W_D, W_D)
D_IN = 4 * W_A + 4 * W_B + 3 * W_C + 2 * W_D

kernel_name = 'hybrid_shortconv_moba_conformer_pool_step'


def rmsnorm(x, g):
    xf = x.astype(jnp.float32)
    y = xf * lax.rsqrt(jnp.mean(xf * xf, axis=-1, keepdims=True) + RMS_EPS)
    return (y * g.astype(jnp.float32)).astype(x.dtype)


def layernorm(x, g, b):
    xf = x.astype(jnp.float32)
    mu = jnp.mean(xf, axis=-1, keepdims=True)
    var = jnp.mean(jnp.square(xf - mu), axis=-1, keepdims=True)
    y = (xf - mu) * lax.rsqrt(var + LN_EPS)
    return (y * g.astype(jnp.float32) + b.astype(jnp.float32)).astype(x.dtype)


def causal_dwconv(prefix, x, w):
    width = w.shape[0]
    buf = jnp.concatenate([prefix.astype(x.dtype), x], axis=1)
    y = lax.conv_general_dilated(buf, w[:, None, :].astype(x.dtype), window_strides=(1,),
                                 padding='VALID', dimension_numbers=('NWC', 'WIO', 'NWC'),
                                 feature_group_count=x.shape[-1])
    return y, buf[:, buf.shape[1] - (width - 1):]


def partial_rope(x, pos):
    half = ROT_DIM // 2
    freqs = ROPE_THETA ** (-jnp.arange(half, dtype=jnp.float32) * 2.0 / ROT_DIM)
    ang = pos.astype(jnp.float32)[:, None] * freqs[None, :]
    cos = jnp.cos(ang)[None, :, None, :]
    sin = jnp.sin(ang)[None, :, None, :]
    xf = x.astype(jnp.float32)
    x1, x2, rest = xf[..., :half], xf[..., half:ROT_DIM], xf[..., ROT_DIM:]
    return jnp.concatenate([x1 * cos - x2 * sin, x1 * sin + x2 * cos, rest], axis=-1).astype(x.dtype)


def moba_block(q, q_pos, kbl, vbl, k_means, k_eff):
    B, Tq, H, dh = q.shape
    nb = kbl.shape[2]
    own = q_pos // MOBA_BLOCK
    gate = jnp.einsum('bqhd,bhnd->bhqn', q.astype(jnp.float32), k_means)
    fully_past = jnp.arange(nb)[None, :] < own[:, None]
    gate = jnp.where(fully_past[None, None], gate, -jnp.inf)
    _, sel = lax.top_k(gate, k_eff)
    idx = jnp.concatenate([sel.astype(jnp.int32),
                           jnp.broadcast_to(own[None, None, :, None], (B, H, Tq, 1)).astype(jnp.int32)], axis=-1)
    bi = jnp.arange(B)[:, None, None, None]
    hi = jnp.arange(H)[None, :, None, None]
    kg = kbl[bi, hi, idx]
    vg = vbl[bi, hi, idx]
    s = jnp.einsum('bqhd,bhqskd->bhqsk', q, kg).astype(jnp.float32)
    key_pos = idx[..., None] * MOBA_BLOCK + jnp.arange(MOBA_BLOCK)
    sel_ok = jnp.arange(k_eff)[None, :] < own[:, None]
    sel_mask = jnp.broadcast_to(sel_ok[None, None, :, :, None], (B, H, Tq, k_eff, MOBA_BLOCK))
    own_mask = key_pos[:, :, :, k_eff:, :] <= q_pos[None, None, :, None, None]
    mask = jnp.concatenate([sel_mask, own_mask], axis=3)
    s = jnp.where(mask, s, -jnp.inf).reshape(B, H, Tq, -1)
    p = jax.nn.softmax(s, axis=-1).astype(vg.dtype).reshape(B, H, Tq, k_eff + 1, MOBA_BLOCK)
    return jnp.einsum('bhqsk,bhqskd->bqhd', p, vg)


def moba_attention(q, k_all, v_all, q_pos):
    B, L, H, dh = k_all.shape
    Tq = q.shape[1]
    nb = -(-L // MOBA_BLOCK)
    pad = nb * MOBA_BLOCK - L
    k_pad = jnp.pad(k_all, ((0, 0), (0, pad), (0, 0), (0, 0)))
    v_pad = jnp.pad(v_all, ((0, 0), (0, pad), (0, 0), (0, 0)))
    kbl = k_pad.reshape(B, nb, MOBA_BLOCK, H, dh).transpose(0, 3, 1, 2, 4)
    vbl = v_pad.reshape(B, nb, MOBA_BLOCK, H, dh).transpose(0, 3, 1, 2, 4)
    k_means = jnp.mean(kbl.astype(jnp.float32), axis=3)
    k_eff = max(1, min(MOBA_TOPK, nb - 1))
    qs = (q * (dh ** -0.5)).astype(q.dtype)

    def attend(args):
        qc, pc = args
        return moba_block(qc, pc, kbl, vbl, k_means, k_eff)

    if Tq > Q_BLOCK and Tq % Q_BLOCK == 0:
        n = Tq // Q_BLOCK
        qc = qs.reshape(B, n, Q_BLOCK, H, dh).transpose(1, 0, 2, 3, 4)
        pc = q_pos.reshape(n, Q_BLOCK)
        out = lax.map(attend, (qc, pc))
        return out.transpose(1, 0, 2, 3, 4).reshape(B, Tq, H, dh)
    return attend((qs, q_pos))


def multiscale_pool(prefix, x, pos, w_pool, pool_scale):
    B, T, C = x.shape
    buf = jnp.concatenate([prefix.astype(x.dtype), x], axis=1)
    bf = buf.astype(jnp.float32)
    cs = jnp.concatenate([jnp.zeros((B, 1, C), jnp.float32), jnp.cumsum(bf, axis=1)], axis=1)
    end = cs[:, POOL_STATE + 1:]
    means = []
    for g, w in enumerate(POOL_WINDOWS):
        lo, hi = g * POOL_GROUP, (g + 1) * POOL_GROUP
        start = cs[:, POOL_STATE + 1 - w:POOL_STATE + 1 - w + T, lo:hi]
        cnt = jnp.minimum(pos + 1, w).astype(jnp.float32)[None, :, None]
        means.append((end[:, :, lo:hi] - start) / cnt)
    diff = jnp.concatenate(means, axis=-1) - bf[:, POOL_STATE:]
    mixed = jnp.einsum('btgc,gce->btge', diff.reshape(B, T, N_POOL_GROUPS, POOL_GROUP),
                       w_pool.astype(jnp.float32)).reshape(B, T, C)
    y = (mixed * pool_scale.astype(jnp.float32)).astype(x.dtype)
    return y, buf[:, buf.shape[1] - POOL_STATE:]


def mixer_layer(x, pos0, k_past, v_past, pre_a, pre_c, pre_d,
                norm_g, w_in, conv_a_w, conv_c_w, conv_c_b, ln_c_g, ln_c_b,
                w_pw_c, w_pool, pool_scale, w_out):
    B, T, _ = x.shape
    h = rmsnorm(x, norm_g)
    proj = jnp.einsum('btd,de->bte', h, w_in)
    cuts = [sum(BRANCH_WIDTHS[:i + 1]) for i in range(len(BRANCH_WIDTHS) - 1)]
    (a_x, a_b, a_c, a_gate, q, k, v, b_gate,
     c_val, c_glu, c_gate, d_x, d_gate) = jnp.split(proj, cuts, axis=-1)
    pos = pos0 + jnp.arange(T, dtype=jnp.int32)
    conv_a, new_a = causal_dwconv(pre_a, a_c * a_x, conv_a_w)
    y_a = a_b * conv_a
    q = partial_rope(q.reshape(B, T, N_HEADS_B, HEAD_DIM_B), pos)
    k = partial_rope(k.reshape(B, T, N_HEADS_B, HEAD_DIM_B), pos)
    v = v.reshape(B, T, N_HEADS_B, HEAD_DIM_B)
    k_all = jnp.concatenate([k_past.astype(k.dtype), k], axis=1)
    v_all = jnp.concatenate([v_past.astype(v.dtype), v], axis=1)
    y_b = moba_attention(q, k_all, v_all, pos).reshape(B, T, W_B)
    u_c = c_val * jax.nn.sigmoid(c_glu)
    conv_c, new_c = causal_dwconv(pre_c, u_c, conv_c_w)
    conv_c = conv_c + conv_c_b
    y_c = jnp.einsum('btc,ce->bte', jax.nn.silu(layernorm(conv_c, ln_c_g, ln_c_b)), w_pw_c)
    y_d, new_d = multiscale_pool(pre_d, d_x, pos, w_pool, pool_scale)
    mixed = jnp.concatenate([y_a * jax.nn.silu(a_gate), y_b * jax.nn.silu(b_gate),
                             y_c * jax.nn.silu(c_gate), y_d * jax.nn.silu(d_gate)], axis=-1)
    out = x + jnp.einsum('btm,md->btd', mixed, w_out)
    return out, k, v, new_a, new_c, new_d


def setup_inputs(seed: int = 0) -> dict:
    key = jax.random.key(seed)
    ks = jax.random.split(key, 24)
    n_pages = PAST_LEN // PAGE_SIZE
    n_used = DEC_BATCH * n_pages
    n_pool = n_used + n_used // 4
    f32 = jnp.float32

    def nrm(k, shape, scale):
        return jax.random.normal(k, shape, f32) * scale

    x_prompt = nrm(ks[0], (BATCH, SEQ, D_MODEL), 1.0)
    x_sample = nrm(ks[1], (DEC_BATCH, DEC_SEQ, D_MODEL), 1.0)
    cache_k = nrm(ks[2], (DEPTH, n_pool, PAGE_SIZE, N_HEADS_B, HEAD_DIM_B), 1.0)
    cache_v = nrm(ks[3], (DEPTH, n_pool, PAGE_SIZE, N_HEADS_B, HEAD_DIM_B), 1.0)
    page_table = jax.random.permutation(ks[4], n_pool)[:n_used].reshape(DEC_BATCH, n_pages).astype(jnp.int32)
    state_conv_a = nrm(ks[5], (DEPTH, DEC_BATCH, CONV_A_WIDTH - 1, W_A), 1.0)
    state_conv_c = nrm(ks[6], (DEPTH, DEC_BATCH, CONV_C_WIDTH - 1, W_C), 0.5)
    state_pool_d = nrm(ks[7], (DEPTH, DEC_BATCH, POOL_STATE, W_D), 1.0)
    norm_g = 1.0 + nrm(ks[8], (DEPTH, D_MODEL), 0.1)
    w_in = nrm(ks[9], (DEPTH, D_MODEL, D_IN), D_MODEL ** -0.5)
    conv_a_w = nrm(ks[10], (DEPTH, CONV_A_WIDTH, W_A), CONV_A_WIDTH ** -0.5)
    conv_c_w = nrm(ks[11], (DEPTH, CONV_C_WIDTH, W_C), CONV_C_WIDTH ** -0.5)
    conv_c_b = nrm(ks[12], (DEPTH, W_C), 0.02)
    ln_c_g = 1.0 + nrm(ks[13], (DEPTH, W_C), 0.1)
    ln_c_b = nrm(ks[14], (DEPTH, W_C), 0.02)
    w_pw_c = nrm(ks[15], (DEPTH, W_C, W_C), W_C ** -0.5)
    w_pool = nrm(ks[16], (DEPTH, N_POOL_GROUPS, POOL_GROUP, POOL_GROUP), POOL_GROUP ** -0.5)
    pool_scale = 1.0 + nrm(ks[17], (DEPTH, W_D), 0.1)
    w_out = nrm(ks[18], (DEPTH, D_MIX, D_MODEL), D_MIX ** -0.5)
    final_norm_g = 1.0 + nrm(ks[19], (D_MODEL,), 0.1)
    return {'x_prompt': x_prompt, 'x_sample': x_sample, 'cache_k': cache_k, 'cache_v': cache_v,
            'page_table': page_table, 'state_conv_a': state_conv_a, 'state_conv_c': state_conv_c,
            'state_pool_d': state_pool_d, 'norm_g': norm_g, 'w_in': w_in, 'conv_a_w': conv_a_w,
            'conv_c_w': conv_c_w, 'conv_c_b': conv_c_b, 'ln_c_g': ln_c_g, 'ln_c_b': ln_c_b,
            'w_pw_c': w_pw_c, 'w_pool': w_pool, 'pool_scale': pool_scale, 'w_out': w_out,
            'final_norm_g': final_norm_g}


def reference(x_prompt, x_sample, cache_k, cache_v, page_table, state_conv_a, state_conv_c,
              state_pool_d, norm_g, w_in, conv_a_w, conv_c_w, conv_c_b, ln_c_g, ln_c_b,
              w_pw_c, w_pool, pool_scale, w_out, final_norm_g):
    dt = x_prompt.dtype
    yp, ys = x_prompt, x_sample
    kp_l, vp_l, ks_l, vs_l = [], [], [], []
    ap_l, as_l, cp_l, cs_l, dp_l, ds_l = [], [], [], [], [], []
    zk = jnp.zeros((BATCH, 0, N_HEADS_B, HEAD_DIM_B), dt)
    za = jnp.zeros((BATCH, CONV_A_WIDTH - 1, W_A), dt)
    zc = jnp.zeros((BATCH, CONV_C_WIDTH - 1, W_C), dt)
    zd = jnp.zeros((BATCH, POOL_STATE, W_D), dt)
    for l in range(DEPTH):
        params = (norm_g[l], w_in[l], conv_a_w[l], conv_c_w[l], conv_c_b[l], ln_c_g[l], ln_c_b[l],
                  w_pw_c[l], w_pool[l], pool_scale[l], w_out[l])
        yp, kp, vp, ap, cp, dp = mixer_layer(yp, 0, zk, zk, za, zc, zd, *params)
        k_past = cache_k[l][page_table].reshape(DEC_BATCH, -1, N_HEADS_B, HEAD_DIM_B)
        v_past = cache_v[l][page_table].reshape(DEC_BATCH, -1, N_HEADS_B, HEAD_DIM_B)
        ys, ks, vs, as_, cs, ds = mixer_layer(ys, PAST_LEN, k_past, v_past, state_conv_a[l],
                                              state_conv_c[l], state_pool_d[l], *params)
        kp_l.append(kp); vp_l.append(vp); ks_l.append(ks); vs_l.append(vs)
        ap_l.append(ap); as_l.append(as_); cp_l.append(cp); cs_l.append(cs)
        dp_l.append(dp); ds_l.append(ds)
    y_prompt = rmsnorm(yp, final_norm_g)
    y_sample = rmsnorm(ys, final_norm_g)
    new_k_prompt = jnp.stack(kp_l)
    new_v_prompt = jnp.stack(vp_l)
    new_k_sample = jnp.stack(ks_l)
    new_v_sample = jnp.stack(vs_l)
    new_conv_a_prompt = jnp.stack(ap_l)
    new_conv_a_sample = jnp.stack(as_l)
    new_conv_c_prompt = jnp.stack(cp_l)
    new_conv_c_sample = jnp.stack(cs_l)
    new_pool_d_prompt = jnp.stack(dp_l)
    new_pool_d_sample = jnp.stack(ds_l)
    return (y_prompt, y_sample, new_k_prompt, new_v_prompt, new_k_sample, new_v_sample,
            new_conv_a_prompt, new_conv_a_sample, new_conv_c_prompt, new_conv_c_sample,
            new_pool_d_prompt, new_pool_d_sample)
```

```python
import functools

import jax
import jax.numpy as jnp
from jax import lax
from jax.experimental import pallas as pl
from jax.experimental.pallas import tpu as pltpu

F32 = jnp.float32
BF16 = jnp.bfloat16

WIDTH = 256
N_HEADS = 4
HEAD_DIM = 64
ROT_HALF = 8
ROPE_THETA = 500000.0
MOBA_BLOCK = 256
MOBA_TOPK = 3
CONV_A_WIDTH = 3
CONV_C_WIDTH = 31
POOL_WINDOWS = (2, 4, 8, 16)
POOL_STATE = 15
RMS_EPS = 1e-6
LN_EPS = 1e-5
HALO = 32
DEC_ROWS = 8
NEG_INF = float("-inf")
VMEM_LIMIT = 56 * 1024 * 1024


def _silu(z):
    return z * jax.nn.sigmoid(z)


def _rmsnorm(x, g):
    ms = jnp.mean(x * x, axis=-1, keepdims=True)
    return x * lax.rsqrt(ms + RMS_EPS) * g


def _out_proj(x, mixed, y_b, wout_ref):
    yb = (y_b * mixed[:, WIDTH:2 * WIDTH]).astype(BF16)
    o = x + jnp.dot(mixed[:, 0:WIDTH].astype(BF16), wout_ref[0:WIDTH, :], preferred_element_type=F32)
    o = o + jnp.dot(yb, wout_ref[WIDTH:2 * WIDTH, :], preferred_element_type=F32)
    o = o + jnp.dot(mixed[:, 2 * WIDTH:4 * WIDTH].astype(BF16), wout_ref[2 * WIDTH:4 * WIDTH, :],
                    preferred_element_type=F32)
    return o


def _pre_body(x_ref, ng_ref, win_ref, caw_ref, ccw_ref, ccb_ref, lng_ref, lnb_ref, wpw_ref, wpool_ref,
              psc_ref, rc_ref, rsa_ref, rsb_ref, prea_ref, prec_ref, pred_ref, *refs,
              tm, last_rows, pos0, prompt):
    if prompt:
        (mixed_ref, k_ref, v_ref, qt_ref, kb_ref, vt_ref, km_ref,
         na_ref, nc_ref, nd_ref, bufa, bufc, bufd) = refs
    else:
        (mixed_ref, k_ref, v_ref, q_ref, na_ref, nc_ref, nd_ref, bufa, bufc, bufd) = refs
    t = pl.program_id(1)
    nt = pl.num_programs(1)

    @pl.when(t == 0)
    def _():
        bufa[0:HALO, :] = prea_ref[0]
        bufc[0:HALO, :] = prec_ref[0]
        bufd[0:HALO, :] = pred_ref[0]

    x = x_ref[0]
    h = _rmsnorm(x, ng_ref[...]).astype(BF16)

    def proj(c0, n):
        return jnp.dot(h, win_ref[:, c0 * WIDTH:(c0 + n) * WIDTH], preferred_element_type=F32)

    pa = proj(0, 4)
    bufa[HALO:HALO + tm, :] = pa[:, 2 * WIDTH:3 * WIDTH] * pa[:, 0:WIDTH]
    conv = None
    for j in range(CONV_A_WIDTH):
        off = HALO - (CONV_A_WIDTH - 1) + j
        term = caw_ref[j:j + 1, :] * bufa[off:off + tm, :]
        conv = term if conv is None else conv + term
    mixed_ref[0, :, 0:WIDTH] = pa[:, WIDTH:2 * WIDTH] * conv * _silu(pa[:, 3 * WIDTH:4 * WIDTH])

    pb = proj(4, 4)
    rc, rsa, rsb = rc_ref[...], rsa_ref[...], rsb_ref[...]

    def rope(z):
        return z * rc + pltpu.roll(z, WIDTH - ROT_HALF, 1) * rsa + pltpu.roll(z, ROT_HALF, 1) * rsb

    qs = rope(pb[:, 0:WIDTH]) * (HEAD_DIM ** -0.5)
    kr = rope(pb[:, WIDTH:2 * WIDTH])
    v = pb[:, 2 * WIDTH:3 * WIDTH]
    k_ref[0] = kr
    v_ref[0] = v
    mixed_ref[0, :, WIDTH:2 * WIDTH] = _silu(pb[:, 3 * WIDTH:4 * WIDTH])
    if prompt:
        qt_ref[0, 0] = qs.T
        kb_ref[0, 0] = kr.astype(BF16)
        vt_ref[0, 0] = v.T.astype(BF16)
        km_ref[0, 0] = jnp.mean(kr, axis=0, keepdims=True)
    else:
        q_ref[0] = qs

    pc = proj(8, 3)
    bufc[HALO:HALO + tm, :] = pc[:, 0:WIDTH] * jax.nn.sigmoid(pc[:, WIDTH:2 * WIDTH])
    acc = None
    for j in range(CONV_C_WIDTH):
        off = HALO - (CONV_C_WIDTH - 1) + j
        term = ccw_ref[j:j + 1, :] * bufc[off:off + tm, :]
        acc = term if acc is None else acc + term
    acc = acc + ccb_ref[...]
    mu = jnp.mean(acc, axis=-1, keepdims=True)
    cen = acc - mu
    var = jnp.mean(cen * cen, axis=-1, keepdims=True)
    ln = cen * lax.rsqrt(var + LN_EPS) * lng_ref[...] + lnb_ref[...]
    yc = jnp.dot(_silu(ln).astype(BF16), wpw_ref[...], preferred_element_type=F32)
    mixed_ref[0, :, 2 * WIDTH:3 * WIDTH] = yc * _silu(pc[:, 2 * WIDTH:3 * WIDTH])

    pd = proj(11, 2)
    dx = pd[:, 0:WIDTH]
    bufd[HALO:HALO + tm, :] = dx
    run = dx
    sums = {}
    for j in range(1, POOL_WINDOWS[-1]):
        run = run + bufd[HALO - j:HALO - j + tm, :]
        if j + 1 in POOL_WINDOWS:
            sums[j + 1] = run
    group = lax.broadcasted_iota(jnp.int32, (tm, WIDTH), 1) // (WIDTH // len(POOL_WINDOWS))
    pos = pos0 + t * tm + lax.broadcasted_iota(jnp.int32, (tm, WIDTH), 0)
    wsum = sums[POOL_WINDOWS[-1]]
    win = jnp.full((tm, WIDTH), POOL_WINDOWS[-1], jnp.int32)
    for g in range(len(POOL_WINDOWS) - 2, -1, -1):
        wsum = jnp.where(group == g, sums[POOL_WINDOWS[g]], wsum)
        win = jnp.where(group == g, POOL_WINDOWS[g], win)
    cnt = jnp.minimum(pos + 1, win).astype(F32)
    diff = wsum / cnt - dx
    yd = jnp.dot(diff.astype(BF16), wpool_ref[...], preferred_element_type=F32) * psc_ref[...]
    mixed_ref[0, :, 3 * WIDTH:4 * WIDTH] = yd * _silu(pd[:, WIDTH:2 * WIDTH])

    @pl.when(t == nt - 1)
    def _():
        end = HALO + last_rows
        na_ref[0] = bufa[end - (CONV_A_WIDTH - 1):end, :]
        nc_ref[0] = bufc[end - (CONV_C_WIDTH - 1):end, :]
        nd_ref[0] = bufd[end - POOL_STATE:end, :]

    if tm >= HALO:
        @pl.when(t < nt - 1)
        def _():
            bufa[0:HALO, :] = bufa[tm:tm + HALO, :]
            bufc[0:HALO, :] = bufc[tm:tm + HALO, :]
            bufd[0:HALO, :] = bufd[tm:tm + HALO, :]


def _pre_call(x, lw, rope_tabs, pre_a, pre_c, pre_d, *, tm, last_rows, pos0, prompt):
    bsz, t_len, d_model = x.shape
    nt = t_len // tm
    assert nt * tm == t_len and (nt == 1 or tm >= HALO)
    rc, rsa, rsb = rope_tabs
    d_in = lw["w_in"].shape[1]

    def const(shape):
        return pl.BlockSpec(shape, lambda b, t: (0,) * len(shape))

    def per_b(shape):
        return pl.BlockSpec((1,) + shape, lambda b, t: (b,) + (0,) * len(shape))

    tok = lambda w: pl.BlockSpec((1, tm, w), lambda b, t: (b, t, 0))
    tab = pl.BlockSpec((tm, WIDTH), lambda b, t: (t, 0))
    in_specs = [tok(d_model), const((1, d_model)), const((d_model, d_in)),
                const((CONV_A_WIDTH, WIDTH)), const((CONV_C_WIDTH, WIDTH)), const((1, WIDTH)),
                const((1, WIDTH)), const((1, WIDTH)), const((WIDTH, WIDTH)), const((WIDTH, WIDTH)),
                const((1, WIDTH)), tab, tab, tab,
                per_b((HALO, WIDTH)), per_b((HALO, WIDTH)), per_b((HALO, WIDTH))]
    out_shape = [jax.ShapeDtypeStruct((bsz, t_len, 4 * WIDTH), F32),
                 jax.ShapeDtypeStruct((bsz, t_len, WIDTH), F32),
                 jax.ShapeDtypeStruct((bsz, t_len, WIDTH), F32)]
    out_specs = [tok(4 * WIDTH), tok(WIDTH), tok(WIDTH)]
    if prompt:
        assert tm == MOBA_BLOCK
        blk = lambda r, c: pl.BlockSpec((1, 1, r, c), lambda b, t: (b, t, 0, 0))
        out_shape += [jax.ShapeDtypeStruct((bsz, nt, WIDTH, tm), F32),
                      jax.ShapeDtypeStruct((bsz, nt, tm, WIDTH), BF16),
                      jax.ShapeDtypeStruct((bsz, nt, WIDTH, tm), BF16),
                      jax.ShapeDtypeStruct((bsz, nt, 1, WIDTH), F32)]
        out_specs += [blk(WIDTH, tm), blk(tm, WIDTH), blk(WIDTH, tm), blk(1, WIDTH)]
    else:
        out_shape += [jax.ShapeDtypeStruct((bsz, t_len, WIDTH), F32)]
        out_specs += [tok(WIDTH)]
    out_shape += [jax.ShapeDtypeStruct((bsz, CONV_A_WIDTH - 1, WIDTH), F32),
                  jax.ShapeDtypeStruct((bsz, CONV_C_WIDTH - 1, WIDTH), F32),
                  jax.ShapeDtypeStruct((bsz, POOL_STATE, WIDTH), F32)]
    out_specs += [per_b((CONV_A_WIDTH - 1, WIDTH)), per_b((CONV_C_WIDTH - 1, WIDTH)),
                  per_b((POOL_STATE, WIDTH))]
    body = functools.partial(_pre_body, tm=tm, last_rows=last_rows, pos0=pos0, prompt=prompt)
    return pl.pallas_call(
        body,
        grid=(bsz, nt),
        in_specs=in_specs,
        out_specs=out_specs,
        out_shape=out_shape,
        scratch_shapes=[pltpu.VMEM((HALO + tm, WIDTH), F32)] * 3,
        compiler_params=pltpu.CompilerParams(dimension_semantics=("arbitrary", "arbitrary"),
                                             vmem_limit_bytes=VMEM_LIMIT),
        name="pre_prompt" if prompt else "pre_sample",
    )(x, lw["norm_g"], lw["w_in"], lw["conv_a_w"], lw["conv_c_w"], lw["conv_c_b"], lw["ln_c_g"],
      lw["ln_c_b"], lw["w_pw_c"], lw["w_pool"], lw["pool_scale"], rc, rsa, rsb, pre_a, pre_c, pre_d)


def _top_blocks_bias(gate, index, n_index, axis):
    bias = jnp.full(gate.shape, NEG_INF, F32)
    for _ in range(MOBA_TOPK):
        top = jnp.max(gate, axis=axis, keepdims=True)
        cand = jnp.logical_and(gate == top, top > NEG_INF)
        first = jnp.min(jnp.where(cand, index, n_index), axis=axis, keepdims=True)
        pick = index == first
        bias = jnp.where(pick, 0.0, bias)
        gate = jnp.where(pick, NEG_INF, gate)
    return bias


def _attn_body(qt_ref, kb_ref, vt_ref, km_ref, mixed_ref, x_ref, wout_ref, fng_ref, out_ref,
               qm_s, bias_s, acc_s, *, nblk, tq, final):
    i = pl.program_id(1)
    qt = qt_ref[0, 0]
    row_head = lax.broadcasted_iota(jnp.int32, (WIDTH, tq), 0) // HEAD_DIM
    km = km_ref[0]
    lane_head = lax.broadcasted_iota(jnp.int32, (nblk, WIDTH), 1) // HEAD_DIM
    blk = lax.broadcasted_iota(jnp.int32, (nblk, tq), 0)
    for h in range(N_HEADS):
        cols = slice(h * tq, (h + 1) * tq)
        qm_s[:, cols] = jnp.where(row_head == h, qt, 0.0).astype(BF16)
        gate = jnp.dot(jnp.where(lane_head == h, km, 0.0), qt, precision=lax.Precision.HIGHEST,
                       preferred_element_type=F32)
        gate = jnp.where(blk < i, gate, NEG_INF)
        bias_s[:, cols] = _top_blocks_bias(gate, blk, nblk, 0)

    qm = qm_s[...]

    def scores(j):
        return jnp.dot(kb_ref[0, j], qm, preferred_element_type=F32)

    def pv(j, p, scale):
        pb = p.astype(BF16)
        vt = vt_ref[0, j]
        for h in range(N_HEADS):
            rows = slice(h * HEAD_DIM, (h + 1) * HEAD_DIM)
            cols = slice(h * tq, (h + 1) * tq)
            o = jnp.dot(vt[rows, :], pb[:, cols], preferred_element_type=F32)
            acc_s[rows, :] = o if scale is None else scale[:, cols] * acc_s[rows, :] + o

    s = scores(i)
    key = lax.broadcasted_iota(jnp.int32, (MOBA_BLOCK, N_HEADS * tq), 0)
    qry = lax.broadcasted_iota(jnp.int32, (MOBA_BLOCK, N_HEADS * tq), 1) % tq
    s = jnp.where(key <= qry, s, NEG_INF)
    m = jnp.max(s, axis=0, keepdims=True)
    p = jnp.exp(s - m)
    l = jnp.sum(p, axis=0, keepdims=True)
    pv(i, p, None)

    def body(j, carry):
        m, l = carry
        s = scores(j) + bias_s[pl.ds(j, 1), :]
        m_new = jnp.maximum(m, jnp.max(s, axis=0, keepdims=True))
        alpha = jnp.exp(m - m_new)
        p = jnp.exp(s - m_new)
        l = alpha * l + jnp.sum(p, axis=0, keepdims=True)
        pv(j, p, alpha)
        return m_new, l

    m, l = lax.fori_loop(0, i, body, (m, l))
    inv = 1.0 / l
    for h in range(N_HEADS):
        rows = slice(h * HEAD_DIM, (h + 1) * HEAD_DIM)
        acc_s[rows, :] = acc_s[rows, :] * inv[:, h * tq:(h + 1) * tq]
    y_b = acc_s[...].T
    o = _out_proj(x_ref[0], mixed_ref[0], y_b, wout_ref)
    if final:
        o = _rmsnorm(o, fng_ref[...])
    out_ref[0] = o


def _attn_call(qt, kb, vt, km, mixed, x, w_out, fng, *, final):
    bsz, nblk, _, tq = qt.shape
    d_model = x.shape[-1]
    assert tq == MOBA_BLOCK
    body = functools.partial(_attn_body, nblk=nblk, tq=tq, final=final)
    whole = lambda r, c: pl.BlockSpec((1, nblk, r, c), lambda b, i: (b, 0, 0, 0))
    return pl.pallas_call(
        body,
        grid=(bsz, nblk),
        in_specs=[pl.BlockSpec((1, 1, WIDTH, tq), lambda b, i: (b, i, 0, 0)),
                  whole(MOBA_BLOCK, WIDTH), whole(WIDTH, MOBA_BLOCK),
                  pl.BlockSpec((1, nblk, WIDTH), lambda b, i: (b, 0, 0)),
                  pl.BlockSpec((1, tq, 4 * WIDTH), lambda b, i: (b, i, 0)),
                  pl.BlockSpec((1, tq, d_model), lambda b, i: (b, i, 0)),
                  pl.BlockSpec((4 * WIDTH, d_model), lambda b, i: (0, 0)),
                  pl.BlockSpec((1, d_model), lambda b, i: (0, 0))],
        out_specs=pl.BlockSpec((1, tq, d_model), lambda b, i: (b, i, 0)),
        out_shape=jax.ShapeDtypeStruct(x.shape, F32),
        scratch_shapes=[pltpu.VMEM((WIDTH, N_HEADS * tq), BF16),
                        pltpu.VMEM((nblk, N_HEADS * tq), F32),
                        pltpu.VMEM((WIDTH, tq), F32)],
        compiler_params=pltpu.CompilerParams(dimension_semantics=("arbitrary", "arbitrary"),
                                             vmem_limit_bytes=VMEM_LIMIT),
        name="attn_prompt",
    )(qt, kb, vt, km.reshape(bsz, nblk, WIDTH), mixed, x, w_out, fng)


def _dec_body(pt_ref, *refs, pps, ns, t_valid, final):
    del pt_ref
    kpages, vpages = refs[0:pps], refs[pps:2 * pps]
    (q_ref, kn_ref, vn_ref, mixed_ref, x_ref, wout_ref, fng_ref, out_ref,
     qm_s, mp_s, lp_s, gp_s, op_s) = refs[2 * pps:]
    step = pl.program_id(1)
    rows = N_HEADS * DEC_ROWS
    lane_head = lax.broadcasted_iota(jnp.int32, (DEC_ROWS, WIDTH), 1) // HEAD_DIM
    lane = lax.broadcasted_iota(jnp.int32, (rows, 128), 1)

    @pl.when(step == 0)
    def _():
        q8 = q_ref[0]
        for h in range(N_HEADS):
            qm_s[h * DEC_ROWS:(h + 1) * DEC_ROWS, :] = jnp.where(lane_head == h, q8, 0.0)
        mp_s[...] = jnp.full((rows, 128), NEG_INF, F32)
        gp_s[...] = jnp.full((rows, 128), NEG_INF, F32)
        lp_s[...] = jnp.zeros((rows, 128), F32)

    qm = qm_s[...]
    qmb = qm.astype(BF16)
    contract_last = (((1,), (1,)), ((), ()))
    for c in range(pps // 2):
        k0, k1 = kpages[2 * c][...], kpages[2 * c + 1][...]
        kmean = (jnp.sum(k0, axis=0, keepdims=True) + jnp.sum(k1, axis=0, keepdims=True)) * (1.0 / MOBA_BLOCK)
        gate = jnp.sum(qm * kmean, axis=1, keepdims=True)
        kblk = jnp.concatenate([k0, k1], axis=0).astype(BF16)
        vblk = jnp.concatenate([vpages[2 * c][...], vpages[2 * c + 1][...]], axis=0).astype(BF16)
        s = lax.dot_general(qmb, kblk, contract_last, preferred_element_type=F32)
        m = jnp.max(s, axis=1, keepdims=True)
        p = jnp.exp(s - m)
        l = jnp.sum(p, axis=1, keepdims=True)
        j = step * (pps // 2) + c
        mp_s[...] = jnp.where(lane == j, m, mp_s[...])
        lp_s[...] = jnp.where(lane == j, l, lp_s[...])
        gp_s[...] = jnp.where(lane == j, gate, gp_s[...])
        op_s[j] = jnp.dot(p.astype(BF16), vblk, preferred_element_type=F32)

    @pl.when(step == ns - 1)
    def _():
        nblk = ns * (pps // 2)
        sel = _top_blocks_bias(gp_s[...], lane, 128, 1) == 0.0
        zpad = jnp.zeros((128 - DEC_ROWS, WIDTH), F32)
        kn = jnp.concatenate([kn_ref[0], zpad], axis=0).astype(BF16)
        vn = jnp.concatenate([vn_ref[0], zpad], axis=0).astype(BF16)
        s = lax.dot_general(qmb, kn, contract_last, preferred_element_type=F32)
        qpos = jnp.minimum(lax.broadcasted_iota(jnp.int32, (rows, 128), 0) % DEC_ROWS, t_valid - 1)
        s = jnp.where(lane <= qpos, s, NEG_INF)
        m_own = jnp.max(s, axis=1, keepdims=True)
        p = jnp.exp(s - m_own)
        l_own = jnp.sum(p, axis=1, keepdims=True)
        o_own = jnp.dot(p.astype(BF16), vn, preferred_element_type=F32)
        mp = mp_s[...]
        m_fin = jnp.maximum(m_own, jnp.max(jnp.where(sel, mp, NEG_INF), axis=1, keepdims=True))
        w = jnp.where(sel, jnp.exp(mp - m_fin), 0.0)
        a_own = jnp.exp(m_own - m_fin)
        l_fin = a_own * l_own + jnp.sum(w * lp_s[...], axis=1, keepdims=True)
        o_fin = a_own * o_own
        for j in range(nblk):
            o_fin = o_fin + w[:, j:j + 1] * op_s[j]
        y = o_fin / l_fin
        y_b = None
        for h in range(N_HEADS):
            part = jnp.where(lane_head == h, y[h * DEC_ROWS:(h + 1) * DEC_ROWS, :], 0.0)
            y_b = part if y_b is None else y_b + part
        o = _out_proj(x_ref[0], mixed_ref[0], y_b, wout_ref)
        if final:
            o = _rmsnorm(o, fng_ref[...])
        out_ref[0] = o


def _dec_call(page_table, cache_k, cache_v, layer, q, kn, vn, mixed, x, w_out, fng, *, t_valid, final):
    bsz, n_pages = page_table.shape
    depth, n_pool, page, _, _ = cache_k.shape
    d_model = x.shape[-1]
    assert (n_pages * page) % MOBA_BLOCK == 0 and MOBA_BLOCK == 2 * page
    nblk = n_pages * page // MOBA_BLOCK
    assert MOBA_TOPK <= nblk <= 128
    pps = 8 if n_pages % 8 == 0 else 2
    ns = n_pages // pps
    ck = cache_k.reshape(depth, n_pool, page, WIDTH)
    cv = cache_v.reshape(depth, n_pool, page, WIDTH)

    def page_spec(c):
        return pl.BlockSpec((None, None, page, WIDTH), lambda b, s, pt: (layer, pt[b, s * pps + c], 0, 0))

    per_b = lambda w: pl.BlockSpec((1, DEC_ROWS, w), lambda b, s, pt: (b, 0, 0))
    rows = N_HEADS * DEC_ROWS
    body = functools.partial(_dec_body, pps=pps, ns=ns, t_valid=t_valid, final=final)
    grid_spec = pltpu.PrefetchScalarGridSpec(
        num_scalar_prefetch=1,
        grid=(bsz, ns),
        in_specs=[page_spec(c) for c in range(pps)] * 2
        + [per_b(WIDTH), per_b(WIDTH), per_b(WIDTH), per_b(4 * WIDTH), per_b(d_model),
           pl.BlockSpec((4 * WIDTH, d_model), lambda b, s, pt: (0, 0)),
           pl.BlockSpec((1, d_model), lambda b, s, pt: (0, 0))],
        out_specs=per_b(d_model),
        scratch_shapes=[pltpu.VMEM((rows, WIDTH), F32), pltpu.VMEM((rows, 128), F32),
                        pltpu.VMEM((rows, 128), F32), pltpu.VMEM((rows, 128), F32),
                        pltpu.VMEM((nblk, rows, WIDTH), F32)],
    )
    return pl.pallas_call(
        body,
        grid_spec=grid_spec,
        out_shape=jax.ShapeDtypeStruct(x.shape, F32),
        compiler_params=pltpu.CompilerParams(dimension_semantics=("arbitrary", "arbitrary"),
                                             vmem_limit_bytes=VMEM_LIMIT),
        name="attn_sample",
    )(page_table, *([ck] * pps), *([cv] * pps), q, kn, vn, mixed, x, w_out, fng)


def _rope_tables(pos0, n):
    freqs = ROPE_THETA ** (-jnp.arange(ROT_HALF, dtype=F32) * 2.0 / (2 * ROT_HALF))
    ang = (pos0 + jnp.arange(n, dtype=jnp.int32)).astype(F32)[:, None] * freqs[None, :]
    cos, sin = jnp.cos(ang), jnp.sin(ang)
    one = jnp.ones((n, HEAD_DIM - 2 * ROT_HALF), F32)
    zero8 = jnp.zeros((n, ROT_HALF), F32)
    zero_rest = jnp.zeros((n, HEAD_DIM - 2 * ROT_HALF), F32)
    rc = jnp.concatenate([cos, cos, one], axis=1)
    rsa = jnp.concatenate([-sin, zero8, zero_rest], axis=1)
    rsb = jnp.concatenate([zero8, sin, zero_rest], axis=1)
    return tuple(jnp.tile(a, (1, N_HEADS)) for a in (rc, rsa, rsb))


def _pad_state(state):
    return jnp.pad(state, ((0, 0), (HALO - state.shape[1], 0), (0, 0)))


def kernel(x_prompt, x_sample, cache_k, cache_v, page_table, state_conv_a, state_conv_c, state_pool_d,
           norm_g, w_in, conv_a_w, conv_c_w, conv_c_b, ln_c_g, ln_c_b, w_pw_c, w_pool, pool_scale, w_out,
           final_norm_g):
    depth = w_in.shape[0]
    bsz, seq, d_model = x_prompt.shape
    dec_b, dec_t, _ = x_sample.shape
    past_len = page_table.shape[1] * cache_k.shape[2]
    assert seq % MOBA_BLOCK == 0 and dec_t <= DEC_ROWS and WIDTH == N_HEADS * HEAD_DIM

    tabs_p = _rope_tables(0, seq)
    tabs_s = _rope_tables(past_len, DEC_ROWS)
    zero_state = jnp.zeros((bsz, HALO, WIDTH), F32)
    fng = final_norm_g.reshape(1, d_model)
    yp = x_prompt
    ys = jnp.pad(x_sample, ((0, 0), (0, DEC_ROWS - dec_t), (0, 0)))
    outs = [[] for _ in range(10)]
    n_groups = w_pool.shape[1]
    for l in range(depth):
        pool_bd = jax.scipy.linalg.block_diag(*[w_pool[l, g] for g in range(n_groups)])
        lw = {"norm_g": norm_g[l].reshape(1, d_model), "w_in": w_in[l].astype(BF16),
              "conv_a_w": conv_a_w[l], "conv_c_w": conv_c_w[l], "conv_c_b": conv_c_b[l].reshape(1, WIDTH),
              "ln_c_g": ln_c_g[l].reshape(1, WIDTH), "ln_c_b": ln_c_b[l].reshape(1, WIDTH),
              "w_pw_c": w_pw_c[l].astype(BF16), "w_pool": pool_bd.astype(BF16),
              "pool_scale": pool_scale[l].reshape(1, WIDTH)}
        wo = w_out[l].astype(BF16)
        final = l == depth - 1

        mixed, kp, vp, qt, kb, vt, km, ap, cp, dp = _pre_call(
            yp, lw, tabs_p, zero_state, zero_state, zero_state,
            tm=MOBA_BLOCK, last_rows=MOBA_BLOCK, pos0=0, prompt=True)
        yp = _attn_call(qt, kb, vt, km, mixed, yp, wo, fng, final=final)

        mixed_s, ks, vs, qs, as_, cs, ds = _pre_call(
            ys, lw, tabs_s, _pad_state(state_conv_a[l]), _pad_state(state_conv_c[l]),
            _pad_state(state_pool_d[l]), tm=DEC_ROWS, last_rows=dec_t, pos0=past_len, prompt=False)
        ys = _dec_call(page_table, cache_k, cache_v, l, qs, ks, vs, mixed_s, ys, wo, fng,
                       t_valid=dec_t, final=final)

        for lst, val in zip(outs, (kp.reshape(bsz, seq, N_HEADS, HEAD_DIM),
                                   vp.reshape(bsz, seq, N_HEADS, HEAD_DIM),
                                   ks[:, :dec_t].reshape(dec_b, dec_t, N_HEADS, HEAD_DIM),
                                   vs[:, :dec_t].reshape(dec_b, dec_t, N_HEADS, HEAD_DIM),
                                   ap, as_, cp, cs, dp, ds)):
            lst.append(val)
    return (yp, ys[:, :dec_t]) + tuple(jnp.stack(o) for o in outs)
```

```python
import functools

import jax
import jax.numpy as jnp
from jax import lax
from jax.experimental import pallas as pl
from jax.experimental.pallas import tpu as pltpu

F32 = jnp.float32
BF16 = jnp.bfloat16

WIDTH = 256
N_HEADS = 4
HEAD_DIM = 64
ROT_HALF = 8
ROPE_THETA = 500000.0
MOBA_BLOCK = 256
MOBA_TOPK = 3
CONV_A_WIDTH = 3
CONV_C_WIDTH = 31
POOL_WINDOWS = (2, 4, 8, 16)
POOL_STATE = 15
RMS_EPS = 1e-6
LN_EPS = 1e-5
HALO = 32
DEC_ROWS = 8
NEG_INF = float("-inf")
LOG2_E = 1.4426950408889634
VMEM_LIMIT = 56 * 1024 * 1024


def _silu(z):
    return z * jax.nn.sigmoid(z)


def _rmsnorm(x, g):
    ms = jnp.mean(x * x, axis=-1, keepdims=True)
    return x * lax.rsqrt(ms + RMS_EPS) * g


def _out_proj(x, mixed, y_b, wout_ref):
    yb = (y_b * mixed[:, WIDTH:2 * WIDTH]).astype(BF16)
    o = x + jnp.dot(mixed[:, 0:WIDTH].astype(BF16), wout_ref[0:WIDTH, :], preferred_element_type=F32)
    o = o + jnp.dot(yb, wout_ref[WIDTH:2 * WIDTH, :], preferred_element_type=F32)
    o = o + jnp.dot(mixed[:, 2 * WIDTH:4 * WIDTH].astype(BF16), wout_ref[2 * WIDTH:4 * WIDTH, :],
                    preferred_element_type=F32)
    return o


def _pre_body(x_ref, ng_ref, win_ref, caw_ref, ccw_ref, ccb_ref, lng_ref, lnb_ref, wpw_ref, wpool_ref,
              psc_ref, rc_ref, rsa_ref, rsb_ref, prea_ref, prec_ref, pred_ref, *refs,
              tm, last_rows, pos0, prompt):
    if prompt:
        (mixed_ref, k_ref, v_ref, qt_ref, kb_ref, vt_ref, km_ref,
         na_ref, nc_ref, nd_ref, bufa, bufc, bufd, shc, x2, x4, x8) = refs
    else:
        (mixed_ref, k_ref, v_ref, q_ref, na_ref, nc_ref, nd_ref, bufa, bufc, bufd, shc, x2, x4, x8) = refs
    t = pl.program_id(1)
    nt = pl.num_programs(1)
    span = tm + HALO - 8

    @pl.when(t == 0)
    def _():
        bufa[0:HALO, :] = prea_ref[0]
        bufc[0:HALO, :] = prec_ref[0]
        bufd[0:HALO, :] = pred_ref[0]

    x = x_ref[0]
    h = _rmsnorm(x, ng_ref[...]).astype(BF16)

    def proj(c0, n):
        return jnp.dot(h, win_ref[:, c0 * WIDTH:(c0 + n) * WIDTH], preferred_element_type=F32)

    pa = proj(0, 4)
    bufa[HALO:HALO + tm, :] = pa[:, 2 * WIDTH:3 * WIDTH] * pa[:, 0:WIDTH]
    conv = None
    for j in range(CONV_A_WIDTH):
        off = HALO - (CONV_A_WIDTH - 1) + j
        term = caw_ref[j:j + 1, :] * bufa[off:off + tm, :]
        conv = term if conv is None else conv + term
    mixed_ref[0, :, 0:WIDTH] = pa[:, WIDTH:2 * WIDTH] * conv * _silu(pa[:, 3 * WIDTH:4 * WIDTH])

    pb = proj(4, 4)
    rc, rsa, rsb = rc_ref[...], rsa_ref[...], rsb_ref[...]

    def rope(z):
        return z * rc + pltpu.roll(z, WIDTH - ROT_HALF, 1) * rsa + pltpu.roll(z, ROT_HALF, 1) * rsb

    qs = rope(pb[:, 0:WIDTH]) * (HEAD_DIM ** -0.5)
    kr = rope(pb[:, WIDTH:2 * WIDTH])
    v = pb[:, 2 * WIDTH:3 * WIDTH]
    k_ref[0] = kr
    v_ref[0] = v
    mixed_ref[0, :, WIDTH:2 * WIDTH] = _silu(pb[:, 3 * WIDTH:4 * WIDTH])
    if prompt:
        qt_ref[0, 0] = qs.T
        kb_ref[0, 0] = kr.astype(BF16)
        vt_ref[0, 0] = v.T.astype(BF16)
        km_ref[0, 0] = jnp.mean(kr, axis=0, keepdims=True)
    else:
        q_ref[0] = qs

    pc = proj(8, 3)
    bufc[HALO:HALO + tm, :] = pc[:, 0:WIDTH] * jax.nn.sigmoid(pc[:, WIDTH:2 * WIDTH])
    for r in range(1, 8):
        shc[r - 1] = bufc[r:r + span, :]
    acc = None
    for j in range(CONV_C_WIDTH):
        a, r = divmod(HALO - (CONV_C_WIDTH - 1) + j, 8)
        rows = bufc[8 * a:8 * a + tm, :] if r == 0 else shc[r - 1, 8 * a:8 * a + tm, :]
        term = ccw_ref[j:j + 1, :] * rows
        acc = term if acc is None else acc + term
    acc = acc + ccb_ref[...]
    mu = jnp.mean(acc, axis=-1, keepdims=True)
    cen = acc - mu
    var = jnp.mean(cen * cen, axis=-1, keepdims=True)
    ln = cen * lax.rsqrt(var + LN_EPS) * lng_ref[...] + lnb_ref[...]
    yc = jnp.dot(_silu(ln).astype(BF16), wpw_ref[...], preferred_element_type=F32)
    mixed_ref[0, :, 2 * WIDTH:3 * WIDTH] = yc * _silu(pc[:, 2 * WIDTH:3 * WIDTH])

    pd = proj(11, 2)
    dx = pd[:, 0:WIDTH]
    bufd[HALO:HALO + tm, :] = dx
    assert POOL_WINDOWS == (2, 4, 8, 16) and HALO == 32
    x2[...] = bufd[8:8 + span, :] + bufd[7:7 + span, :]
    x4[0:span - 8, :] = x2[8:span, :] + x2[6:span - 2, :]
    x8[0:span - 16, :] = x4[8:span - 8, :] + x4[4:span - 12, :]
    sums = {2: x2[24:24 + tm, :], 4: x4[16:16 + tm, :], 8: x8[8:8 + tm, :],
            16: x8[8:8 + tm, :] + x8[0:tm, :]}
    group = lax.broadcasted_iota(jnp.int32, (tm, WIDTH), 1) // (WIDTH // len(POOL_WINDOWS))
    pos = pos0 + t * tm + lax.broadcasted_iota(jnp.int32, (tm, WIDTH), 0)
    wsum = sums[POOL_WINDOWS[-1]]
    win = jnp.full((tm, WIDTH), POOL_WINDOWS[-1], jnp.int32)
    for g in range(len(POOL_WINDOWS) - 2, -1, -1):
        wsum = jnp.where(group == g, sums[POOL_WINDOWS[g]], wsum)
        win = jnp.where(group == g, POOL_WINDOWS[g], win)
    cnt = jnp.minimum(pos + 1, win).astype(F32)
    diff = wsum / cnt - dx
    yd = jnp.dot(diff.astype(BF16), wpool_ref[...], preferred_element_type=F32) * psc_ref[...]
    mixed_ref[0, :, 3 * WIDTH:4 * WIDTH] = yd * _silu(pd[:, WIDTH:2 * WIDTH])

    @pl.when(t == nt - 1)
    def _():
        end = HALO + last_rows
        na_ref[0] = bufa[end - (CONV_A_WIDTH - 1):end, :]
        nc_ref[0] = bufc[end - (CONV_C_WIDTH - 1):end, :]
        nd_ref[0] = bufd[end - POOL_STATE:end, :]

    if tm >= HALO:
        @pl.when(t < nt - 1)
        def _():
            bufa[0:HALO, :] = bufa[tm:tm + HALO, :]
            bufc[0:HALO, :] = bufc[tm:tm + HALO, :]
            bufd[0:HALO, :] = bufd[tm:tm + HALO, :]


def _pre_call(x, lw, rope_tabs, pre_a, pre_c, pre_d, *, tm, last_rows, pos0, prompt):
    bsz, t_len, d_model = x.shape
    nt = t_len // tm
    assert nt * tm == t_len and (nt == 1 or tm >= HALO)
    rc, rsa, rsb = rope_tabs
    d_in = lw["w_in"].shape[1]

    def const(shape):
        return pl.BlockSpec(shape, lambda b, t: (0,) * len(shape))

    def per_b(shape):
        return pl.BlockSpec((1,) + shape, lambda b, t: (b,) + (0,) * len(shape))

    tok = lambda w: pl.BlockSpec((1, tm, w), lambda b, t: (b, t, 0))
    tab = pl.BlockSpec((tm, WIDTH), lambda b, t: (t, 0))
    in_specs = [tok(d_model), const((1, d_model)), const((d_model, d_in)),
                const((CONV_A_WIDTH, WIDTH)), const((CONV_C_WIDTH, WIDTH)), const((1, WIDTH)),
                const((1, WIDTH)), const((1, WIDTH)), const((WIDTH, WIDTH)), const((WIDTH, WIDTH)),
                const((1, WIDTH)), tab, tab, tab,
                per_b((HALO, WIDTH)), per_b((HALO, WIDTH)), per_b((HALO, WIDTH))]
    out_shape = [jax.ShapeDtypeStruct((bsz, t_len, 4 * WIDTH), F32),
                 jax.ShapeDtypeStruct((bsz, t_len, WIDTH), F32),
                 jax.ShapeDtypeStruct((bsz, t_len, WIDTH), F32)]
    out_specs = [tok(4 * WIDTH), tok(WIDTH), tok(WIDTH)]
    if prompt:
        assert tm == MOBA_BLOCK
        blk = lambda r, c: pl.BlockSpec((1, 1, r, c), lambda b, t: (b, t, 0, 0))
        out_shape += [jax.ShapeDtypeStruct((bsz, nt, WIDTH, tm), F32),
                      jax.ShapeDtypeStruct((bsz, nt, tm, WIDTH), BF16),
                      jax.ShapeDtypeStruct((bsz, nt, WIDTH, tm), BF16),
                      jax.ShapeDtypeStruct((bsz, nt, 1, WIDTH), F32)]
        out_specs += [blk(WIDTH, tm), blk(tm, WIDTH), blk(WIDTH, tm), blk(1, WIDTH)]
    else:
        out_shape += [jax.ShapeDtypeStruct((bsz, t_len, WIDTH), F32)]
        out_specs += [tok(WIDTH)]
    out_shape += [jax.ShapeDtypeStruct((bsz, CONV_A_WIDTH - 1, WIDTH), F32),
                  jax.ShapeDtypeStruct((bsz, CONV_C_WIDTH - 1, WIDTH), F32),
                  jax.ShapeDtypeStruct((bsz, POOL_STATE, WIDTH), F32)]
    out_specs += [per_b((CONV_A_WIDTH - 1, WIDTH)), per_b((CONV_C_WIDTH - 1, WIDTH)),
                  per_b((POOL_STATE, WIDTH))]
    body = functools.partial(_pre_body, tm=tm, last_rows=last_rows, pos0=pos0, prompt=prompt)
    return pl.pallas_call(
        body,
        grid=(bsz, nt),
        in_specs=in_specs,
        out_specs=out_specs,
        out_shape=out_shape,
        scratch_shapes=[pltpu.VMEM((HALO + tm, WIDTH), F32)] * 3
        + [pltpu.VMEM((7, HALO + tm - 8, WIDTH), F32)] + [pltpu.VMEM((HALO + tm - 8, WIDTH), F32)] * 3,
        compiler_params=pltpu.CompilerParams(dimension_semantics=("arbitrary", "arbitrary"),
                                             vmem_limit_bytes=VMEM_LIMIT),
        name="pre_prompt" if prompt else "pre_sample",
    )(x, lw["norm_g"], lw["w_in"], lw["conv_a_w"], lw["conv_c_w"], lw["conv_c_b"], lw["ln_c_g"],
      lw["ln_c_b"], lw["w_pw_c"], lw["w_pool"], lw["pool_scale"], rc, rsa, rsb, pre_a, pre_c, pre_d)


def _top_blocks_bias(gate, index, n_index, axis):
    bias = jnp.full(gate.shape, NEG_INF, F32)
    for _ in range(MOBA_TOPK):
        top = jnp.max(gate, axis=axis, keepdims=True)
        cand = jnp.logical_and(gate == top, top > NEG_INF)
        first = jnp.min(jnp.where(cand, index, n_index), axis=axis, keepdims=True)
        pick = index == first
        bias = jnp.where(pick, 0.0, bias)
        gate = jnp.where(pick, NEG_INF, gate)
    return bias


def _attn_body(qt_ref, kb_ref, vt_ref, km_ref, mixed_ref, x_ref, wout_ref, fng_ref, out_ref,
               qm_s, bias_s, acc_s, s_a, s_b, p_a, p_b, al_a, al_b, *, nblk, tq, final):
    s_bufs, p_bufs, al_bufs = (s_a, s_b), (p_a, p_b), (al_a, al_b)
    i = pl.program_id(1)
    qt = qt_ref[0, 0]
    row_head = lax.broadcasted_iota(jnp.int32, (WIDTH, tq), 0) // HEAD_DIM
    km = km_ref[0]
    lane_head = lax.broadcasted_iota(jnp.int32, (nblk, WIDTH), 1) // HEAD_DIM
    blk = lax.broadcasted_iota(jnp.int32, (nblk, tq), 0)
    for h in range(N_HEADS):
        cols = slice(h * tq, (h + 1) * tq)
        qm_s[:, cols] = jnp.where(row_head == h, qt * LOG2_E, 0.0).astype(BF16)
        gate = jnp.dot(jnp.where(lane_head == h, km, 0.0), qt, precision=lax.Precision.HIGHEST,
                       preferred_element_type=F32)
        gate = jnp.where(blk < i, gate, NEG_INF)
        bias_s[:, cols] = _top_blocks_bias(gate, blk, nblk, 0)

    def qk(block, slot):
        s_bufs[slot][...] = jnp.dot(kb_ref[0, block], qm_s[...], preferred_element_type=F32)

    def pv(block, slot, first):
        for h in range(N_HEADS):
            rows = slice(h * HEAD_DIM, (h + 1) * HEAD_DIM)
            cols = slice(h * tq, (h + 1) * tq)
            o = jnp.dot(vt_ref[0, block, rows, :], p_bufs[slot][:, cols], preferred_element_type=F32)
            acc_s[rows, :] = o if first else al_bufs[slot][:, cols] * acc_s[rows, :] + o

    qk(i, 0)
    qk(0, 1)
    key = lax.broadcasted_iota(jnp.int32, (MOBA_BLOCK, N_HEADS * tq), 0)
    qry = lax.broadcasted_iota(jnp.int32, (MOBA_BLOCK, N_HEADS * tq), 1) % tq
    s = jnp.where(key <= qry, s_bufs[0][...], NEG_INF)
    m = jnp.max(s, axis=0, keepdims=True)
    p = jnp.exp2(s - m)
    l = jnp.sum(p, axis=0, keepdims=True)
    p_bufs[0][...] = p.astype(BF16)
    pv(i, 0, True)
    p_bufs[0][...] = jnp.zeros((MOBA_BLOCK, N_HEADS * tq), BF16)
    al_bufs[0][...] = jnp.ones((1, N_HEADS * tq), F32)

    def half(n, m, l, cur, nxt):
        s = s_bufs[cur][...]
        picked = bias_s[pl.ds(n - 1, 1), :]
        m_new = jnp.maximum(m, jnp.max(s, axis=0, keepdims=True) + picked)
        alpha = jnp.exp2(m - m_new)
        p = jnp.exp2(s - (m_new - picked))
        l = alpha * l + jnp.sum(p, axis=0, keepdims=True)
        pv(jnp.maximum(n - 2, 0), nxt, False)
        qk(jnp.minimum(n, nblk - 1), nxt)
        p_bufs[cur][...] = p.astype(BF16)
        al_bufs[cur][...] = alpha
        return m_new, l

    def pair(t, carry):
        m, l = half(2 * t + 1, carry[0], carry[1], 1, 0)
        return half(2 * t + 2, m, l, 0, 1)

    m, l = lax.fori_loop(0, i // 2, pair, (m, l))
    odd = i % 2 == 1
    m, l = lax.cond(odd, lambda: half(i, m, l, 1, 0), lambda: (m, l))

    @pl.when(odd)
    def _():
        pv(i - 1, 1, False)

    @pl.when(jnp.logical_not(odd))
    def _():
        pv(jnp.maximum(i - 1, 0), 0, False)

    inv = 1.0 / l
    for h in range(N_HEADS):
        rows = slice(h * HEAD_DIM, (h + 1) * HEAD_DIM)
        acc_s[rows, :] = acc_s[rows, :] * inv[:, h * tq:(h + 1) * tq]
    y_b = acc_s[...].T
    o = _out_proj(x_ref[0], mixed_ref[0], y_b, wout_ref)
    if final:
        o = _rmsnorm(o, fng_ref[...])
    out_ref[0] = o


def _attn_call(qt, kb, vt, km, mixed, x, w_out, fng, *, final):
    bsz, nblk, _, tq = qt.shape
    d_model = x.shape[-1]
    assert tq == MOBA_BLOCK
    body = functools.partial(_attn_body, nblk=nblk, tq=tq, final=final)
    whole = lambda r, c: pl.BlockSpec((1, nblk, r, c), lambda b, i: (b, 0, 0, 0))
    return pl.pallas_call(
        body,
        grid=(bsz, nblk),
        in_specs=[pl.BlockSpec((1, 1, WIDTH, tq), lambda b, i: (b, i, 0, 0)),
                  whole(MOBA_BLOCK, WIDTH), whole(WIDTH, MOBA_BLOCK),
                  pl.BlockSpec((1, nblk, WIDTH), lambda b, i: (b, 0, 0)),
                  pl.BlockSpec((1, tq, 4 * WIDTH), lambda b, i: (b, i, 0)),
                  pl.BlockSpec((1, tq, d_model), lambda b, i: (b, i, 0)),
                  pl.BlockSpec((4 * WIDTH, d_model), lambda b, i: (0, 0)),
                  pl.BlockSpec((1, d_model), lambda b, i: (0, 0))],
        out_specs=pl.BlockSpec((1, tq, d_model), lambda b, i: (b, i, 0)),
        out_shape=jax.ShapeDtypeStruct(x.shape, F32),
        scratch_shapes=[pltpu.VMEM((WIDTH, N_HEADS * tq), BF16),
                        pltpu.VMEM((nblk, N_HEADS * tq), F32),
                        pltpu.VMEM((WIDTH, tq), F32)]
        + [pltpu.VMEM((MOBA_BLOCK, N_HEADS * tq), F32)] * 2
        + [pltpu.VMEM((MOBA_BLOCK, N_HEADS * tq), BF16)] * 2
        + [pltpu.VMEM((1, N_HEADS * tq), F32)] * 2,
        compiler_params=pltpu.CompilerParams(dimension_semantics=("arbitrary", "arbitrary"),
                                             vmem_limit_bytes=VMEM_LIMIT),
        name="attn_prompt",
    )(qt, kb, vt, km.reshape(bsz, nblk, WIDTH), mixed, x, w_out, fng)


def _dec_body(pt_ref, *refs, pps, ns, t_valid, final):
    del pt_ref
    kpages, vpages = refs[0:pps], refs[pps:2 * pps]
    (q_ref, kn_ref, vn_ref, mixed_ref, x_ref, wout_ref, fng_ref, out_ref,
     qm_s, qmt_s, mp_s, lp_s, gp_s, op_s) = refs[2 * pps:]
    step = pl.program_id(1)
    rows = N_HEADS * DEC_ROWS
    lane_head = lax.broadcasted_iota(jnp.int32, (DEC_ROWS, WIDTH), 1) // HEAD_DIM
    lane = lax.broadcasted_iota(jnp.int32, (rows, 128), 1)

    @pl.when(step == 0)
    def _():
        q8 = q_ref[0]
        for h in range(N_HEADS):
            qm_s[h * DEC_ROWS:(h + 1) * DEC_ROWS, :] = jnp.where(lane_head == h, q8, 0.0)
        qm_s[rows:128, :] = jnp.zeros((128 - rows, WIDTH), F32)
        qmt_s[...] = qm_s[...].T
        mp_s[...] = jnp.full((rows, 128), NEG_INF, F32)
        gp_s[...] = jnp.full((128, 128), NEG_INF, F32)
        lp_s[...] = jnp.zeros((rows, 128), F32)

    qmb = qm_s[0:rows, :].astype(BF16)
    contract_last = (((1,), (1,)), ((), ()))
    for c in range(pps // 2):
        kt = jnp.concatenate([kpages[2 * c][...], kpages[2 * c + 1][...]], axis=1)
        vt = jnp.concatenate([vpages[2 * c][...], vpages[2 * c + 1][...]], axis=1)
        kmean = jnp.sum(kt, axis=1, keepdims=True) * (1.0 / MOBA_BLOCK)
        gate = jnp.sum(qmt_s[...] * kmean, axis=0, keepdims=True)
        s = jnp.dot(qmb, kt.astype(BF16), preferred_element_type=F32)
        m = jnp.max(s, axis=1, keepdims=True)
        p = jnp.exp(s - m)
        l = jnp.sum(p, axis=1, keepdims=True)
        j = step * (pps // 2) + c
        mp_s[...] = jnp.where(lane == j, m, mp_s[...])
        lp_s[...] = jnp.where(lane == j, l, lp_s[...])
        gp_s[pl.ds(j, 1), :] = gate
        op_s[j] = lax.dot_general(p.astype(BF16), vt.astype(BF16), contract_last, preferred_element_type=F32)

    @pl.when(step == ns - 1)
    def _():
        nblk = ns * (pps // 2)
        blk = lax.broadcasted_iota(jnp.int32, (128, 128), 0)
        sel = (_top_blocks_bias(gp_s[...], blk, 128, 0).T == 0.0)[0:rows, :]
        zpad = jnp.zeros((128 - DEC_ROWS, WIDTH), F32)
        kn = jnp.concatenate([kn_ref[0], zpad], axis=0).astype(BF16)
        vn = jnp.concatenate([vn_ref[0], zpad], axis=0).astype(BF16)
        s = lax.dot_general(qmb, kn, contract_last, preferred_element_type=F32)
        qpos = jnp.minimum(lax.broadcasted_iota(jnp.int32, (rows, 128), 0) % DEC_ROWS, t_valid - 1)
        s = jnp.where(lane <= qpos, s, NEG_INF)
        m_own = jnp.max(s, axis=1, keepdims=True)
        p = jnp.exp(s - m_own)
        l_own = jnp.sum(p, axis=1, keepdims=True)
        o_own = jnp.dot(p.astype(BF16), vn, preferred_element_type=F32)
        mp = mp_s[...]
        m_fin = jnp.maximum(m_own, jnp.max(jnp.where(sel, mp, NEG_INF), axis=1, keepdims=True))
        w = jnp.where(sel, jnp.exp(mp - m_fin), 0.0)
        a_own = jnp.exp(m_own - m_fin)
        l_fin = a_own * l_own + jnp.sum(w * lp_s[...], axis=1, keepdims=True)
        o_fin = a_own * o_own
        for j in range(nblk):
            o_fin = o_fin + w[:, j:j + 1] * op_s[j]
        y = o_fin / l_fin
        y_b = None
        for h in range(N_HEADS):
            part = jnp.where(lane_head == h, y[h * DEC_ROWS:(h + 1) * DEC_ROWS, :], 0.0)
            y_b = part if y_b is None else y_b + part
        o = _out_proj(x_ref[0], mixed_ref[0], y_b, wout_ref)
        if final:
            o = _rmsnorm(o, fng_ref[...])
        out_ref[0] = o


def _dec_call(page_table, cache_k, cache_v, layer, q, kn, vn, mixed, x, w_out, fng, *, t_valid, final):
    bsz, n_pages = page_table.shape
    depth, n_pool, page, _, _ = cache_k.shape
    d_model = x.shape[-1]
    assert (n_pages * page) % MOBA_BLOCK == 0 and MOBA_BLOCK == 2 * page
    nblk = n_pages * page // MOBA_BLOCK
    assert MOBA_TOPK <= nblk <= 128
    pps = 8 if n_pages % 8 == 0 else 2
    ns = n_pages // pps
    ck = cache_k.transpose(0, 1, 3, 4, 2).reshape(depth, n_pool, WIDTH, page)
    cv = cache_v.transpose(0, 1, 3, 4, 2).reshape(depth, n_pool, WIDTH, page)

    def page_spec(c):
        return pl.BlockSpec((None, None, WIDTH, page), lambda b, s, pt: (layer, pt[b, s * pps + c], 0, 0))

    per_b = lambda w: pl.BlockSpec((1, DEC_ROWS, w), lambda b, s, pt: (b, 0, 0))
    rows = N_HEADS * DEC_ROWS
    body = functools.partial(_dec_body, pps=pps, ns=ns, t_valid=t_valid, final=final)
    grid_spec = pltpu.PrefetchScalarGridSpec(
        num_scalar_prefetch=1,
        grid=(bsz, ns),
        in_specs=[page_spec(c) for c in range(pps)] * 2
        + [per_b(WIDTH), per_b(WIDTH), per_b(WIDTH), per_b(4 * WIDTH), per_b(d_model),
           pl.BlockSpec((4 * WIDTH, d_model), lambda b, s, pt: (0, 0)),
           pl.BlockSpec((1, d_model), lambda b, s, pt: (0, 0))],
        out_specs=per_b(d_model),
        scratch_shapes=[pltpu.VMEM((128, WIDTH), F32), pltpu.VMEM((WIDTH, 128), F32),
                        pltpu.VMEM((rows, 128), F32), pltpu.VMEM((rows, 128), F32),
                        pltpu.VMEM((128, 128), F32), pltpu.VMEM((nblk, rows, WIDTH), F32)],
    )
    return pl.pallas_call(
        body,
        grid_spec=grid_spec,
        out_shape=jax.ShapeDtypeStruct(x.shape, F32),
        compiler_params=pltpu.CompilerParams(dimension_semantics=("arbitrary", "arbitrary"),
                                             vmem_limit_bytes=VMEM_LIMIT),
        name="attn_sample",
    )(page_table, *([ck] * pps), *([cv] * pps), q, kn, vn, mixed, x, w_out, fng)


def _rope_tables(pos0, n):
    freqs = ROPE_THETA ** (-jnp.arange(ROT_HALF, dtype=F32) * 2.0 / (2 * ROT_HALF))
    ang = (pos0 + jnp.arange(n, dtype=jnp.int32)).astype(F32)[:, None] * freqs[None, :]
    cos, sin = jnp.cos(ang), jnp.sin(ang)
    one = jnp.ones((n, HEAD_DIM - 2 * ROT_HALF), F32)
    zero8 = jnp.zeros((n, ROT_HALF), F32)
    zero_rest = jnp.zeros((n, HEAD_DIM - 2 * ROT_HALF), F32)
    rc = jnp.concatenate([cos, cos, one], axis=1)
    rsa = jnp.concatenate([-sin, zero8, zero_rest], axis=1)
    rsb = jnp.concatenate([zero8, sin, zero_rest], axis=1)
    return tuple(jnp.tile(a, (1, N_HEADS)) for a in (rc, rsa, rsb))


def _pad_state(state):
    return jnp.pad(state, ((0, 0), (HALO - state.shape[1], 0), (0, 0)))


def kernel(x_prompt, x_sample, cache_k, cache_v, page_table, state_conv_a, state_conv_c, state_pool_d,
           norm_g, w_in, conv_a_w, conv_c_w, conv_c_b, ln_c_g, ln_c_b, w_pw_c, w_pool, pool_scale, w_out,
           final_norm_g):
    depth = w_in.shape[0]
    bsz, seq, d_model = x_prompt.shape
    dec_b, dec_t, _ = x_sample.shape
    past_len = page_table.shape[1] * cache_k.shape[2]
    assert seq % MOBA_BLOCK == 0 and dec_t <= DEC_ROWS and WIDTH == N_HEADS * HEAD_DIM

    tabs_p = _rope_tables(0, seq)
    tabs_s = _rope_tables(past_len, DEC_ROWS)
    zero_state = jnp.zeros((bsz, HALO, WIDTH), F32)
    fng = final_norm_g.reshape(1, d_model)
    yp = x_prompt
    ys = jnp.pad(x_sample, ((0, 0), (0, DEC_ROWS - dec_t), (0, 0)))
    outs = [[] for _ in range(10)]
    n_groups = w_pool.shape[1]
    for l in range(depth):
        pool_bd = jax.scipy.linalg.block_diag(*[w_pool[l, g] for g in range(n_groups)])
        lw = {"norm_g": norm_g[l].reshape(1, d_model), "w_in": w_in[l].astype(BF16),
              "conv_a_w": conv_a_w[l], "conv_c_w": conv_c_w[l], "conv_c_b": conv_c_b[l].reshape(1, WIDTH),
              "ln_c_g": ln_c_g[l].reshape(1, WIDTH), "ln_c_b": ln_c_b[l].reshape(1, WIDTH),
              "w_pw_c": w_pw_c[l].astype(BF16), "w_pool": pool_bd.astype(BF16),
              "pool_scale": pool_scale[l].reshape(1, WIDTH)}
        wo = w_out[l].astype(BF16)
        final = l == depth - 1

        mixed, kp, vp, qt, kb, vt, km, ap, cp, dp = _pre_call(
            yp, lw, tabs_p, zero_state, zero_state, zero_state,
            tm=MOBA_BLOCK, last_rows=MOBA_BLOCK, pos0=0, prompt=True)
        yp = _attn_call(qt, kb, vt, km, mixed, yp, wo, fng, final=final)

        mixed_s, ks, vs, qs, as_, cs, ds = _pre_call(
            ys, lw, tabs_s, _pad_state(state_conv_a[l]), _pad_state(state_conv_c[l]),
            _pad_state(state_pool_d[l]), tm=DEC_ROWS, last_rows=dec_t, pos0=past_len, prompt=False)
        ys = _dec_call(page_table, cache_k, cache_v, l, qs, ks, vs, mixed_s, ys, wo, fng,
                       t_valid=dec_t, final=final)

        for lst, val in zip(outs, (kp.reshape(bsz, seq, N_HEADS, HEAD_DIM),
                                   vp.reshape(bsz, seq, N_HEADS, HEAD_DIM),
                                   ks[:, :dec_t].reshape(dec_b, dec_t, N_HEADS, HEAD_DIM),
                                   vs[:, :dec_t].reshape(dec_b, dec_t, N_HEADS, HEAD_DIM),
                                   ap, as_, cp, cs, dp, ds)):
            lst.append(val)
    return (yp, ys[:, :dec_t]) + tuple(jnp.stack(o) for o in outs)
```

```python
import functools

import jax
import jax.numpy as jnp
from jax import lax
from jax.experimental import pallas as pl
from jax.experimental.pallas import tpu as pltpu

F32 = jnp.float32
BF16 = jnp.bfloat16

WIDTH = 256
N_HEADS = 4
HEAD_DIM = 64
ROT_HALF = 8
ROPE_THETA = 500000.0
MOBA_BLOCK = 256
MOBA_TOPK = 3
CONV_A_WIDTH = 3
CONV_C_WIDTH = 31
POOL_WINDOWS = (2, 4, 8, 16)
POOL_STATE = 15
RMS_EPS = 1e-6
LN_EPS = 1e-5
HALO = 32
DEC_ROWS = 8
NEG_INF = float("-inf")
LOG2_E = 1.4426950408889634
VMEM_LIMIT = 56 * 1024 * 1024


def _silu(z):
    return z * jax.nn.sigmoid(z)


def _rmsnorm(x, g):
    ms = jnp.mean(x * x, axis=-1, keepdims=True)
    return x * lax.rsqrt(ms + RMS_EPS) * g


def _out_proj(x, mixed, y_b, wout_ref):
    yb = (y_b * mixed[:, WIDTH:2 * WIDTH]).astype(BF16)
    o = x + jnp.dot(mixed[:, 0:WIDTH].astype(BF16), wout_ref[0:WIDTH, :], preferred_element_type=F32)
    o = o + jnp.dot(yb, wout_ref[WIDTH:2 * WIDTH, :], preferred_element_type=F32)
    o = o + jnp.dot(mixed[:, 2 * WIDTH:4 * WIDTH].astype(BF16), wout_ref[2 * WIDTH:4 * WIDTH, :],
                    preferred_element_type=F32)
    return o


def _pre_body(x_ref, ng_ref, win_ref, caw_ref, ccw_ref, ccb_ref, lng_ref, lnb_ref, wpw_ref, wpool_ref,
              psc_ref, rc_ref, rsa_ref, rsb_ref, prea_ref, prec_ref, pred_ref, *refs,
              tm, last_rows, pos0, prompt):
    if prompt:
        (mixed_ref, k_ref, v_ref, qt_ref, kb_ref, vt_ref, km_ref,
         na_ref, nc_ref, nd_ref, bufa, bufc, bufd, shc, x2, x4, x8) = refs
    else:
        (mixed_ref, k_ref, v_ref, q_ref, na_ref, nc_ref, nd_ref, bufa, bufc, bufd, shc, x2, x4, x8) = refs
    t = pl.program_id(1)
    nt = pl.num_programs(1)
    span = tm + HALO - 8

    @pl.when(t == 0)
    def _():
        bufa[0:HALO, :] = prea_ref[0]
        bufc[0:HALO, :] = prec_ref[0]
        bufd[0:HALO, :] = pred_ref[0]

    x = x_ref[0]
    h = _rmsnorm(x, ng_ref[...]).astype(BF16)

    def proj(c0, n):
        return jnp.dot(h, win_ref[:, c0 * WIDTH:(c0 + n) * WIDTH], preferred_element_type=F32)

    pa = proj(0, 4)
    bufa[HALO:HALO + tm, :] = pa[:, 2 * WIDTH:3 * WIDTH] * pa[:, 0:WIDTH]
    conv = None
    for j in range(CONV_A_WIDTH):
        off = HALO - (CONV_A_WIDTH - 1) + j
        term = caw_ref[j:j + 1, :] * bufa[off:off + tm, :]
        conv = term if conv is None else conv + term
    mixed_ref[0, :, 0:WIDTH] = pa[:, WIDTH:2 * WIDTH] * conv * _silu(pa[:, 3 * WIDTH:4 * WIDTH])

    pb = proj(4, 4)
    rc, rsa, rsb = rc_ref[...], rsa_ref[...], rsb_ref[...]

    def rope(z):
        return z * rc + pltpu.roll(z, WIDTH - ROT_HALF, 1) * rsa + pltpu.roll(z, ROT_HALF, 1) * rsb

    qs = rope(pb[:, 0:WIDTH]) * (HEAD_DIM ** -0.5)
    kr = rope(pb[:, WIDTH:2 * WIDTH])
    v = pb[:, 2 * WIDTH:3 * WIDTH]
    k_ref[0] = kr
    v_ref[0] = v
    mixed_ref[0, :, WIDTH:2 * WIDTH] = _silu(pb[:, 3 * WIDTH:4 * WIDTH])
    if prompt:
        qt_ref[0, 0] = qs.T
        kb_ref[0, 0] = kr.astype(BF16)
        vt_ref[0, 0] = v.T.astype(BF16)
        km_ref[0, 0] = jnp.mean(kr, axis=0, keepdims=True)
    else:
        q_ref[0] = qs

    pc = proj(8, 3)
    bufc[HALO:HALO + tm, :] = pc[:, 0:WIDTH] * jax.nn.sigmoid(pc[:, WIDTH:2 * WIDTH])
    for r in range(1, 8):
        shc[r - 1] = bufc[r:r + span, :]
    acc = None
    for j in range(CONV_C_WIDTH):
        a, r = divmod(HALO - (CONV_C_WIDTH - 1) + j, 8)
        rows = bufc[8 * a:8 * a + tm, :] if r == 0 else shc[r - 1, 8 * a:8 * a + tm, :]
        term = ccw_ref[j:j + 1, :] * rows
        acc = term if acc is None else acc + term
    acc = acc + ccb_ref[...]
    mu = jnp.mean(acc, axis=-1, keepdims=True)
    cen = acc - mu
    var = jnp.mean(cen * cen, axis=-1, keepdims=True)
    ln = cen * lax.rsqrt(var + LN_EPS) * lng_ref[...] + lnb_ref[...]
    yc = jnp.dot(_silu(ln).astype(BF16), wpw_ref[...], preferred_element_type=F32)
    mixed_ref[0, :, 2 * WIDTH:3 * WIDTH] = yc * _silu(pc[:, 2 * WIDTH:3 * WIDTH])

    pd = proj(11, 2)
    dx = pd[:, 0:WIDTH]
    bufd[HALO:HALO + tm, :] = dx
    assert POOL_WINDOWS == (2, 4, 8, 16) and HALO == 32
    x2[...] = bufd[8:8 + span, :] + bufd[7:7 + span, :]
    x4[0:span - 8, :] = x2[8:span, :] + x2[6:span - 2, :]
    x8[0:span - 16, :] = x4[8:span - 8, :] + x4[4:span - 12, :]
    sums = {2: x2[24:24 + tm, :], 4: x4[16:16 + tm, :], 8: x8[8:8 + tm, :],
            16: x8[8:8 + tm, :] + x8[0:tm, :]}
    group = lax.broadcasted_iota(jnp.int32, (tm, WIDTH), 1) // (WIDTH // len(POOL_WINDOWS))
    pos = pos0 + t * tm + lax.broadcasted_iota(jnp.int32, (tm, WIDTH), 0)
    wsum = sums[POOL_WINDOWS[-1]]
    win = jnp.full((tm, WIDTH), POOL_WINDOWS[-1], jnp.int32)
    for g in range(len(POOL_WINDOWS) - 2, -1, -1):
        wsum = jnp.where(group == g, sums[POOL_WINDOWS[g]], wsum)
        win = jnp.where(group == g, POOL_WINDOWS[g], win)
    cnt = jnp.minimum(pos + 1, win).astype(F32)
    diff = wsum / cnt - dx
    yd = jnp.dot(diff.astype(BF16), wpool_ref[...], preferred_element_type=F32) * psc_ref[...]
    mixed_ref[0, :, 3 * WIDTH:4 * WIDTH] = yd * _silu(pd[:, WIDTH:2 * WIDTH])

    @pl.when(t == nt - 1)
    def _():
        end = HALO + last_rows
        na_ref[0] = bufa[end - (CONV_A_WIDTH - 1):end, :]
        nc_ref[0] = bufc[end - (CONV_C_WIDTH - 1):end, :]
        nd_ref[0] = bufd[end - POOL_STATE:end, :]

    if tm >= HALO:
        @pl.when(t < nt - 1)
        def _():
            bufa[0:HALO, :] = bufa[tm:tm + HALO, :]
            bufc[0:HALO, :] = bufc[tm:tm + HALO, :]
            bufd[0:HALO, :] = bufd[tm:tm + HALO, :]


def _pre_call(x, lw, rope_tabs, pre_a, pre_c, pre_d, *, tm, last_rows, pos0, prompt):
    bsz, t_len, d_model = x.shape
    nt = t_len // tm
    assert nt * tm == t_len and (nt == 1 or tm >= HALO)
    rc, rsa, rsb = rope_tabs
    d_in = lw["w_in"].shape[1]

    def const(shape):
        return pl.BlockSpec(shape, lambda b, t: (0,) * len(shape))

    def per_b(shape):
        return pl.BlockSpec((1,) + shape, lambda b, t: (b,) + (0,) * len(shape))

    tok = lambda w: pl.BlockSpec((1, tm, w), lambda b, t: (b, t, 0))
    tab = pl.BlockSpec((tm, WIDTH), lambda b, t: (t, 0))
    in_specs = [tok(d_model), const((1, d_model)), const((d_model, d_in)),
                const((CONV_A_WIDTH, WIDTH)), const((CONV_C_WIDTH, WIDTH)), const((1, WIDTH)),
                const((1, WIDTH)), const((1, WIDTH)), const((WIDTH, WIDTH)), const((WIDTH, WIDTH)),
                const((1, WIDTH)), tab, tab, tab,
                per_b((HALO, WIDTH)), per_b((HALO, WIDTH)), per_b((HALO, WIDTH))]
    out_shape = [jax.ShapeDtypeStruct((bsz, t_len, 4 * WIDTH), F32),
                 jax.ShapeDtypeStruct((bsz, t_len, WIDTH), F32),
                 jax.ShapeDtypeStruct((bsz, t_len, WIDTH), F32)]
    out_specs = [tok(4 * WIDTH), tok(WIDTH), tok(WIDTH)]
    if prompt:
        assert tm == MOBA_BLOCK
        blk = lambda r, c: pl.BlockSpec((1, 1, r, c), lambda b, t: (b, t, 0, 0))
        out_shape += [jax.ShapeDtypeStruct((bsz, nt, WIDTH, tm), F32),
                      jax.ShapeDtypeStruct((bsz, nt, tm, WIDTH), BF16),
                      jax.ShapeDtypeStruct((bsz, nt, WIDTH, tm), BF16),
                      jax.ShapeDtypeStruct((bsz, nt, 1, WIDTH), F32)]
        out_specs += [blk(WIDTH, tm), blk(tm, WIDTH), blk(WIDTH, tm), blk(1, WIDTH)]
    else:
        out_shape += [jax.ShapeDtypeStruct((bsz, t_len, WIDTH), F32)]
        out_specs += [tok(WIDTH)]
    out_shape += [jax.ShapeDtypeStruct((bsz, CONV_A_WIDTH - 1, WIDTH), F32),
                  jax.ShapeDtypeStruct((bsz, CONV_C_WIDTH - 1, WIDTH), F32),
                  jax.ShapeDtypeStruct((bsz, POOL_STATE, WIDTH), F32)]
    out_specs += [per_b((CONV_A_WIDTH - 1, WIDTH)), per_b((CONV_C_WIDTH - 1, WIDTH)),
                  per_b((POOL_STATE, WIDTH))]
    body = functools.partial(_pre_body, tm=tm, last_rows=last_rows, pos0=pos0, prompt=prompt)
    return pl.pallas_call(
        body,
        grid=(bsz, nt),
        in_specs=in_specs,
        out_specs=out_specs,
        out_shape=out_shape,
        scratch_shapes=[pltpu.VMEM((HALO + tm, WIDTH), F32)] * 3
        + [pltpu.VMEM((7, HALO + tm - 8, WIDTH), F32)] + [pltpu.VMEM((HALO + tm - 8, WIDTH), F32)] * 3,
        compiler_params=pltpu.CompilerParams(dimension_semantics=("arbitrary", "arbitrary"),
                                             vmem_limit_bytes=VMEM_LIMIT),
        name="pre_prompt" if prompt else "pre_sample",
    )(x, lw["norm_g"], lw["w_in"], lw["conv_a_w"], lw["conv_c_w"], lw["conv_c_b"], lw["ln_c_g"],
      lw["ln_c_b"], lw["w_pw_c"], lw["w_pool"], lw["pool_scale"], rc, rsa, rsb, pre_a, pre_c, pre_d)


def _top_blocks_bias(gate, index, n_index, axis):
    bias = jnp.full(gate.shape, NEG_INF, F32)
    for _ in range(MOBA_TOPK):
        top = jnp.max(gate, axis=axis, keepdims=True)
        cand = jnp.logical_and(gate == top, top > NEG_INF)
        first = jnp.min(jnp.where(cand, index, n_index), axis=axis, keepdims=True)
        pick = index == first
        bias = jnp.where(pick, 0.0, bias)
        gate = jnp.where(pick, NEG_INF, gate)
    return bias


def _attn_body(qt_ref, kb_ref, vt_ref, km_ref, mixed_ref, x_ref, wout_ref, fng_ref, out_ref,
               qm_s, bias_s, acc_s, s_a, s_b, p_a, p_b, al_a, al_b, *, nblk, tq, final):
    s_bufs, p_bufs, al_bufs = (s_a, s_b), (p_a, p_b), (al_a, al_b)
    w4 = N_HEADS * tq
    i = pl.program_id(1)
    qt = qt_ref[0, 0]
    row_head = lax.broadcasted_iota(jnp.int32, (WIDTH, tq), 0) // HEAD_DIM
    km = km_ref[0]
    lane_head = lax.broadcasted_iota(jnp.int32, (nblk, WIDTH), 1) // HEAD_DIM
    blk = lax.broadcasted_iota(jnp.int32, (nblk, tq), 0)
    for h in range(N_HEADS):
        cols = slice(h * tq, (h + 1) * tq)
        qm_s[:, cols] = jnp.where(row_head == h, qt * LOG2_E, 0.0).astype(BF16)
        gate = jnp.dot(jnp.where(lane_head == h, km, 0.0), qt, precision=lax.Precision.HIGHEST,
                       preferred_element_type=F32)
        gate = jnp.where(blk < i, gate, NEG_INF)
        bias_s[:, cols] = _top_blocks_bias(gate, blk, nblk, 0)

    def qk(block, slot):
        s_bufs[slot][:, 0:w4] = jnp.dot(kb_ref[0, block], qm_s[...], preferred_element_type=F32)

    def pv(block, slot, first):
        for h in range(N_HEADS):
            rows = slice(h * HEAD_DIM, (h + 1) * HEAD_DIM)
            cols = slice(h * tq, (h + 1) * tq)
            o = jnp.dot(vt_ref[0, block, rows, :], p_bufs[slot][:, cols], preferred_element_type=F32)
            acc_s[rows, :] = o if first else al_bufs[slot][:, cols] * acc_s[rows, :] + o

    qk(i, 0)
    qk(0, 1)
    key = lax.broadcasted_iota(jnp.int32, (MOBA_BLOCK, N_HEADS * tq), 0)
    qry = lax.broadcasted_iota(jnp.int32, (MOBA_BLOCK, N_HEADS * tq), 1) % tq
    s = jnp.where(key <= qry, s_bufs[0][:, 0:w4], NEG_INF)
    m = jnp.max(s, axis=0, keepdims=True)
    p = jnp.exp2(s - m)
    l = jnp.sum(p, axis=0, keepdims=True)
    p_bufs[0][:, 0:w4] =p.astype(BF16)
    pv(i, 0, True)
    p_bufs[0][:, 0:w4] =jnp.zeros((MOBA_BLOCK, N_HEADS * tq), BF16)
    al_bufs[0][...] = jnp.ones((1, N_HEADS * tq), F32)

    def half(n, m, l, cur, nxt):
        s = s_bufs[cur][:, 0:w4]
        picked = bias_s[pl.ds(n - 1, 1), :]
        m_new = jnp.maximum(m, jnp.max(s, axis=0, keepdims=True) + picked)
        alpha = jnp.exp2(m - m_new)
        p = jnp.exp2(s - (m_new - picked))
        l = alpha * l + jnp.sum(p, axis=0, keepdims=True)
        pv(jnp.maximum(n - 2, 0), nxt, False)
        qk(jnp.minimum(n, nblk - 1), nxt)
        p_bufs[cur][:, 0:w4] = p.astype(BF16)
        al_bufs[cur][...] = alpha
        return m_new, l

    def pair(t, carry):
        m, l = half(2 * t + 1, carry[0], carry[1], 1, 0)
        return half(2 * t + 2, m, l, 0, 1)

    m, l = lax.fori_loop(0, i // 2, pair, (m, l))
    odd = i % 2 == 1
    m, l = lax.cond(odd, lambda: half(i, m, l, 1, 0), lambda: (m, l))

    @pl.when(odd)
    def _():
        pv(i - 1, 1, False)

    @pl.when(jnp.logical_not(odd))
    def _():
        pv(jnp.maximum(i - 1, 0), 0, False)

    inv = 1.0 / l
    for h in range(N_HEADS):
        rows = slice(h * HEAD_DIM, (h + 1) * HEAD_DIM)
        acc_s[rows, :] = acc_s[rows, :] * inv[:, h * tq:(h + 1) * tq]
    y_b = acc_s[...].T
    o = _out_proj(x_ref[0], mixed_ref[0], y_b, wout_ref)
    if final:
        o = _rmsnorm(o, fng_ref[...])
    out_ref[0] = o


def _attn_call(qt, kb, vt, km, mixed, x, w_out, fng, *, final):
    bsz, nblk, _, tq = qt.shape
    d_model = x.shape[-1]
    assert tq == MOBA_BLOCK
    body = functools.partial(_attn_body, nblk=nblk, tq=tq, final=final)
    whole = lambda r, c: pl.BlockSpec((1, nblk, r, c), lambda b, i: (b, 0, 0, 0))
    return pl.pallas_call(
        body,
        grid=(bsz, nblk),
        in_specs=[pl.BlockSpec((1, 1, WIDTH, tq), lambda b, i: (b, i, 0, 0)),
                  whole(MOBA_BLOCK, WIDTH), whole(WIDTH, MOBA_BLOCK),
                  pl.BlockSpec((1, nblk, WIDTH), lambda b, i: (b, 0, 0)),
                  pl.BlockSpec((1, tq, 4 * WIDTH), lambda b, i: (b, i, 0)),
                  pl.BlockSpec((1, tq, d_model), lambda b, i: (b, i, 0)),
                  pl.BlockSpec((4 * WIDTH, d_model), lambda b, i: (0, 0)),
                  pl.BlockSpec((1, d_model), lambda b, i: (0, 0))],
        out_specs=pl.BlockSpec((1, tq, d_model), lambda b, i: (b, i, 0)),
        out_shape=jax.ShapeDtypeStruct(x.shape, F32),
        scratch_shapes=[pltpu.VMEM((WIDTH, N_HEADS * tq), BF16),
                        pltpu.VMEM((nblk, N_HEADS * tq), F32),
                        pltpu.VMEM((WIDTH, tq), F32)]
        + [pltpu.VMEM((MOBA_BLOCK, N_HEADS * tq + 128), F32)] * 2
        + [pltpu.VMEM((MOBA_BLOCK, N_HEADS * tq + 128), BF16)] * 2
        + [pltpu.VMEM((1, N_HEADS * tq), F32)] * 2,
        compiler_params=pltpu.CompilerParams(dimension_semantics=("arbitrary", "arbitrary"),
                                             vmem_limit_bytes=VMEM_LIMIT),
        name="attn_prompt",
    )(qt, kb, vt, km.reshape(bsz, nblk, WIDTH), mixed, x, w_out, fng)


def _dec_body(pt_ref, *refs, pps, ns, t_valid, final):
    del pt_ref
    kpages, vpages = refs[0:pps], refs[pps:2 * pps]
    (q_ref, kn_ref, vn_ref, mixed_ref, x_ref, wout_ref, fng_ref, out_ref,
     qm_s, qmt_s, mp_s, lp_s, gp_s, op_s) = refs[2 * pps:]
    step = pl.program_id(1)
    rows = N_HEADS * DEC_ROWS
    lane_head = lax.broadcasted_iota(jnp.int32, (DEC_ROWS, WIDTH), 1) // HEAD_DIM
    lane = lax.broadcasted_iota(jnp.int32, (rows, 128), 1)

    @pl.when(step == 0)
    def _():
        q8 = q_ref[0]
        for h in range(N_HEADS):
            qm_s[h * DEC_ROWS:(h + 1) * DEC_ROWS, :] = jnp.where(lane_head == h, q8, 0.0)
        qm_s[rows:128, :] = jnp.zeros((128 - rows, WIDTH), F32)
        qmt_s[...] = qm_s[...].T
        mp_s[...] = jnp.full((rows, 128), NEG_INF, F32)
        gp_s[...] = jnp.full((128, 128), NEG_INF, F32)
        lp_s[...] = jnp.zeros((rows, 128), F32)

    qmb = qm_s[0:rows, :].astype(BF16)
    contract_last = (((1,), (1,)), ((), ()))
    for c in range(pps // 2):
        kt = jnp.concatenate([kpages[2 * c][...], kpages[2 * c + 1][...]], axis=1)
        vt = jnp.concatenate([vpages[2 * c][...], vpages[2 * c + 1][...]], axis=1)
        kmean = jnp.sum(kt, axis=1, keepdims=True) * (1.0 / MOBA_BLOCK)
        gate = jnp.sum(qmt_s[...] * kmean, axis=0, keepdims=True)
        s = jnp.dot(qmb, kt.astype(BF16), preferred_element_type=F32)
        m = jnp.max(s, axis=1, keepdims=True)
        p = jnp.exp(s - m)
        l = jnp.sum(p, axis=1, keepdims=True)
        j = step * (pps // 2) + c
        mp_s[...] = jnp.where(lane == j, m, mp_s[...])
        lp_s[...] = jnp.where(lane == j, l, lp_s[...])
        gp_s[pl.ds(j, 1), :] = gate
        op_s[j] = lax.dot_general(p.astype(BF16), vt.astype(BF16), contract_last, preferred_element_type=F32)

    @pl.when(step == ns - 1)
    def _():
        nblk = ns * (pps // 2)
        blk = lax.broadcasted_iota(jnp.int32, (128, 128), 0)
        sel = (_top_blocks_bias(gp_s[...], blk, 128, 0).T == 0.0)[0:rows, :]
        zpad = jnp.zeros((128 - DEC_ROWS, WIDTH), F32)
        kn = jnp.concatenate([kn_ref[0], zpad], axis=0).astype(BF16)
        vn = jnp.concatenate([vn_ref[0], zpad], axis=0).astype(BF16)
        s = lax.dot_general(qmb, kn, contract_last, preferred_element_type=F32)
        qpos = jnp.minimum(lax.broadcasted_iota(jnp.int32, (rows, 128), 0) % DEC_ROWS, t_valid - 1)
        s = jnp.where(lane <= qpos, s, NEG_INF)
        m_own = jnp.max(s, axis=1, keepdims=True)
        p = jnp.exp(s - m_own)
        l_own = jnp.sum(p, axis=1, keepdims=True)
        o_own = jnp.dot(p.astype(BF16), vn, preferred_element_type=F32)
        mp = mp_s[...]
        m_fin = jnp.maximum(m_own, jnp.max(jnp.where(sel, mp, NEG_INF), axis=1, keepdims=True))
        w = jnp.where(sel, jnp.exp(mp - m_fin), 0.0)
        a_own = jnp.exp(m_own - m_fin)
        l_fin = a_own * l_own + jnp.sum(w * lp_s[...], axis=1, keepdims=True)
        o_fin = a_own * o_own
        for j in range(nblk):
            o_fin = o_fin + w[:, j:j + 1] * op_s[j]
        y = o_fin / l_fin
        y_b = None
        for h in range(N_HEADS):
            part = jnp.where(lane_head == h, y[h * DEC_ROWS:(h + 1) * DEC_ROWS, :], 0.0)
            y_b = part if y_b is None else y_b + part
        o = _out_proj(x_ref[0], mixed_ref[0], y_b, wout_ref)
        if final:
            o = _rmsnorm(o, fng_ref[...])
        out_ref[0] = o


def _dec_call(page_table, cache_k, cache_v, layer, q, kn, vn, mixed, x, w_out, fng, *, t_valid, final):
    bsz, n_pages = page_table.shape
    depth, n_pool, page, _, _ = cache_k.shape
    d_model = x.shape[-1]
    assert (n_pages * page) % MOBA_BLOCK == 0 and MOBA_BLOCK == 2 * page
    nblk = n_pages * page // MOBA_BLOCK
    assert MOBA_TOPK <= nblk <= 128
    pps = next(c for c in (16, 8, 4, 2) if n_pages % c == 0)
    ns = n_pages // pps
    ck = cache_k.transpose(0, 1, 3, 4, 2).reshape(depth, n_pool, WIDTH, page)
    cv = cache_v.transpose(0, 1, 3, 4, 2).reshape(depth, n_pool, WIDTH, page)

    def page_spec(c):
        return pl.BlockSpec((None, None, WIDTH, page), lambda b, s, pt: (layer, pt[b, s * pps + c], 0, 0))

    per_b = lambda w: pl.BlockSpec((1, DEC_ROWS, w), lambda b, s, pt: (b, 0, 0))
    rows = N_HEADS * DEC_ROWS
    body = functools.partial(_dec_body, pps=pps, ns=ns, t_valid=t_valid, final=final)
    grid_spec = pltpu.PrefetchScalarGridSpec(
        num_scalar_prefetch=1,
        grid=(bsz, ns),
        in_specs=[page_spec(c) for c in range(pps)] * 2
        + [per_b(WIDTH), per_b(WIDTH), per_b(WIDTH), per_b(4 * WIDTH), per_b(d_model),
           pl.BlockSpec((4 * WIDTH, d_model), lambda b, s, pt: (0, 0)),
           pl.BlockSpec((1, d_model), lambda b, s, pt: (0, 0))],
        out_specs=per_b(d_model),
        scratch_shapes=[pltpu.VMEM((128, WIDTH), F32), pltpu.VMEM((WIDTH, 128), F32),
                        pltpu.VMEM((rows, 128), F32), pltpu.VMEM((rows, 128), F32),
                        pltpu.VMEM((128, 128), F32), pltpu.VMEM((nblk, rows, WIDTH), F32)],
    )
    return pl.pallas_call(
        body,
        grid_spec=grid_spec,
        out_shape=jax.ShapeDtypeStruct(x.shape, F32),
        compiler_params=pltpu.CompilerParams(dimension_semantics=("arbitrary", "arbitrary"),
                                             vmem_limit_bytes=VMEM_LIMIT),
        name="attn_sample",
    )(page_table, *([ck] * pps), *([cv] * pps), q, kn, vn, mixed, x, w_out, fng)


def _rope_tables(pos0, n):
    freqs = ROPE_THETA ** (-jnp.arange(ROT_HALF, dtype=F32) * 2.0 / (2 * ROT_HALF))
    ang = (pos0 + jnp.arange(n, dtype=jnp.int32)).astype(F32)[:, None] * freqs[None, :]
    cos, sin = jnp.cos(ang), jnp.sin(ang)
    one = jnp.ones((n, HEAD_DIM - 2 * ROT_HALF), F32)
    zero8 = jnp.zeros((n, ROT_HALF), F32)
    zero_rest = jnp.zeros((n, HEAD_DIM - 2 * ROT_HALF), F32)
    rc = jnp.concatenate([cos, cos, one], axis=1)
    rsa = jnp.concatenate([-sin, zero8, zero_rest], axis=1)
    rsb = jnp.concatenate([zero8, sin, zero_rest], axis=1)
    return tuple(jnp.tile(a, (1, N_HEADS)) for a in (rc, rsa, rsb))


def _pad_state(state):
    return jnp.pad(state, ((0, 0), (HALO - state.shape[1], 0), (0, 0)))


def kernel(x_prompt, x_sample, cache_k, cache_v, page_table, state_conv_a, state_conv_c, state_pool_d,
           norm_g, w_in, conv_a_w, conv_c_w, conv_c_b, ln_c_g, ln_c_b, w_pw_c, w_pool, pool_scale, w_out,
           final_norm_g):
    depth = w_in.shape[0]
    bsz, seq, d_model = x_prompt.shape
    dec_b, dec_t, _ = x_sample.shape
    past_len = page_table.shape[1] * cache_k.shape[2]
    assert seq % MOBA_BLOCK == 0 and dec_t <= DEC_ROWS and WIDTH == N_HEADS * HEAD_DIM

    tabs_p = _rope_tables(0, seq)
    tabs_s = _rope_tables(past_len, DEC_ROWS)
    zero_state = jnp.zeros((bsz, HALO, WIDTH), F32)
    fng = final_norm_g.reshape(1, d_model)
    yp = x_prompt
    ys = jnp.pad(x_sample, ((0, 0), (0, DEC_ROWS - dec_t), (0, 0)))
    outs = [[] for _ in range(10)]
    n_groups = w_pool.shape[1]
    for l in range(depth):
        pool_bd = jax.scipy.linalg.block_diag(*[w_pool[l, g] for g in range(n_groups)])
        lw = {"norm_g": norm_g[l].reshape(1, d_model), "w_in": w_in[l].astype(BF16),
              "conv_a_w": conv_a_w[l], "conv_c_w": conv_c_w[l], "conv_c_b": conv_c_b[l].reshape(1, WIDTH),
              "ln_c_g": ln_c_g[l].reshape(1, WIDTH), "ln_c_b": ln_c_b[l].reshape(1, WIDTH),
              "w_pw_c": w_pw_c[l].astype(BF16), "w_pool": pool_bd.astype(BF16),
              "pool_scale": pool_scale[l].reshape(1, WIDTH)}
        wo = w_out[l].astype(BF16)
        final = l == depth - 1

        mixed, kp, vp, qt, kb, vt, km, ap, cp, dp = _pre_call(
            yp, lw, tabs_p, zero_state, zero_state, zero_state,
            tm=MOBA_BLOCK, last_rows=MOBA_BLOCK, pos0=0, prompt=True)
        yp = _attn_call(qt, kb, vt, km, mixed, yp, wo, fng, final=final)

        mixed_s, ks, vs, qs, as_, cs, ds = _pre_call(
            ys, lw, tabs_s, _pad_state(state_conv_a[l]), _pad_state(state_conv_c[l]),
            _pad_state(state_pool_d[l]), tm=DEC_ROWS, last_rows=dec_t, pos0=past_len, prompt=False)
        ys = _dec_call(page_table, cache_k, cache_v, l, qs, ks, vs, mixed_s, ys, wo, fng,
                       t_valid=dec_t, final=final)

        for lst, val in zip(outs, (kp.reshape(bsz, seq, N_HEADS, HEAD_DIM),
                                   vp.reshape(bsz, seq, N_HEADS, HEAD_DIM),
                                   ks[:, :dec_t].reshape(dec_b, dec_t, N_HEADS, HEAD_DIM),
                                   vs[:, :dec_t].reshape(dec_b, dec_t, N_HEADS, HEAD_DIM),
                                   ap, as_, cp, cs, dp, ds)):
            lst.append(val)
    return (yp, ys[:, :dec_t]) + tuple(jnp.stack(o) for o in outs)
```

```python
import functools

import jax
import jax.numpy as jnp
from jax import lax
from jax.experimental import pallas as pl
from jax.experimental.pallas import tpu as pltpu

F32 = jnp.float32
BF16 = jnp.bfloat16

WIDTH = 256
N_HEADS = 4
HEAD_DIM = 64
ROT_HALF = 8
ROPE_THETA = 500000.0
MOBA_BLOCK = 256
MOBA_TOPK = 3
CONV_A_WIDTH = 3
CONV_C_WIDTH = 31
POOL_WINDOWS = (2, 4, 8, 16)
POOL_STATE = 15
RMS_EPS = 1e-6
LN_EPS = 1e-5
HALO = 32
DEC_ROWS = 8
ATTN_GROUP = 4
NEG_INF = float("-inf")
LOG2_E = 1.4426950408889634
VMEM_LIMIT = 56 * 1024 * 1024


def _silu(z):
    return z * jax.nn.sigmoid(z)


def _rmsnorm(x, g):
    ms = jnp.mean(x * x, axis=-1, keepdims=True)
    return x * lax.rsqrt(ms + RMS_EPS) * g


def _out_proj(x, mixed, y_b, wout_ref):
    yb = (y_b * mixed[:, WIDTH:2 * WIDTH]).astype(BF16)
    o = x + jnp.dot(mixed[:, 0:WIDTH].astype(BF16), wout_ref[0:WIDTH, :], preferred_element_type=F32)
    o = o + jnp.dot(yb, wout_ref[WIDTH:2 * WIDTH, :], preferred_element_type=F32)
    o = o + jnp.dot(mixed[:, 2 * WIDTH:4 * WIDTH].astype(BF16), wout_ref[2 * WIDTH:4 * WIDTH, :],
                    preferred_element_type=F32)
    return o


def _pre_body(x_ref, ng_ref, win_ref, caw_ref, ccw_ref, ccb_ref, lng_ref, lnb_ref, wpw_ref, wpool_ref,
              psc_ref, rc_ref, rsa_ref, rsb_ref, prea_ref, prec_ref, pred_ref, *refs,
              tm, last_rows, pos0, prompt):
    if prompt:
        (mixed_ref, k_ref, v_ref, qt_ref, kb_ref, vt_ref, km_ref,
         na_ref, nc_ref, nd_ref, bufa, bufc, bufd, shc, x2, x4, x8) = refs
    else:
        (mixed_ref, k_ref, v_ref, q_ref, na_ref, nc_ref, nd_ref, bufa, bufc, bufd, shc, x2, x4, x8) = refs
    t = pl.program_id(1)
    nt = pl.num_programs(1)
    span = tm + HALO - 8

    @pl.when(t == 0)
    def _():
        bufa[0:HALO, :] = prea_ref[0]
        bufc[0:HALO, :] = prec_ref[0]
        bufd[0:HALO, :] = pred_ref[0]

    x = x_ref[0]
    h = _rmsnorm(x, ng_ref[...]).astype(BF16)

    def proj(c0, n):
        return jnp.dot(h, win_ref[:, c0 * WIDTH:(c0 + n) * WIDTH], preferred_element_type=F32)

    pa = proj(0, 4)
    bufa[HALO:HALO + tm, :] = pa[:, 2 * WIDTH:3 * WIDTH] * pa[:, 0:WIDTH]
    conv = None
    for j in range(CONV_A_WIDTH):
        off = HALO - (CONV_A_WIDTH - 1) + j
        term = caw_ref[j:j + 1, :] * bufa[off:off + tm, :]
        conv = term if conv is None else conv + term
    mixed_ref[0, :, 0:WIDTH] = pa[:, WIDTH:2 * WIDTH] * conv * _silu(pa[:, 3 * WIDTH:4 * WIDTH])

    pb = proj(4, 4)
    rc, rsa, rsb = rc_ref[...], rsa_ref[...], rsb_ref[...]

    def rope(z):
        return z * rc + pltpu.roll(z, WIDTH - ROT_HALF, 1) * rsa + pltpu.roll(z, ROT_HALF, 1) * rsb

    qs = rope(pb[:, 0:WIDTH]) * (HEAD_DIM ** -0.5)
    kr = rope(pb[:, WIDTH:2 * WIDTH])
    v = pb[:, 2 * WIDTH:3 * WIDTH]
    k_ref[0] = kr
    v_ref[0] = v
    mixed_ref[0, :, WIDTH:2 * WIDTH] = _silu(pb[:, 3 * WIDTH:4 * WIDTH])
    if prompt:
        qt_ref[0, 0] = qs.T
        kb_ref[0, 0] = kr.astype(BF16)
        vt_ref[0, 0] = v.T.astype(BF16)
        km_ref[0, 0] = jnp.mean(kr, axis=0, keepdims=True)
    else:
        q_ref[0] = qs

    pc = proj(8, 3)
    bufc[HALO:HALO + tm, :] = pc[:, 0:WIDTH] * jax.nn.sigmoid(pc[:, WIDTH:2 * WIDTH])
    for r in range(1, 8):
        shc[r - 1] = bufc[r:r + span, :]
    acc = None
    for j in range(CONV_C_WIDTH):
        a, r = divmod(HALO - (CONV_C_WIDTH - 1) + j, 8)
        rows = bufc[8 * a:8 * a + tm, :] if r == 0 else shc[r - 1, 8 * a:8 * a + tm, :]
        term = ccw_ref[j:j + 1, :] * rows
        acc = term if acc is None else acc + term
    acc = acc + ccb_ref[...]
    mu = jnp.mean(acc, axis=-1, keepdims=True)
    cen = acc - mu
    var = jnp.mean(cen * cen, axis=-1, keepdims=True)
    ln = cen * lax.rsqrt(var + LN_EPS) * lng_ref[...] + lnb_ref[...]
    yc = jnp.dot(_silu(ln).astype(BF16), wpw_ref[...], preferred_element_type=F32)
    mixed_ref[0, :, 2 * WIDTH:3 * WIDTH] = yc * _silu(pc[:, 2 * WIDTH:3 * WIDTH])

    pd = proj(11, 2)
    dx = pd[:, 0:WIDTH]
    bufd[HALO:HALO + tm, :] = dx
    assert POOL_WINDOWS == (2, 4, 8, 16) and HALO == 32
    x2[...] = bufd[8:8 + span, :] + bufd[7:7 + span, :]
    x4[0:span - 8, :] = x2[8:span, :] + x2[6:span - 2, :]
    x8[0:span - 16, :] = x4[8:span - 8, :] + x4[4:span - 12, :]
    sums = {2: x2[24:24 + tm, :], 4: x4[16:16 + tm, :], 8: x8[8:8 + tm, :],
            16: x8[8:8 + tm, :] + x8[0:tm, :]}
    group = lax.broadcasted_iota(jnp.int32, (tm, WIDTH), 1) // (WIDTH // len(POOL_WINDOWS))
    pos = pos0 + t * tm + lax.broadcasted_iota(jnp.int32, (tm, WIDTH), 0)
    wsum = sums[POOL_WINDOWS[-1]]
    win = jnp.full((tm, WIDTH), POOL_WINDOWS[-1], jnp.int32)
    for g in range(len(POOL_WINDOWS) - 2, -1, -1):
        wsum = jnp.where(group == g, sums[POOL_WINDOWS[g]], wsum)
        win = jnp.where(group == g, POOL_WINDOWS[g], win)
    cnt = jnp.minimum(pos + 1, win).astype(F32)
    diff = wsum / cnt - dx
    yd = jnp.dot(diff.astype(BF16), wpool_ref[...], preferred_element_type=F32) * psc_ref[...]
    mixed_ref[0, :, 3 * WIDTH:4 * WIDTH] = yd * _silu(pd[:, WIDTH:2 * WIDTH])

    @pl.when(t == nt - 1)
    def _():
        end = HALO + last_rows
        na_ref[0] = bufa[end - (CONV_A_WIDTH - 1):end, :]
        nc_ref[0] = bufc[end - (CONV_C_WIDTH - 1):end, :]
        nd_ref[0] = bufd[end - POOL_STATE:end, :]

    if tm >= HALO:
        @pl.when(t < nt - 1)
        def _():
            bufa[0:HALO, :] = bufa[tm:tm + HALO, :]
            bufc[0:HALO, :] = bufc[tm:tm + HALO, :]
            bufd[0:HALO, :] = bufd[tm:tm + HALO, :]


def _pre_call(x, lw, rope_tabs, pre_a, pre_c, pre_d, *, tm, last_rows, pos0, prompt):
    bsz, t_len, d_model = x.shape
    nt = t_len // tm
    assert nt * tm == t_len and (nt == 1 or tm >= HALO)
    rc, rsa, rsb = rope_tabs
    d_in = lw["w_in"].shape[1]

    def const(shape):
        return pl.BlockSpec(shape, lambda b, t: (0,) * len(shape))

    def per_b(shape):
        return pl.BlockSpec((1,) + shape, lambda b, t: (b,) + (0,) * len(shape))

    tok = lambda w: pl.BlockSpec((1, tm, w), lambda b, t: (b, t, 0))
    tab = pl.BlockSpec((tm, WIDTH), lambda b, t: (t, 0))
    in_specs = [tok(d_model), const((1, d_model)), const((d_model, d_in)),
                const((CONV_A_WIDTH, WIDTH)), const((CONV_C_WIDTH, WIDTH)), const((1, WIDTH)),
                const((1, WIDTH)), const((1, WIDTH)), const((WIDTH, WIDTH)), const((WIDTH, WIDTH)),
                const((1, WIDTH)), tab, tab, tab,
                per_b((HALO, WIDTH)), per_b((HALO, WIDTH)), per_b((HALO, WIDTH))]
    out_shape = [jax.ShapeDtypeStruct((bsz, t_len, 4 * WIDTH), F32),
                 jax.ShapeDtypeStruct((bsz, t_len, WIDTH), F32),
                 jax.ShapeDtypeStruct((bsz, t_len, WIDTH), F32)]
    out_specs = [tok(4 * WIDTH), tok(WIDTH), tok(WIDTH)]
    if prompt:
        assert tm == MOBA_BLOCK
        blk = lambda r, c: pl.BlockSpec((1, 1, r, c), lambda b, t: (b, t, 0, 0))
        out_shape += [jax.ShapeDtypeStruct((bsz, nt, WIDTH, tm), F32),
                      jax.ShapeDtypeStruct((bsz, nt, tm, WIDTH), BF16),
                      jax.ShapeDtypeStruct((bsz, nt, WIDTH, tm), BF16),
                      jax.ShapeDtypeStruct((bsz, nt, 1, WIDTH), F32)]
        out_specs += [blk(WIDTH, tm), blk(tm, WIDTH), blk(WIDTH, tm), blk(1, WIDTH)]
    else:
        out_shape += [jax.ShapeDtypeStruct((bsz, t_len, WIDTH), F32)]
        out_specs += [tok(WIDTH)]
    out_shape += [jax.ShapeDtypeStruct((bsz, CONV_A_WIDTH - 1, WIDTH), F32),
                  jax.ShapeDtypeStruct((bsz, CONV_C_WIDTH - 1, WIDTH), F32),
                  jax.ShapeDtypeStruct((bsz, POOL_STATE, WIDTH), F32)]
    out_specs += [per_b((CONV_A_WIDTH - 1, WIDTH)), per_b((CONV_C_WIDTH - 1, WIDTH)),
                  per_b((POOL_STATE, WIDTH))]
    body = functools.partial(_pre_body, tm=tm, last_rows=last_rows, pos0=pos0, prompt=prompt)
    return pl.pallas_call(
        body,
        grid=(bsz, nt),
        in_specs=in_specs,
        out_specs=out_specs,
        out_shape=out_shape,
        scratch_shapes=[pltpu.VMEM((HALO + tm, WIDTH), F32)] * 3
        + [pltpu.VMEM((7, HALO + tm - 8, WIDTH), F32)] + [pltpu.VMEM((HALO + tm - 8, WIDTH), F32)] * 3,
        compiler_params=pltpu.CompilerParams(dimension_semantics=("arbitrary", "arbitrary"),
                                             vmem_limit_bytes=VMEM_LIMIT),
        name="pre_prompt" if prompt else "pre_sample",
    )(x, lw["norm_g"], lw["w_in"], lw["conv_a_w"], lw["conv_c_w"], lw["conv_c_b"], lw["ln_c_g"],
      lw["ln_c_b"], lw["w_pw_c"], lw["w_pool"], lw["pool_scale"], rc, rsa, rsb, pre_a, pre_c, pre_d)


def _top_blocks_bias(gate, index, n_index, axis):
    bias = jnp.full(gate.shape, NEG_INF, F32)
    for _ in range(MOBA_TOPK):
        top = jnp.max(gate, axis=axis, keepdims=True)
        cand = jnp.logical_and(gate == top, top > NEG_INF)
        first = jnp.min(jnp.where(cand, index, n_index), axis=axis, keepdims=True)
        pick = index == first
        bias = jnp.where(pick, 0.0, bias)
        gate = jnp.where(pick, NEG_INF, gate)
    return bias


def _attn_body(qt_ref, kb_ref, vt_ref, km_ref, mixed_ref, x_ref, wout_ref, fng_ref, out_ref,
               qm_s, bias_s, acc_s, s_a, s_b, p_a, p_b, al_a, al_b, *, nblk, tq, grp, final):
    s_bufs, p_bufs, al_bufs = (s_a, s_b), (p_a, p_b), (al_a, al_b)
    w4 = N_HEADS * tq
    i = pl.program_id(1)
    qt = qt_ref[0, 0]
    row_head = lax.broadcasted_iota(jnp.int32, (WIDTH, tq), 0) // HEAD_DIM
    km = km_ref[0]
    lane_head = lax.broadcasted_iota(jnp.int32, (nblk, WIDTH), 1) // HEAD_DIM
    blk = lax.broadcasted_iota(jnp.int32, (nblk, tq), 0)
    for h in range(N_HEADS):
        cols = slice(h * tq, (h + 1) * tq)
        qm_s[:, cols] = jnp.where(row_head == h, qt * LOG2_E, 0.0).astype(BF16)
        gate = jnp.dot(jnp.where(lane_head == h, km, 0.0), qt, precision=lax.Precision.HIGHEST,
                       preferred_element_type=F32)
        gate = jnp.where(blk < i, gate, NEG_INF)
        bias_s[:, cols] = _top_blocks_bias(gate, blk, nblk, 0)

    n_groups = nblk // grp
    sub = [slice(c * MOBA_BLOCK, (c + 1) * MOBA_BLOCK) for c in range(grp)]

    def qk(group, slot):
        kb = kb_ref[0, pl.ds(group * grp, grp)].reshape(grp * MOBA_BLOCK, WIDTH)
        s_bufs[slot][:, 0:w4] = jnp.dot(kb, qm_s[...], preferred_element_type=F32)

    def pv(group, slot):
        for h in range(N_HEADS):
            rows = slice(h * HEAD_DIM, (h + 1) * HEAD_DIM)
            cols = slice(h * tq, (h + 1) * tq)
            vt = jnp.concatenate([vt_ref[0, group * grp + c, rows, :] for c in range(grp)], axis=1)
            o = jnp.dot(vt, p_bufs[slot][:, cols], preferred_element_type=F32)
            acc_s[rows, :] = al_bufs[slot][:, cols] * acc_s[rows, :] + o

    qk(0, 1)
    key = lax.broadcasted_iota(jnp.int32, (MOBA_BLOCK, w4), 0)
    qry = lax.broadcasted_iota(jnp.int32, (MOBA_BLOCK, w4), 1) % tq
    s = jnp.dot(kb_ref[0, i], qm_s[...], preferred_element_type=F32)
    s = jnp.where(key <= qry, s, NEG_INF)
    m = jnp.max(s, axis=0, keepdims=True)
    p = jnp.exp2(s - m)
    l = jnp.sum(p, axis=0, keepdims=True)
    pb = p.astype(BF16)
    for h in range(N_HEADS):
        rows = slice(h * HEAD_DIM, (h + 1) * HEAD_DIM)
        acc_s[rows, :] = jnp.dot(vt_ref[0, i, rows, :], pb[:, h * tq:(h + 1) * tq], preferred_element_type=F32)
    p_bufs[0][:, 0:w4] = jnp.zeros((grp * MOBA_BLOCK, w4), BF16)
    al_bufs[0][...] = jnp.ones((1, w4), F32)

    def half(n, m, l, cur, nxt):
        picked = [bias_s[pl.ds((n - 1) * grp + c, 1), :] for c in range(grp)]
        m_new = m
        for c in range(grp):
            m_new = jnp.maximum(m_new, jnp.max(s_bufs[cur][sub[c], 0:w4], axis=0, keepdims=True) + picked[c])
        alpha = jnp.exp2(m - m_new)
        l = alpha * l
        for c in range(grp):
            p = jnp.exp2(s_bufs[cur][sub[c], 0:w4] - (m_new - picked[c]))
            l = l + jnp.sum(p, axis=0, keepdims=True)
            p_bufs[cur][sub[c], 0:w4] = p.astype(BF16)
        al_bufs[cur][...] = alpha
        pv(jnp.maximum(n - 2, 0), nxt)
        qk(jnp.minimum(n, n_groups - 1), nxt)
        return m_new, l

    def visit(n, carry):
        return lax.cond(n % 2 == 1, lambda: half(n, carry[0], carry[1], 1, 0),
                        lambda: half(n, carry[0], carry[1], 0, 1))

    n_visits = (i + grp - 1) // grp
    m, l = lax.fori_loop(1, n_visits + 1, visit, (m, l))
    odd = n_visits % 2 == 1

    @pl.when(odd)
    def _():
        pv(n_visits - 1, 1)

    @pl.when(jnp.logical_not(odd))
    def _():
        pv(jnp.maximum(n_visits - 1, 0), 0)

    inv = 1.0 / l
    for h in range(N_HEADS):
        rows = slice(h * HEAD_DIM, (h + 1) * HEAD_DIM)
        acc_s[rows, :] = acc_s[rows, :] * inv[:, h * tq:(h + 1) * tq]
    y_b = acc_s[...].T
    o = _out_proj(x_ref[0], mixed_ref[0], y_b, wout_ref)
    if final:
        o = _rmsnorm(o, fng_ref[...])
    out_ref[0] = o


def _attn_call(qt, kb, vt, km, mixed, x, w_out, fng, *, final):
    bsz, nblk, _, tq = qt.shape
    d_model = x.shape[-1]
    assert tq == MOBA_BLOCK
    grp = next(g for g in (ATTN_GROUP, 2, 1) if nblk % g == 0)
    body = functools.partial(_attn_body, nblk=nblk, tq=tq, grp=grp, final=final)
    whole = lambda r, c: pl.BlockSpec((1, nblk, r, c), lambda b, i: (b, 0, 0, 0))
    return pl.pallas_call(
        body,
        grid=(bsz, nblk),
        in_specs=[pl.BlockSpec((1, 1, WIDTH, tq), lambda b, i: (b, i, 0, 0)),
                  whole(MOBA_BLOCK, WIDTH), whole(WIDTH, MOBA_BLOCK),
                  pl.BlockSpec((1, nblk, WIDTH), lambda b, i: (b, 0, 0)),
                  pl.BlockSpec((1, tq, 4 * WIDTH), lambda b, i: (b, i, 0)),
                  pl.BlockSpec((1, tq, d_model), lambda b, i: (b, i, 0)),
                  pl.BlockSpec((4 * WIDTH, d_model), lambda b, i: (0, 0)),
                  pl.BlockSpec((1, d_model), lambda b, i: (0, 0))],
        out_specs=pl.BlockSpec((1, tq, d_model), lambda b, i: (b, i, 0)),
        out_shape=jax.ShapeDtypeStruct(x.shape, F32),
        scratch_shapes=[pltpu.VMEM((WIDTH, N_HEADS * tq), BF16),
                        pltpu.VMEM((nblk, N_HEADS * tq), F32),
                        pltpu.VMEM((WIDTH, tq), F32)]
        + [pltpu.VMEM((grp * MOBA_BLOCK, N_HEADS * tq + 128), F32)] * 2
        + [pltpu.VMEM((grp * MOBA_BLOCK, N_HEADS * tq + 128), BF16)] * 2
        + [pltpu.VMEM((1, N_HEADS * tq), F32)] * 2,
        compiler_params=pltpu.CompilerParams(dimension_semantics=("arbitrary", "arbitrary"),
                                             vmem_limit_bytes=VMEM_LIMIT),
        name="attn_prompt",
    )(qt, kb, vt, km.reshape(bsz, nblk, WIDTH), mixed, x, w_out, fng)


def _dec_body(pt_ref, *refs, pps, ns, t_valid):
    del pt_ref
    kpages, vpages = refs[0:pps], refs[pps:2 * pps]
    q_ref, kn_ref, vn_ref, yb_ref, qm_s, mp_s, lp_s, gp_s, op_s, pbd_s = refs[2 * pps:]
    step = pl.program_id(1)
    rows = N_HEADS * DEC_ROWS
    nb = pps // 2
    lane_head = lax.broadcasted_iota(jnp.int32, (DEC_ROWS, WIDTH), 1) // HEAD_DIM
    lane = lax.broadcasted_iota(jnp.int32, (rows, 128), 1)

    @pl.when(jnp.logical_and(pl.program_id(0) == 0, step == 0))
    def _():
        pbd_s[...] = jnp.zeros(pbd_s.shape, BF16)

    @pl.when(step == 0)
    def _():
        q8 = q_ref[0]
        for h in range(N_HEADS):
            qm_s[h * DEC_ROWS:(h + 1) * DEC_ROWS, :] = jnp.where(lane_head == h, q8, 0.0)
        mp_s[...] = jnp.full((rows, 128), NEG_INF, F32)
        gp_s[...] = jnp.full((rows, 128), NEG_INF, F32)
        lp_s[...] = jnp.zeros((rows, 128), F32)

    qmb = qm_s[...].astype(BF16)
    contract_last = (((1,), (1,)), ((), ()))
    kt = jnp.concatenate([kpages[c][...] for c in range(pps)], axis=1).astype(BF16)
    vt = jnp.concatenate([vpages[c][...] for c in range(pps)], axis=1).astype(BF16)
    s_all = jnp.dot(qmb, kt, preferred_element_type=F32)
    for c in range(nb):
        s = s_all[:, c * MOBA_BLOCK:(c + 1) * MOBA_BLOCK]
        gate = jnp.sum(s, axis=1, keepdims=True) * (1.0 / MOBA_BLOCK)
        m = jnp.max(s, axis=1, keepdims=True)
        p = jnp.exp(s - m)
        l = jnp.sum(p, axis=1, keepdims=True)
        j = step * nb + c
        mp_s[...] = jnp.where(lane == j, m, mp_s[...])
        lp_s[...] = jnp.where(lane == j, l, lp_s[...])
        gp_s[...] = jnp.where(lane == j, gate, gp_s[...])
        pbd_s[c * rows:(c + 1) * rows, c * MOBA_BLOCK:(c + 1) * MOBA_BLOCK] = p.astype(BF16)
    op_s[pl.ds(pl.multiple_of(step * (nb * rows), nb * rows), nb * rows), :] = lax.dot_general(
        pbd_s[...], vt, contract_last, preferred_element_type=F32)

    @pl.when(step == ns - 1)
    def _():
        nblk = ns * nb
        sel = _top_blocks_bias(gp_s[...], lane, 128, 1) == 0.0
        zpad = jnp.zeros((128 - DEC_ROWS, WIDTH), F32)
        kn = jnp.concatenate([kn_ref[0], zpad], axis=0).astype(BF16)
        vn = jnp.concatenate([vn_ref[0], zpad], axis=0).astype(BF16)
        s = lax.dot_general(qmb, kn, contract_last, preferred_element_type=F32)
        qpos = jnp.minimum(lax.broadcasted_iota(jnp.int32, (rows, 128), 0) % DEC_ROWS, t_valid - 1)
        s = jnp.where(lane <= qpos, s, NEG_INF)
        m_own = jnp.max(s, axis=1, keepdims=True)
        p = jnp.exp(s - m_own)
        l_own = jnp.sum(p, axis=1, keepdims=True)
        o_own = jnp.dot(p.astype(BF16), vn, preferred_element_type=F32)
        mp = mp_s[...]
        m_fin = jnp.maximum(m_own, jnp.max(jnp.where(sel, mp, NEG_INF), axis=1, keepdims=True))
        w = jnp.where(sel, jnp.exp(mp - m_fin), 0.0)
        a_own = jnp.exp(m_own - m_fin)
        l_fin = a_own * l_own + jnp.sum(w * lp_s[...], axis=1, keepdims=True)
        o_fin = a_own * o_own
        for j in range(nblk):
            o_fin = o_fin + w[:, j:j + 1] * op_s[j * rows:(j + 1) * rows, :]
        y = o_fin / l_fin
        y_b = None
        for h in range(N_HEADS):
            part = jnp.where(lane_head == h, y[h * DEC_ROWS:(h + 1) * DEC_ROWS, :], 0.0)
            y_b = part if y_b is None else y_b + part
        yb_ref[0] = y_b


def _dec_call(page_table, cache_k, cache_v, layer, q, kn, vn, *, t_valid):
    bsz, n_pages = page_table.shape
    depth, n_pool, page, _, _ = cache_k.shape
    assert (n_pages * page) % MOBA_BLOCK == 0 and MOBA_BLOCK == 2 * page
    nblk = n_pages * page // MOBA_BLOCK
    assert MOBA_TOPK <= nblk <= 128
    pps = next(c for c in (16, 8, 4, 2) if n_pages % c == 0)
    ns = n_pages // pps
    ck = cache_k.transpose(0, 1, 3, 4, 2).reshape(depth, n_pool, WIDTH, page)
    cv = cache_v.transpose(0, 1, 3, 4, 2).reshape(depth, n_pool, WIDTH, page)

    def page_spec(c):
        return pl.BlockSpec((None, None, WIDTH, page), lambda b, s, pt: (layer, pt[b, s * pps + c], 0, 0))

    per_b = pl.BlockSpec((1, DEC_ROWS, WIDTH), lambda b, s, pt: (b, 0, 0))
    rows = N_HEADS * DEC_ROWS
    body = functools.partial(_dec_body, pps=pps, ns=ns, t_valid=t_valid)
    grid_spec = pltpu.PrefetchScalarGridSpec(
        num_scalar_prefetch=1,
        grid=(bsz, ns),
        in_specs=[page_spec(c) for c in range(pps)] * 2 + [per_b, per_b, per_b],
        out_specs=per_b,
        scratch_shapes=[pltpu.VMEM((rows, WIDTH), F32), pltpu.VMEM((rows, 128), F32),
                        pltpu.VMEM((rows, 128), F32), pltpu.VMEM((rows, 128), F32),
                        pltpu.VMEM((nblk * rows, WIDTH), F32),
                        pltpu.VMEM((pps // 2 * rows, pps * page), BF16)],
    )
    return pl.pallas_call(
        body,
        grid_spec=grid_spec,
        out_shape=jax.ShapeDtypeStruct((bsz, DEC_ROWS, WIDTH), F32),
        compiler_params=pltpu.CompilerParams(dimension_semantics=("arbitrary", "arbitrary"),
                                             vmem_limit_bytes=VMEM_LIMIT),
        name="attn_sample",
    )(page_table, *([ck] * pps), *([cv] * pps), q, kn, vn)


def _proj_body(yb_ref, mixed_ref, x_ref, wout_ref, fng_ref, out_ref, *, final):
    o = _out_proj(x_ref[...], mixed_ref[...], yb_ref[...], wout_ref)
    if final:
        o = _rmsnorm(o, fng_ref[...])
    out_ref[...] = o


def _proj_call(y_b, mixed, x, w_out, fng, *, final):
    bsz, t_len, d_model = x.shape
    n = bsz * t_len
    whole = lambda r, c: pl.BlockSpec((r, c), lambda i: (0, 0))
    out = pl.pallas_call(
        functools.partial(_proj_body, final=final),
        grid=(1,),
        in_specs=[whole(n, WIDTH), whole(n, 4 * WIDTH), whole(n, d_model), whole(4 * WIDTH, d_model),
                  whole(1, d_model)],
        out_specs=whole(n, d_model),
        out_shape=jax.ShapeDtypeStruct((n, d_model), F32),
        compiler_params=pltpu.CompilerParams(dimension_semantics=("arbitrary",), vmem_limit_bytes=VMEM_LIMIT),
        name="proj_sample",
    )(y_b.reshape(n, WIDTH), mixed.reshape(n, 4 * WIDTH), x.reshape(n, d_model), w_out, fng)
    return out.reshape(bsz, t_len, d_model)


def _rope_tables(pos0, n):
    freqs = ROPE_THETA ** (-jnp.arange(ROT_HALF, dtype=F32) * 2.0 / (2 * ROT_HALF))
    ang = (pos0 + jnp.arange(n, dtype=jnp.int32)).astype(F32)[:, None] * freqs[None, :]
    cos, sin = jnp.cos(ang), jnp.sin(ang)
    one = jnp.ones((n, HEAD_DIM - 2 * ROT_HALF), F32)
    zero8 = jnp.zeros((n, ROT_HALF), F32)
    zero_rest = jnp.zeros((n, HEAD_DIM - 2 * ROT_HALF), F32)
    rc = jnp.concatenate([cos, cos, one], axis=1)
    rsa = jnp.concatenate([-sin, zero8, zero_rest], axis=1)
    rsb = jnp.concatenate([zero8, sin, zero_rest], axis=1)
    return tuple(jnp.tile(a, (1, N_HEADS)) for a in (rc, rsa, rsb))


def _pad_state(state):
    return jnp.pad(state, ((0, 0), (HALO - state.shape[1], 0), (0, 0)))


def kernel(x_prompt, x_sample, cache_k, cache_v, page_table, state_conv_a, state_conv_c, state_pool_d,
           norm_g, w_in, conv_a_w, conv_c_w, conv_c_b, ln_c_g, ln_c_b, w_pw_c, w_pool, pool_scale, w_out,
           final_norm_g):
    depth = w_in.shape[0]
    bsz, seq, d_model = x_prompt.shape
    dec_b, dec_t, _ = x_sample.shape
    past_len = page_table.shape[1] * cache_k.shape[2]
    assert seq % MOBA_BLOCK == 0 and dec_t <= DEC_ROWS and WIDTH == N_HEADS * HEAD_DIM

    tabs_p = _rope_tables(0, seq)
    tabs_s = _rope_tables(past_len, DEC_ROWS)
    zero_state = jnp.zeros((bsz, HALO, WIDTH), F32)
    fng = final_norm_g.reshape(1, d_model)
    yp = x_prompt
    ys = jnp.pad(x_sample, ((0, 0), (0, DEC_ROWS - dec_t), (0, 0)))
    outs = [[] for _ in range(10)]
    n_groups = w_pool.shape[1]
    for l in range(depth):
        pool_bd = jax.scipy.linalg.block_diag(*[w_pool[l, g] for g in range(n_groups)])
        lw = {"norm_g": norm_g[l].reshape(1, d_model), "w_in": w_in[l].astype(BF16),
              "conv_a_w": conv_a_w[l], "conv_c_w": conv_c_w[l], "conv_c_b": conv_c_b[l].reshape(1, WIDTH),
              "ln_c_g": ln_c_g[l].reshape(1, WIDTH), "ln_c_b": ln_c_b[l].reshape(1, WIDTH),
              "w_pw_c": w_pw_c[l].astype(BF16), "w_pool": pool_bd.astype(BF16),
              "pool_scale": pool_scale[l].reshape(1, WIDTH)}
        wo = w_out[l].astype(BF16)
        final = l == depth - 1

        mixed, kp, vp, qt, kb, vt, km, ap, cp, dp = _pre_call(
            yp, lw, tabs_p, zero_state, zero_state, zero_state,
            tm=MOBA_BLOCK, last_rows=MOBA_BLOCK, pos0=0, prompt=True)
        yp = _attn_call(qt, kb, vt, km, mixed, yp, wo, fng, final=final)

        mixed_s, ks, vs, qs, as_, cs, ds = _pre_call(
            ys, lw, tabs_s, _pad_state(state_conv_a[l]), _pad_state(state_conv_c[l]),
            _pad_state(state_pool_d[l]), tm=DEC_ROWS, last_rows=dec_t, pos0=past_len, prompt=False)
        yb_s = _dec_call(page_table, cache_k, cache_v, l, qs, ks, vs, t_valid=dec_t)
        ys = _proj_call(yb_s, mixed_s, ys, wo, fng, final=final)

        for lst, val in zip(outs, (kp.reshape(bsz, seq, N_HEADS, HEAD_DIM),
                                   vp.reshape(bsz, seq, N_HEADS, HEAD_DIM),
                                   ks[:, :dec_t].reshape(dec_b, dec_t, N_HEADS, HEAD_DIM),
                                   vs[:, :dec_t].reshape(dec_b, dec_t, N_HEADS, HEAD_DIM),
                                   ap, as_, cp, cs, dp, ds)):
            lst.append(val)
    return (yp, ys[:, :dec_t]) + tuple(jnp.stack(o) for o in outs)
```

```python
import functools

import jax
import jax.numpy as jnp
from jax import lax
from jax.experimental import pallas as pl
from jax.experimental.pallas import tpu as pltpu

F32 = jnp.float32
BF16 = jnp.bfloat16

WIDTH = 256
N_HEADS = 4
HEAD_DIM = 64
ROT_HALF = 8
ROPE_THETA = 500000.0
MOBA_BLOCK = 256
MOBA_TOPK = 3
CONV_A_WIDTH = 3
CONV_C_WIDTH = 31
POOL_WINDOWS = (2, 4, 8, 16)
POOL_STATE = 15
RMS_EPS = 1e-6
LN_EPS = 1e-5
HALO = 32
DEC_ROWS = 8
ATTN_GROUP = 4
NEG_INF = float("-inf")
LOG2_E = 1.4426950408889634
VMEM_LIMIT = 56 * 1024 * 1024


def _silu(z):
    return z * jax.nn.sigmoid(z)


def _rmsnorm(x, g):
    ms = jnp.mean(x * x, axis=-1, keepdims=True)
    return x * lax.rsqrt(ms + RMS_EPS) * g


def _out_proj(x, mixed, y_b, wout_ref):
    yb = (y_b * mixed[:, WIDTH:2 * WIDTH]).astype(BF16)
    o = x + jnp.dot(mixed[:, 0:WIDTH].astype(BF16), wout_ref[0:WIDTH, :], preferred_element_type=F32)
    o = o + jnp.dot(yb, wout_ref[WIDTH:2 * WIDTH, :], preferred_element_type=F32)
    o = o + jnp.dot(mixed[:, 2 * WIDTH:4 * WIDTH].astype(BF16), wout_ref[2 * WIDTH:4 * WIDTH, :],
                    preferred_element_type=F32)
    return o


def _pre_body(x_ref, ng_ref, win_ref, caw_ref, ccw_ref, ccb_ref, lng_ref, lnb_ref, wpw_ref, wpool_ref,
              psc_ref, rc_ref, rsa_ref, rsb_ref, prea_ref, prec_ref, pred_ref, *refs,
              ns, tm, last_rows, pos0, prompt):
    if prompt:
        (mixed_ref, k_ref, v_ref, qt_ref, kb_ref, vt_ref, km_ref,
         na_ref, nc_ref, nd_ref, bufa, bufc, bufd, shc, x2, x4, x8) = refs
    else:
        (mixed_ref, k_ref, v_ref, q_ref, na_ref, nc_ref, nd_ref, bufa, bufc, bufd, shc, x2, x4, x8) = refs
    t = pl.program_id(1)
    nt = pl.num_programs(1)
    span = tm + HALO - 8
    n = ns * tm
    seq = lambda z: z.reshape(ns, tm, WIDTH)
    flat = lambda z: z.reshape(n, WIDTH)

    @pl.when(t == 0)
    def _():
        bufa[:, 0:HALO, :] = prea_ref[...]
        bufc[:, 0:HALO, :] = prec_ref[...]
        bufd[:, 0:HALO, :] = pred_ref[...]

    x = x_ref[...].reshape(n, x_ref.shape[-1])
    h = _rmsnorm(x, ng_ref[...]).astype(BF16)

    def proj(c0, cn):
        return jnp.dot(h, win_ref[:, c0 * WIDTH:(c0 + cn) * WIDTH], preferred_element_type=F32)

    def put(col, val):
        mixed_ref[:, :, col * WIDTH:(col + 1) * WIDTH] = seq(val)

    pa = proj(0, 4)
    bufa[:, HALO:HALO + tm, :] = seq(pa[:, 2 * WIDTH:3 * WIDTH] * pa[:, 0:WIDTH])
    conv = None
    for j in range(CONV_A_WIDTH):
        off = HALO - (CONV_A_WIDTH - 1) + j
        term = caw_ref[j:j + 1, :] * bufa[:, off:off + tm, :]
        conv = term if conv is None else conv + term
    put(0, pa[:, WIDTH:2 * WIDTH] * flat(conv) * _silu(pa[:, 3 * WIDTH:4 * WIDTH]))

    pb = proj(4, 4)
    tab = lambda r: flat(jnp.broadcast_to(r[...][None], (ns, tm, WIDTH)))
    rc, rsa, rsb = tab(rc_ref), tab(rsa_ref), tab(rsb_ref)

    def rope(z):
        return z * rc + pltpu.roll(z, WIDTH - ROT_HALF, 1) * rsa + pltpu.roll(z, ROT_HALF, 1) * rsb

    qs = rope(pb[:, 0:WIDTH]) * (HEAD_DIM ** -0.5)
    kr = rope(pb[:, WIDTH:2 * WIDTH])
    v = pb[:, 2 * WIDTH:3 * WIDTH]
    k_ref[...] = seq(kr)
    v_ref[...] = seq(v)
    put(1, _silu(pb[:, 3 * WIDTH:4 * WIDTH]))
    if prompt:
        qt_ref[0, 0] = qs.T
        kb_ref[0, 0] = kr.astype(BF16)
        vt_ref[0, 0] = v.T.astype(BF16)
        km_ref[0, 0] = jnp.mean(kr, axis=0, keepdims=True)
    else:
        q_ref[...] = seq(qs)

    pc = proj(8, 3)
    bufc[:, HALO:HALO + tm, :] = seq(pc[:, 0:WIDTH] * jax.nn.sigmoid(pc[:, WIDTH:2 * WIDTH]))
    for r in range(1, 8):
        shc[r - 1] = bufc[:, r:r + span, :]
    acc = None
    for j in range(CONV_C_WIDTH):
        a, r = divmod(HALO - (CONV_C_WIDTH - 1) + j, 8)
        rows = bufc[:, 8 * a:8 * a + tm, :] if r == 0 else shc[r - 1, :, 8 * a:8 * a + tm, :]
        term = ccw_ref[j:j + 1, :] * rows
        acc = term if acc is None else acc + term
    acc = flat(acc) + ccb_ref[...]
    mu = jnp.mean(acc, axis=-1, keepdims=True)
    cen = acc - mu
    var = jnp.mean(cen * cen, axis=-1, keepdims=True)
    ln = cen * lax.rsqrt(var + LN_EPS) * lng_ref[...] + lnb_ref[...]
    yc = jnp.dot(_silu(ln).astype(BF16), wpw_ref[...], preferred_element_type=F32)
    put(2, yc * _silu(pc[:, 2 * WIDTH:3 * WIDTH]))

    pd = proj(11, 2)
    dx = pd[:, 0:WIDTH]
    bufd[:, HALO:HALO + tm, :] = seq(dx)
    assert POOL_WINDOWS == (2, 4, 8, 16) and HALO == 32
    x2[...] = bufd[:, 8:8 + span, :] + bufd[:, 7:7 + span, :]
    x4[:, 0:span - 8, :] = x2[:, 8:span, :] + x2[:, 6:span - 2, :]
    x8[:, 0:span - 16, :] = x4[:, 8:span - 8, :] + x4[:, 4:span - 12, :]
    sums = {2: x2[:, 24:24 + tm, :], 4: x4[:, 16:16 + tm, :], 8: x8[:, 8:8 + tm, :],
            16: x8[:, 8:8 + tm, :] + x8[:, 0:tm, :]}
    group = lax.broadcasted_iota(jnp.int32, (ns, tm, WIDTH), 2) // (WIDTH // len(POOL_WINDOWS))
    pos = pos0 + t * tm + lax.broadcasted_iota(jnp.int32, (ns, tm, WIDTH), 1)
    wsum = sums[POOL_WINDOWS[-1]]
    win = jnp.full((ns, tm, WIDTH), POOL_WINDOWS[-1], jnp.int32)
    for g in range(len(POOL_WINDOWS) - 2, -1, -1):
        wsum = jnp.where(group == g, sums[POOL_WINDOWS[g]], wsum)
        win = jnp.where(group == g, POOL_WINDOWS[g], win)
    cnt = jnp.minimum(pos + 1, win).astype(F32)
    diff = flat(wsum / cnt) - dx
    yd = jnp.dot(diff.astype(BF16), wpool_ref[...], preferred_element_type=F32) * psc_ref[...]
    put(3, yd * _silu(pd[:, WIDTH:2 * WIDTH]))

    @pl.when(t == nt - 1)
    def _():
        end = HALO + last_rows
        na_ref[...] = bufa[:, end - (CONV_A_WIDTH - 1):end, :]
        nc_ref[...] = bufc[:, end - (CONV_C_WIDTH - 1):end, :]
        nd_ref[...] = bufd[:, end - POOL_STATE:end, :]

    if tm >= HALO:
        @pl.when(t < nt - 1)
        def _():
            bufa[:, 0:HALO, :] = bufa[:, tm:tm + HALO, :]
            bufc[:, 0:HALO, :] = bufc[:, tm:tm + HALO, :]
            bufd[:, 0:HALO, :] = bufd[:, tm:tm + HALO, :]


def _pre_call(x, lw, rope_tabs, pre_a, pre_c, pre_d, *, ns, tm, last_rows, pos0, prompt):
    bsz, t_len, d_model = x.shape
    nt = t_len // tm
    assert nt * tm == t_len and (nt == 1 or tm >= HALO) and bsz % ns == 0 and tm % 8 == 0
    rc, rsa, rsb = rope_tabs
    d_in = lw["w_in"].shape[1]

    def const(shape):
        return pl.BlockSpec(shape, lambda b, t: (0,) * len(shape))

    def per_b(shape):
        return pl.BlockSpec((ns,) + shape, lambda b, t: (b,) + (0,) * len(shape))

    tok = lambda w: pl.BlockSpec((ns, tm, w), lambda b, t: (b, t, 0))
    tab = pl.BlockSpec((tm, WIDTH), lambda b, t: (t, 0))
    in_specs = [tok(d_model), const((1, d_model)), const((d_model, d_in)),
                const((CONV_A_WIDTH, WIDTH)), const((CONV_C_WIDTH, WIDTH)), const((1, WIDTH)),
                const((1, WIDTH)), const((1, WIDTH)), const((WIDTH, WIDTH)), const((WIDTH, WIDTH)),
                const((1, WIDTH)), tab, tab, tab,
                per_b((HALO, WIDTH)), per_b((HALO, WIDTH)), per_b((HALO, WIDTH))]
    out_shape = [jax.ShapeDtypeStruct((bsz, t_len, 4 * WIDTH), F32),
                 jax.ShapeDtypeStruct((bsz, t_len, WIDTH), F32),
                 jax.ShapeDtypeStruct((bsz, t_len, WIDTH), F32)]
    out_specs = [tok(4 * WIDTH), tok(WIDTH), tok(WIDTH)]
    if prompt:
        assert tm == MOBA_BLOCK and ns == 1
        blk = lambda r, c: pl.BlockSpec((1, 1, r, c), lambda b, t: (b, t, 0, 0))
        out_shape += [jax.ShapeDtypeStruct((bsz, nt, WIDTH, tm), F32),
                      jax.ShapeDtypeStruct((bsz, nt, tm, WIDTH), BF16),
                      jax.ShapeDtypeStruct((bsz, nt, WIDTH, tm), BF16),
                      jax.ShapeDtypeStruct((bsz, nt, 1, WIDTH), F32)]
        out_specs += [blk(WIDTH, tm), blk(tm, WIDTH), blk(WIDTH, tm), blk(1, WIDTH)]
    else:
        out_shape += [jax.ShapeDtypeStruct((bsz, t_len, WIDTH), F32)]
        out_specs += [tok(WIDTH)]
    out_shape += [jax.ShapeDtypeStruct((bsz, CONV_A_WIDTH - 1, WIDTH), F32),
                  jax.ShapeDtypeStruct((bsz, CONV_C_WIDTH - 1, WIDTH), F32),
                  jax.ShapeDtypeStruct((bsz, POOL_STATE, WIDTH), F32)]
    out_specs += [per_b((CONV_A_WIDTH - 1, WIDTH)), per_b((CONV_C_WIDTH - 1, WIDTH)),
                  per_b((POOL_STATE, WIDTH))]
    body = functools.partial(_pre_body, ns=ns, tm=tm, last_rows=last_rows, pos0=pos0, prompt=prompt)
    return pl.pallas_call(
        body,
        grid=(bsz // ns, nt),
        in_specs=in_specs,
        out_specs=out_specs,
        out_shape=out_shape,
        scratch_shapes=[pltpu.VMEM((ns, HALO + tm, WIDTH), F32)] * 3
        + [pltpu.VMEM((7, ns, HALO + tm - 8, WIDTH), F32)] + [pltpu.VMEM((ns, HALO + tm - 8, WIDTH), F32)] * 3,
        compiler_params=pltpu.CompilerParams(dimension_semantics=("arbitrary", "arbitrary"),
                                             vmem_limit_bytes=VMEM_LIMIT),
        name="pre_prompt" if prompt else "pre_sample",
    )(x, lw["norm_g"], lw["w_in"], lw["conv_a_w"], lw["conv_c_w"], lw["conv_c_b"], lw["ln_c_g"],
      lw["ln_c_b"], lw["w_pw_c"], lw["w_pool"], lw["pool_scale"], rc, rsa, rsb, pre_a, pre_c, pre_d)


def _top_blocks_bias(gate, index, n_index, axis):
    bias = jnp.full(gate.shape, NEG_INF, F32)
    for _ in range(MOBA_TOPK):
        top = jnp.max(gate, axis=axis, keepdims=True)
        cand = jnp.logical_and(gate == top, top > NEG_INF)
        first = jnp.min(jnp.where(cand, index, n_index), axis=axis, keepdims=True)
        pick = index == first
        bias = jnp.where(pick, 0.0, bias)
        gate = jnp.where(pick, NEG_INF, gate)
    return bias


def _attn_body(qt_ref, kb_ref, vt_ref, km_ref, mixed_ref, x_ref, wout_ref, fng_ref, out_ref,
               qm_s, bias_s, acc_s, s_a, s_b, p_a, p_b, al_a, al_b, *, nblk, tq, grp, final):
    s_bufs, p_bufs, al_bufs = (s_a, s_b), (p_a, p_b), (al_a, al_b)
    w4 = N_HEADS * tq
    i = pl.program_id(1)
    qt = qt_ref[0, 0]
    row_head = lax.broadcasted_iota(jnp.int32, (WIDTH, tq), 0) // HEAD_DIM
    km = km_ref[0]
    lane_head = lax.broadcasted_iota(jnp.int32, (nblk, WIDTH), 1) // HEAD_DIM
    blk = lax.broadcasted_iota(jnp.int32, (nblk, tq), 0)
    km_heads = jnp.concatenate([jnp.where(lane_head == h, km, 0.0) for h in range(N_HEADS)], axis=0)
    gates = jnp.dot(km_heads, qt, precision=lax.Precision.HIGHEST, preferred_element_type=F32)
    for h in range(N_HEADS):
        cols = slice(h * tq, (h + 1) * tq)
        qm_s[:, cols] = jnp.where(row_head == h, qt * LOG2_E, 0.0).astype(BF16)
        gate = jnp.where(blk < i, gates[h * nblk:(h + 1) * nblk, :], NEG_INF)
        bias_s[:, cols] = _top_blocks_bias(gate, blk, nblk, 0)

    n_groups = nblk // grp
    sub = [slice(c * MOBA_BLOCK, (c + 1) * MOBA_BLOCK) for c in range(grp)]

    def qk(group, slot):
        kb = kb_ref[0, pl.ds(group * grp, grp)].reshape(grp * MOBA_BLOCK, WIDTH)
        s_bufs[slot][:, 0:w4] = jnp.dot(kb, qm_s[...], preferred_element_type=F32)

    def pv(group, slot):
        for h in range(N_HEADS):
            rows = slice(h * HEAD_DIM, (h + 1) * HEAD_DIM)
            cols = slice(h * tq, (h + 1) * tq)
            vt = jnp.concatenate([vt_ref[0, group * grp + c, rows, :] for c in range(grp)], axis=1)
            o = jnp.dot(vt, p_bufs[slot][:, cols], preferred_element_type=F32)
            acc_s[rows, :] = al_bufs[slot][:, cols] * acc_s[rows, :] + o

    qk(0, 1)
    key = lax.broadcasted_iota(jnp.int32, (MOBA_BLOCK, w4), 0)
    qry = lax.broadcasted_iota(jnp.int32, (MOBA_BLOCK, w4), 1) % tq
    s = jnp.dot(kb_ref[0, i], qm_s[...], preferred_element_type=F32)
    s = jnp.where(key <= qry, s, NEG_INF)
    m = jnp.max(s, axis=0, keepdims=True)
    p = jnp.exp2(s - m)
    l = jnp.sum(p, axis=0, keepdims=True)
    pb = p.astype(BF16)
    for h in range(N_HEADS):
        rows = slice(h * HEAD_DIM, (h + 1) * HEAD_DIM)
        acc_s[rows, :] = jnp.dot(vt_ref[0, i, rows, :], pb[:, h * tq:(h + 1) * tq], preferred_element_type=F32)
    p_bufs[0][:, 0:w4] = jnp.zeros((grp * MOBA_BLOCK, w4), BF16)
    al_bufs[0][...] = jnp.ones((1, w4), F32)

    def half(n, m, l, cur, nxt):
        picked = [bias_s[pl.ds((n - 1) * grp + c, 1), :] for c in range(grp)]
        m_new = m
        for c in range(grp):
            m_new = jnp.maximum(m_new, jnp.max(s_bufs[cur][sub[c], 0:w4], axis=0, keepdims=True) + picked[c])
        alpha = jnp.exp2(m - m_new)
        l = alpha * l
        for c in range(grp):
            p = jnp.exp2(s_bufs[cur][sub[c], 0:w4] - (m_new - picked[c]))
            l = l + jnp.sum(p, axis=0, keepdims=True)
            p_bufs[cur][sub[c], 0:w4] = p.astype(BF16)
        al_bufs[cur][...] = alpha
        pv(jnp.maximum(n - 2, 0), nxt)
        qk(jnp.minimum(n, n_groups - 1), nxt)
        return m_new, l

    def visit(n, carry):
        return lax.cond(n % 2 == 1, lambda: half(n, carry[0], carry[1], 1, 0),
                        lambda: half(n, carry[0], carry[1], 0, 1))

    n_visits = (i + grp - 1) // grp
    m, l = lax.fori_loop(1, n_visits + 1, visit, (m, l))
    odd = n_visits % 2 == 1

    @pl.when(odd)
    def _():
        pv(n_visits - 1, 1)

    @pl.when(jnp.logical_not(odd))
    def _():
        pv(jnp.maximum(n_visits - 1, 0), 0)

    inv = 1.0 / l
    for h in range(N_HEADS):
        rows = slice(h * HEAD_DIM, (h + 1) * HEAD_DIM)
        acc_s[rows, :] = acc_s[rows, :] * inv[:, h * tq:(h + 1) * tq]
    y_b = acc_s[...].T
    o = _out_proj(x_ref[0], mixed_ref[0], y_b, wout_ref)
    if final:
        o = _rmsnorm(o, fng_ref[...])
    out_ref[0] = o


def _attn_call(qt, kb, vt, km, mixed, x, w_out, fng, *, final):
    bsz, nblk, _, tq = qt.shape
    d_model = x.shape[-1]
    assert tq == MOBA_BLOCK
    grp = next(g for g in (ATTN_GROUP, 2, 1) if nblk % g == 0)
    body = functools.partial(_attn_body, nblk=nblk, tq=tq, grp=grp, final=final)
    whole = lambda r, c: pl.BlockSpec((1, nblk, r, c), lambda b, i: (b, 0, 0, 0))
    return pl.pallas_call(
        body,
        grid=(bsz, nblk),
        in_specs=[pl.BlockSpec((1, 1, WIDTH, tq), lambda b, i: (b, i, 0, 0)),
                  whole(MOBA_BLOCK, WIDTH), whole(WIDTH, MOBA_BLOCK),
                  pl.BlockSpec((1, nblk, WIDTH), lambda b, i: (b, 0, 0)),
                  pl.BlockSpec((1, tq, 4 * WIDTH), lambda b, i: (b, i, 0)),
                  pl.BlockSpec((1, tq, d_model), lambda b, i: (b, i, 0)),
                  pl.BlockSpec((4 * WIDTH, d_model), lambda b, i: (0, 0)),
                  pl.BlockSpec((1, d_model), lambda b, i: (0, 0))],
        out_specs=pl.BlockSpec((1, tq, d_model), lambda b, i: (b, i, 0)),
        out_shape=jax.ShapeDtypeStruct(x.shape, F32),
        scratch_shapes=[pltpu.VMEM((WIDTH, N_HEADS * tq), BF16),
                        pltpu.VMEM((nblk, N_HEADS * tq), F32),
                        pltpu.VMEM((WIDTH, tq), F32)]
        + [pltpu.VMEM((grp * MOBA_BLOCK, N_HEADS * tq + 128), F32)] * 2
        + [pltpu.VMEM((grp * MOBA_BLOCK, N_HEADS * tq + 128), BF16)] * 2
        + [pltpu.VMEM((1, N_HEADS * tq), F32)] * 2,
        compiler_params=pltpu.CompilerParams(dimension_semantics=("arbitrary", "arbitrary"),
                                             vmem_limit_bytes=VMEM_LIMIT),
        name="attn_prompt",
    )(qt, kb, vt, km.reshape(bsz, nblk, WIDTH), mixed, x, w_out, fng)


def _dec_body(pt_ref, *refs, pps, ns, t_valid):
    del pt_ref
    kpages, vpages = refs[0:pps], refs[pps:2 * pps]
    q_ref, kn_ref, vn_ref, yb_ref, qm_s, mp_s, lp_s, gp_s, op_s, pbd_s = refs[2 * pps:]
    step = pl.program_id(1)
    rows = N_HEADS * DEC_ROWS
    nb = pps // 2
    lane_head = lax.broadcasted_iota(jnp.int32, (DEC_ROWS, WIDTH), 1) // HEAD_DIM
    lane = lax.broadcasted_iota(jnp.int32, (rows, 128), 1)

    @pl.when(jnp.logical_and(pl.program_id(0) == 0, step == 0))
    def _():
        pbd_s[...] = jnp.zeros(pbd_s.shape, BF16)

    @pl.when(step == 0)
    def _():
        q8 = q_ref[0]
        for h in range(N_HEADS):
            qm_s[h * DEC_ROWS:(h + 1) * DEC_ROWS, :] = jnp.where(lane_head == h, q8, 0.0)
        mp_s[...] = jnp.full((rows, 128), NEG_INF, F32)
        gp_s[...] = jnp.full((rows, 128), NEG_INF, F32)
        lp_s[...] = jnp.zeros((rows, 128), F32)

    qmb = qm_s[...].astype(BF16)
    contract_last = (((1,), (1,)), ((), ()))
    kt = jnp.concatenate([kpages[c][...] for c in range(pps)], axis=1).astype(BF16)
    vt = jnp.concatenate([vpages[c][...] for c in range(pps)], axis=1).astype(BF16)
    s_all = jnp.dot(qmb, kt, preferred_element_type=F32)
    for c in range(nb):
        s = s_all[:, c * MOBA_BLOCK:(c + 1) * MOBA_BLOCK]
        gate = jnp.sum(s, axis=1, keepdims=True) * (1.0 / MOBA_BLOCK)
        m = jnp.max(s, axis=1, keepdims=True)
        p = jnp.exp(s - m)
        l = jnp.sum(p, axis=1, keepdims=True)
        j = step * nb + c
        mp_s[...] = jnp.where(lane == j, m, mp_s[...])
        lp_s[...] = jnp.where(lane == j, l, lp_s[...])
        gp_s[...] = jnp.where(lane == j, gate, gp_s[...])
        pbd_s[c * rows:(c + 1) * rows, c * MOBA_BLOCK:(c + 1) * MOBA_BLOCK] = p.astype(BF16)
    op_s[pl.ds(pl.multiple_of(step * (nb * rows), nb * rows), nb * rows), :] = lax.dot_general(
        pbd_s[...], vt, contract_last, preferred_element_type=F32)

    @pl.when(step == ns - 1)
    def _():
        nblk = ns * nb
        sel = _top_blocks_bias(gp_s[...], lane, 128, 1) == 0.0
        zpad = jnp.zeros((128 - DEC_ROWS, WIDTH), F32)
        kn = jnp.concatenate([kn_ref[0], zpad], axis=0).astype(BF16)
        vn = jnp.concatenate([vn_ref[0], zpad], axis=0).astype(BF16)
        s = lax.dot_general(qmb, kn, contract_last, preferred_element_type=F32)
        qpos = jnp.minimum(lax.broadcasted_iota(jnp.int32, (rows, 128), 0) % DEC_ROWS, t_valid - 1)
        s = jnp.where(lane <= qpos, s, NEG_INF)
        m_own = jnp.max(s, axis=1, keepdims=True)
        p = jnp.exp(s - m_own)
        l_own = jnp.sum(p, axis=1, keepdims=True)
        o_own = jnp.dot(p.astype(BF16), vn, preferred_element_type=F32)
        mp = mp_s[...]
        m_fin = jnp.maximum(m_own, jnp.max(jnp.where(sel, mp, NEG_INF), axis=1, keepdims=True))
        w = jnp.where(sel, jnp.exp(mp - m_fin), 0.0)
        a_own = jnp.exp(m_own - m_fin)
        l_fin = a_own * l_own + jnp.sum(w * lp_s[...], axis=1, keepdims=True)
        o_fin = a_own * o_own
        for j in range(nblk):
            o_fin = o_fin + w[:, j:j + 1] * op_s[j * rows:(j + 1) * rows, :]
        y = o_fin / l_fin
        y_b = None
        for h in range(N_HEADS):
            part = jnp.where(lane_head == h, y[h * DEC_ROWS:(h + 1) * DEC_ROWS, :], 0.0)
            y_b = part if y_b is None else y_b + part
        yb_ref[0] = y_b


def _dec_call(page_table, cache_k, cache_v, layer, q, kn, vn, *, t_valid):
    bsz, n_pages = page_table.shape
    depth, n_pool, page, _, _ = cache_k.shape
    assert (n_pages * page) % MOBA_BLOCK == 0 and MOBA_BLOCK == 2 * page
    nblk = n_pages * page // MOBA_BLOCK
    assert MOBA_TOPK <= nblk <= 128
    pps = next(c for c in (16, 8, 4, 2) if n_pages % c == 0)
    ns = n_pages // pps
    ck = cache_k.transpose(0, 1, 3, 4, 2).reshape(depth, n_pool, WIDTH, page)
    cv = cache_v.transpose(0, 1, 3, 4, 2).reshape(depth, n_pool, WIDTH, page)

    def page_spec(c):
        return pl.BlockSpec((None, None, WIDTH, page), lambda b, s, pt: (layer, pt[b, s * pps + c], 0, 0))

    per_b = pl.BlockSpec((1, DEC_ROWS, WIDTH), lambda b, s, pt: (b, 0, 0))
    rows = N_HEADS * DEC_ROWS
    body = functools.partial(_dec_body, pps=pps, ns=ns, t_valid=t_valid)
    grid_spec = pltpu.PrefetchScalarGridSpec(
        num_scalar_prefetch=1,
        grid=(bsz, ns),
        in_specs=[page_spec(c) for c in range(pps)] * 2 + [per_b, per_b, per_b],
        out_specs=per_b,
        scratch_shapes=[pltpu.VMEM((rows, WIDTH), F32), pltpu.VMEM((rows, 128), F32),
                        pltpu.VMEM((rows, 128), F32), pltpu.VMEM((rows, 128), F32),
                        pltpu.VMEM((nblk * rows, WIDTH), F32),
                        pltpu.VMEM((pps // 2 * rows, pps * page), BF16)],
    )
    return pl.pallas_call(
        body,
        grid_spec=grid_spec,
        out_shape=jax.ShapeDtypeStruct((bsz, DEC_ROWS, WIDTH), F32),
        compiler_params=pltpu.CompilerParams(dimension_semantics=("arbitrary", "arbitrary"),
                                             vmem_limit_bytes=VMEM_LIMIT),
        name="attn_sample",
    )(page_table, *([ck] * pps), *([cv] * pps), q, kn, vn)


def _proj_body(yb_ref, mixed_ref, x_ref, wout_ref, fng_ref, out_ref, *, final):
    o = _out_proj(x_ref[...], mixed_ref[...], yb_ref[...], wout_ref)
    if final:
        o = _rmsnorm(o, fng_ref[...])
    out_ref[...] = o


def _proj_call(y_b, mixed, x, w_out, fng, *, final):
    bsz, t_len, d_model = x.shape
    n = bsz * t_len
    whole = lambda r, c: pl.BlockSpec((r, c), lambda i: (0, 0))
    out = pl.pallas_call(
        functools.partial(_proj_body, final=final),
        grid=(1,),
        in_specs=[whole(n, WIDTH), whole(n, 4 * WIDTH), whole(n, d_model), whole(4 * WIDTH, d_model),
                  whole(1, d_model)],
        out_specs=whole(n, d_model),
        out_shape=jax.ShapeDtypeStruct((n, d_model), F32),
        compiler_params=pltpu.CompilerParams(dimension_semantics=("arbitrary",), vmem_limit_bytes=VMEM_LIMIT),
        name="proj_sample",
    )(y_b.reshape(n, WIDTH), mixed.reshape(n, 4 * WIDTH), x.reshape(n, d_model), w_out, fng)
    return out.reshape(bsz, t_len, d_model)


def _rope_tables(pos0, n):
    freqs = ROPE_THETA ** (-jnp.arange(ROT_HALF, dtype=F32) * 2.0 / (2 * ROT_HALF))
    ang = (pos0 + jnp.arange(n, dtype=jnp.int32)).astype(F32)[:, None] * freqs[None, :]
    cos, sin = jnp.cos(ang), jnp.sin(ang)
    one = jnp.ones((n, HEAD_DIM - 2 * ROT_HALF), F32)
    zero8 = jnp.zeros((n, ROT_HALF), F32)
    zero_rest = jnp.zeros((n, HEAD_DIM - 2 * ROT_HALF), F32)
    rc = jnp.concatenate([cos, cos, one], axis=1)
    rsa = jnp.concatenate([-sin, zero8, zero_rest], axis=1)
    rsb = jnp.concatenate([zero8, sin, zero_rest], axis=1)
    return tuple(jnp.tile(a, (1, N_HEADS)) for a in (rc, rsa, rsb))


def _pad_state(state):
    return jnp.pad(state, ((0, 0), (HALO - state.shape[1], 0), (0, 0)))


def kernel(x_prompt, x_sample, cache_k, cache_v, page_table, state_conv_a, state_conv_c, state_pool_d,
           norm_g, w_in, conv_a_w, conv_c_w, conv_c_b, ln_c_g, ln_c_b, w_pw_c, w_pool, pool_scale, w_out,
           final_norm_g):
    depth = w_in.shape[0]
    bsz, seq, d_model = x_prompt.shape
    dec_b, dec_t, _ = x_sample.shape
    past_len = page_table.shape[1] * cache_k.shape[2]
    assert seq % MOBA_BLOCK == 0 and dec_t <= DEC_ROWS and WIDTH == N_HEADS * HEAD_DIM

    tabs_p = _rope_tables(0, seq)
    tabs_s = _rope_tables(past_len, DEC_ROWS)
    zero_state = jnp.zeros((bsz, HALO, WIDTH), F32)
    fng = final_norm_g.reshape(1, d_model)
    yp = x_prompt
    ys = jnp.pad(x_sample, ((0, 0), (0, DEC_ROWS - dec_t), (0, 0)))
    outs = [[] for _ in range(10)]
    n_groups = w_pool.shape[1]
    for l in range(depth):
        pool_bd = jax.scipy.linalg.block_diag(*[w_pool[l, g] for g in range(n_groups)])
        lw = {"norm_g": norm_g[l].reshape(1, d_model), "w_in": w_in[l].astype(BF16),
              "conv_a_w": conv_a_w[l], "conv_c_w": conv_c_w[l], "conv_c_b": conv_c_b[l].reshape(1, WIDTH),
              "ln_c_g": ln_c_g[l].reshape(1, WIDTH), "ln_c_b": ln_c_b[l].reshape(1, WIDTH),
              "w_pw_c": w_pw_c[l].astype(BF16), "w_pool": pool_bd.astype(BF16),
              "pool_scale": pool_scale[l].reshape(1, WIDTH)}
        wo = w_out[l].astype(BF16)
        final = l == depth - 1

        mixed, kp, vp, qt, kb, vt, km, ap, cp, dp = _pre_call(
            yp, lw, tabs_p, zero_state, zero_state, zero_state,
            ns=1, tm=MOBA_BLOCK, last_rows=MOBA_BLOCK, pos0=0, prompt=True)
        yp = _attn_call(qt, kb, vt, km, mixed, yp, wo, fng, final=final)

        mixed_s, ks, vs, qs, as_, cs, ds = _pre_call(
            ys, lw, tabs_s, _pad_state(state_conv_a[l]), _pad_state(state_conv_c[l]),
            _pad_state(state_pool_d[l]), ns=dec_b, tm=DEC_ROWS, last_rows=dec_t, pos0=past_len, prompt=False)
        yb_s = _dec_call(page_table, cache_k, cache_v, l, qs, ks, vs, t_valid=dec_t)
        ys = _proj_call(yb_s, mixed_s, ys, wo, fng, final=final)

        for lst, val in zip(outs, (kp.reshape(bsz, seq, N_HEADS, HEAD_DIM),
                                   vp.reshape(bsz, seq, N_HEADS, HEAD_DIM),
                                   ks[:, :dec_t].reshape(dec_b, dec_t, N_HEADS, HEAD_DIM),
                                   vs[:, :dec_t].reshape(dec_b, dec_t, N_HEADS, HEAD_DIM),
                                   ap, as_, cp, cs, dp, ds)):
            lst.append(val)
    return (yp, ys[:, :dec_t]) + tuple(jnp.stack(o) for o in outs)
```

```python
import functools

import jax
import jax.numpy as jnp
from jax import lax
from jax.experimental import pallas as pl
from jax.experimental.pallas import tpu as pltpu

F32 = jnp.float32
BF16 = jnp.bfloat16

WIDTH = 256
N_HEADS = 4
HEAD_DIM = 64
ROT_HALF = 8
ROPE_THETA = 500000.0
MOBA_BLOCK = 256
MOBA_TOPK = 3
CONV_A_WIDTH = 3
CONV_C_WIDTH = 31
POOL_WINDOWS = (2, 4, 8, 16)
POOL_STATE = 15
RMS_EPS = 1e-6
LN_EPS = 1e-5
HALO = 32
DEC_ROWS = 8
ATTN_GROUP = 4
PRE_TILE = 1024
NEG_INF = float("-inf")
LOG2_E = 1.4426950408889634
VMEM_LIMIT = 56 * 1024 * 1024


def _silu(z):
    return z * jax.nn.sigmoid(z)


def _rmsnorm(x, g):
    ms = jnp.mean(x * x, axis=-1, keepdims=True)
    return x * lax.rsqrt(ms + RMS_EPS) * g


def _out_proj(x, mixed, y_b, wout_ref):
    yb = (y_b * mixed[:, WIDTH:2 * WIDTH]).astype(BF16)
    o = x + jnp.dot(mixed[:, 0:WIDTH].astype(BF16), wout_ref[0:WIDTH, :], preferred_element_type=F32)
    o = o + jnp.dot(yb, wout_ref[WIDTH:2 * WIDTH, :], preferred_element_type=F32)
    o = o + jnp.dot(mixed[:, 2 * WIDTH:4 * WIDTH].astype(BF16), wout_ref[2 * WIDTH:4 * WIDTH, :],
                    preferred_element_type=F32)
    return o


def _pre_body(x_ref, ng_ref, win_ref, caw_ref, ccw_ref, ccb_ref, lng_ref, lnb_ref, wpw_ref, wpool_ref,
              psc_ref, rc_ref, rsa_ref, rsb_ref, prea_ref, prec_ref, pred_ref, *refs,
              ns, tm, last_rows, pos0, prompt):
    if prompt:
        (mixed_ref, k_ref, v_ref, qt_ref, kb_ref, vt_ref, km_ref,
         na_ref, nc_ref, nd_ref, bufa, bufc, bufd, shc, x2, x4, x8) = refs
    else:
        (mixed_ref, k_ref, v_ref, q_ref, na_ref, nc_ref, nd_ref, bufa, bufc, bufd, shc, x2, x4, x8) = refs
    t = pl.program_id(1)
    nt = pl.num_programs(1)
    span = tm + HALO - 8
    n = ns * tm
    seq = lambda z: z.reshape(ns, tm, WIDTH)
    flat = lambda z: z.reshape(n, WIDTH)

    @pl.when(t == 0)
    def _():
        bufa[:, 0:HALO, :] = prea_ref[...]
        bufc[:, 0:HALO, :] = prec_ref[...]
        bufd[:, 0:HALO, :] = pred_ref[...]

    x = x_ref[...].reshape(n, x_ref.shape[-1])
    h = _rmsnorm(x, ng_ref[...]).astype(BF16)

    def proj(c0, cn):
        return jnp.dot(h, win_ref[:, c0 * WIDTH:(c0 + cn) * WIDTH], preferred_element_type=F32)

    def put(col, val):
        mixed_ref[:, :, col * WIDTH:(col + 1) * WIDTH] = seq(val)

    pa = proj(0, 4)
    bufa[:, HALO:HALO + tm, :] = seq(pa[:, 2 * WIDTH:3 * WIDTH] * pa[:, 0:WIDTH])
    conv = None
    for j in range(CONV_A_WIDTH):
        off = HALO - (CONV_A_WIDTH - 1) + j
        term = caw_ref[j:j + 1, :] * bufa[:, off:off + tm, :]
        conv = term if conv is None else conv + term
    put(0, pa[:, WIDTH:2 * WIDTH] * flat(conv) * _silu(pa[:, 3 * WIDTH:4 * WIDTH]))

    pb = proj(4, 4)
    tab = lambda r: flat(jnp.broadcast_to(r[...][None], (ns, tm, WIDTH)))
    rc, rsa, rsb = tab(rc_ref), tab(rsa_ref), tab(rsb_ref)

    def rope(z):
        return z * rc + pltpu.roll(z, WIDTH - ROT_HALF, 1) * rsa + pltpu.roll(z, ROT_HALF, 1) * rsb

    qs = rope(pb[:, 0:WIDTH]) * (HEAD_DIM ** -0.5)
    kr = rope(pb[:, WIDTH:2 * WIDTH])
    v = pb[:, 2 * WIDTH:3 * WIDTH]
    k_ref[...] = seq(kr)
    v_ref[...] = seq(v)
    put(1, _silu(pb[:, 3 * WIDTH:4 * WIDTH]))
    if prompt:
        for r in range(tm // MOBA_BLOCK):
            blk = slice(r * MOBA_BLOCK, (r + 1) * MOBA_BLOCK)
            qt_ref[0, r] = qs[blk, :].T
            kb_ref[0, r] = kr[blk, :].astype(BF16)
            vt_ref[0, r] = v[blk, :].T.astype(BF16)
            km_ref[0, r] = jnp.mean(kr[blk, :], axis=0, keepdims=True)
    else:
        q_ref[...] = seq(qs)

    pc = proj(8, 3)
    bufc[:, HALO:HALO + tm, :] = seq(pc[:, 0:WIDTH] * jax.nn.sigmoid(pc[:, WIDTH:2 * WIDTH]))
    for r in range(1, 8):
        shc[r - 1] = bufc[:, r:r + span, :]
    acc = None
    for j in range(CONV_C_WIDTH):
        a, r = divmod(HALO - (CONV_C_WIDTH - 1) + j, 8)
        rows = bufc[:, 8 * a:8 * a + tm, :] if r == 0 else shc[r - 1, :, 8 * a:8 * a + tm, :]
        term = ccw_ref[j:j + 1, :] * rows
        acc = term if acc is None else acc + term
    acc = flat(acc) + ccb_ref[...]
    mu = jnp.mean(acc, axis=-1, keepdims=True)
    cen = acc - mu
    var = jnp.mean(cen * cen, axis=-1, keepdims=True)
    ln = cen * lax.rsqrt(var + LN_EPS) * lng_ref[...] + lnb_ref[...]
    yc = jnp.dot(_silu(ln).astype(BF16), wpw_ref[...], preferred_element_type=F32)
    put(2, yc * _silu(pc[:, 2 * WIDTH:3 * WIDTH]))

    pd = proj(11, 2)
    dx = pd[:, 0:WIDTH]
    bufd[:, HALO:HALO + tm, :] = seq(dx)
    assert POOL_WINDOWS == (2, 4, 8, 16) and HALO == 32
    x2[...] = bufd[:, 8:8 + span, :] + bufd[:, 7:7 + span, :]
    x4[:, 0:span - 8, :] = x2[:, 8:span, :] + x2[:, 6:span - 2, :]
    x8[:, 0:span - 16, :] = x4[:, 8:span - 8, :] + x4[:, 4:span - 12, :]
    sums = {2: x2[:, 24:24 + tm, :], 4: x4[:, 16:16 + tm, :], 8: x8[:, 8:8 + tm, :],
            16: x8[:, 8:8 + tm, :] + x8[:, 0:tm, :]}
    group = lax.broadcasted_iota(jnp.int32, (ns, tm, WIDTH), 2) // (WIDTH // len(POOL_WINDOWS))
    pos = pos0 + t * tm + lax.broadcasted_iota(jnp.int32, (ns, tm, WIDTH), 1)
    wsum = sums[POOL_WINDOWS[-1]]
    win = jnp.full((ns, tm, WIDTH), POOL_WINDOWS[-1], jnp.int32)
    for g in range(len(POOL_WINDOWS) - 2, -1, -1):
        wsum = jnp.where(group == g, sums[POOL_WINDOWS[g]], wsum)
        win = jnp.where(group == g, POOL_WINDOWS[g], win)
    cnt = jnp.minimum(pos + 1, win).astype(F32)
    diff = flat(wsum / cnt) - dx
    yd = jnp.dot(diff.astype(BF16), wpool_ref[...], preferred_element_type=F32) * psc_ref[...]
    put(3, yd * _silu(pd[:, WIDTH:2 * WIDTH]))

    @pl.when(t == nt - 1)
    def _():
        end = HALO + last_rows
        na_ref[...] = bufa[:, end - (CONV_A_WIDTH - 1):end, :]
        nc_ref[...] = bufc[:, end - (CONV_C_WIDTH - 1):end, :]
        nd_ref[...] = bufd[:, end - POOL_STATE:end, :]

    if tm >= HALO:
        @pl.when(t < nt - 1)
        def _():
            bufa[:, 0:HALO, :] = bufa[:, tm:tm + HALO, :]
            bufc[:, 0:HALO, :] = bufc[:, tm:tm + HALO, :]
            bufd[:, 0:HALO, :] = bufd[:, tm:tm + HALO, :]


def _pre_call(x, lw, layer, rope_tabs, pre_a, pre_c, pre_d, state_layer, *, ns, tm, last_rows, pos0, prompt):
    bsz, t_len, d_model = x.shape
    nt = t_len // tm
    assert nt * tm == t_len and (nt == 1 or tm >= HALO) and bsz % ns == 0 and tm % 8 == 0
    rc, rsa, rsb = rope_tabs
    d_in = lw["w_in"].shape[-1]

    def vec(n):
        return pl.BlockSpec((None, 1, n), lambda b, t: (layer, 0, 0))

    def mat(r, c):
        return pl.BlockSpec((None, r, c), lambda b, t: (layer, 0, 0))

    def per_b(shape):
        return pl.BlockSpec((ns,) + shape, lambda b, t: (b,) + (0,) * len(shape))

    hist = pl.BlockSpec((None, ns, HALO, WIDTH), lambda b, t: (state_layer, b, 0, 0))
    tok = lambda w: pl.BlockSpec((ns, tm, w), lambda b, t: (b, t, 0))
    tab = pl.BlockSpec((tm, WIDTH), lambda b, t: (t, 0))
    in_specs = [tok(d_model), vec(d_model), mat(d_model, d_in),
                mat(CONV_A_WIDTH, WIDTH), mat(CONV_C_WIDTH, WIDTH), vec(WIDTH),
                vec(WIDTH), vec(WIDTH), mat(WIDTH, WIDTH), mat(WIDTH, WIDTH),
                vec(WIDTH), tab, tab, tab, hist, hist, hist]
    out_shape = [jax.ShapeDtypeStruct((bsz, t_len, 4 * WIDTH), F32),
                 jax.ShapeDtypeStruct((bsz, t_len, WIDTH), F32),
                 jax.ShapeDtypeStruct((bsz, t_len, WIDTH), F32)]
    out_specs = [tok(4 * WIDTH), tok(WIDTH), tok(WIDTH)]
    if prompt:
        assert tm % MOBA_BLOCK == 0 and ns == 1
        per_tile = tm // MOBA_BLOCK
        nblk = t_len // MOBA_BLOCK
        blk = lambda r, c: pl.BlockSpec((1, per_tile, r, c), lambda b, t: (b, t, 0, 0))
        out_shape += [jax.ShapeDtypeStruct((bsz, nblk, WIDTH, MOBA_BLOCK), F32),
                      jax.ShapeDtypeStruct((bsz, nblk, MOBA_BLOCK, WIDTH), BF16),
                      jax.ShapeDtypeStruct((bsz, nblk, WIDTH, MOBA_BLOCK), BF16),
                      jax.ShapeDtypeStruct((bsz, nblk, 1, WIDTH), F32)]
        out_specs += [blk(WIDTH, MOBA_BLOCK), blk(MOBA_BLOCK, WIDTH), blk(WIDTH, MOBA_BLOCK), blk(1, WIDTH)]
    else:
        out_shape += [jax.ShapeDtypeStruct((bsz, t_len, WIDTH), F32)]
        out_specs += [tok(WIDTH)]
    out_shape += [jax.ShapeDtypeStruct((bsz, CONV_A_WIDTH - 1, WIDTH), F32),
                  jax.ShapeDtypeStruct((bsz, CONV_C_WIDTH - 1, WIDTH), F32),
                  jax.ShapeDtypeStruct((bsz, POOL_STATE, WIDTH), F32)]
    out_specs += [per_b((CONV_A_WIDTH - 1, WIDTH)), per_b((CONV_C_WIDTH - 1, WIDTH)),
                  per_b((POOL_STATE, WIDTH))]
    body = functools.partial(_pre_body, ns=ns, tm=tm, last_rows=last_rows, pos0=pos0, prompt=prompt)
    return pl.pallas_call(
        body,
        grid=(bsz // ns, nt),
        in_specs=in_specs,
        out_specs=out_specs,
        out_shape=out_shape,
        scratch_shapes=[pltpu.VMEM((ns, HALO + tm, WIDTH), F32)] * 3
        + [pltpu.VMEM((7, ns, HALO + tm - 8, WIDTH), F32)] + [pltpu.VMEM((ns, HALO + tm - 8, WIDTH), F32)] * 3,
        compiler_params=pltpu.CompilerParams(dimension_semantics=("arbitrary", "arbitrary"),
                                             vmem_limit_bytes=VMEM_LIMIT),
        name="pre_prompt" if prompt else "pre_sample",
    )(x, lw["norm_g"], lw["w_in"], lw["conv_a_w"], lw["conv_c_w"], lw["conv_c_b"], lw["ln_c_g"],
      lw["ln_c_b"], lw["w_pw_c"], lw["w_pool"], lw["pool_scale"], rc, rsa, rsb, pre_a, pre_c, pre_d)


def _top_blocks_bias(gate, index, n_index, axis):
    bias = jnp.full(gate.shape, NEG_INF, F32)
    for _ in range(MOBA_TOPK):
        top = jnp.max(gate, axis=axis, keepdims=True)
        cand = jnp.logical_and(gate == top, top > NEG_INF)
        first = jnp.min(jnp.where(cand, index, n_index), axis=axis, keepdims=True)
        pick = index == first
        bias = jnp.where(pick, 0.0, bias)
        gate = jnp.where(pick, NEG_INF, gate)
    return bias


def _attn_body(qt_ref, kb_ref, vt_ref, km_ref, mixed_ref, x_ref, wout_ref, fng_ref, out_ref,
               qm_s, bias_s, acc_s, s_a, s_b, p_a, p_b, al_a, al_b, *, nblk, tq, grp, final):
    s_bufs, p_bufs, al_bufs = (s_a, s_b), (p_a, p_b), (al_a, al_b)
    w4 = N_HEADS * tq
    i = pl.program_id(1)
    qt = qt_ref[0, 0]
    row_head = lax.broadcasted_iota(jnp.int32, (WIDTH, tq), 0) // HEAD_DIM
    km = km_ref[0]
    lane_head = lax.broadcasted_iota(jnp.int32, (nblk, WIDTH), 1) // HEAD_DIM
    blk = lax.broadcasted_iota(jnp.int32, (nblk, tq), 0)
    km_heads = jnp.concatenate([jnp.where(lane_head == h, km, 0.0) for h in range(N_HEADS)], axis=0)
    gates = jnp.dot(km_heads, qt, precision=lax.Precision.HIGHEST, preferred_element_type=F32)
    for h in range(N_HEADS):
        cols = slice(h * tq, (h + 1) * tq)
        qm_s[:, cols] = jnp.where(row_head == h, qt * LOG2_E, 0.0).astype(BF16)
        gate = jnp.where(blk < i, gates[h * nblk:(h + 1) * nblk, :], NEG_INF)
        bias_s[:, cols] = _top_blocks_bias(gate, blk, nblk, 0)

    n_groups = nblk // grp
    sub = [slice(c * MOBA_BLOCK, (c + 1) * MOBA_BLOCK) for c in range(grp)]

    def qk(group, slot):
        kb = kb_ref[0, pl.ds(group * grp, grp)].reshape(grp * MOBA_BLOCK, WIDTH)
        s_bufs[slot][:, 0:w4] = jnp.dot(kb, qm_s[...], preferred_element_type=F32)

    def pv(group, slot):
        for h in range(N_HEADS):
            rows = slice(h * HEAD_DIM, (h + 1) * HEAD_DIM)
            cols = slice(h * tq, (h + 1) * tq)
            vt = jnp.concatenate([vt_ref[0, group * grp + c, rows, :] for c in range(grp)], axis=1)
            o = jnp.dot(vt, p_bufs[slot][:, cols], preferred_element_type=F32)
            acc_s[rows, :] = al_bufs[slot][:, cols] * acc_s[rows, :] + o

    qk(0, 1)
    key = lax.broadcasted_iota(jnp.int32, (MOBA_BLOCK, w4), 0)
    qry = lax.broadcasted_iota(jnp.int32, (MOBA_BLOCK, w4), 1) % tq
    s = jnp.dot(kb_ref[0, i], qm_s[...], preferred_element_type=F32)
    s = jnp.where(key <= qry, s, NEG_INF)
    m = jnp.max(s, axis=0, keepdims=True)
    p = jnp.exp2(s - m)
    l = jnp.sum(p, axis=0, keepdims=True)
    pb = p.astype(BF16)
    for h in range(N_HEADS):
        rows = slice(h * HEAD_DIM, (h + 1) * HEAD_DIM)
        acc_s[rows, :] = jnp.dot(vt_ref[0, i, rows, :], pb[:, h * tq:(h + 1) * tq], preferred_element_type=F32)
    p_bufs[0][:, 0:w4] = jnp.zeros((grp * MOBA_BLOCK, w4), BF16)
    al_bufs[0][...] = jnp.ones((1, w4), F32)

    def half(n, m, l, cur, nxt):
        picked = [bias_s[pl.ds((n - 1) * grp + c, 1), :] for c in range(grp)]
        m_new = m
        for c in range(grp):
            m_new = jnp.maximum(m_new, jnp.max(s_bufs[cur][sub[c], 0:w4], axis=0, keepdims=True) + picked[c])
        alpha = jnp.exp2(m - m_new)
        l = alpha * l
        for c in range(grp):
            p = jnp.exp2(s_bufs[cur][sub[c], 0:w4] - (m_new - picked[c]))
            l = l + jnp.sum(p, axis=0, keepdims=True)
            p_bufs[cur][sub[c], 0:w4] = p.astype(BF16)
        al_bufs[cur][...] = alpha
        pv(jnp.maximum(n - 2, 0), nxt)
        qk(jnp.minimum(n, n_groups - 1), nxt)
        return m_new, l

    def visit(n, carry):
        return lax.cond(n % 2 == 1, lambda: half(n, carry[0], carry[1], 1, 0),
                        lambda: half(n, carry[0], carry[1], 0, 1))

    n_visits = (i + grp - 1) // grp
    m, l = lax.fori_loop(1, n_visits + 1, visit, (m, l))
    odd = n_visits % 2 == 1

    @pl.when(odd)
    def _():
        pv(n_visits - 1, 1)

    @pl.when(jnp.logical_not(odd))
    def _():
        pv(jnp.maximum(n_visits - 1, 0), 0)

    inv = 1.0 / l
    for h in range(N_HEADS):
        rows = slice(h * HEAD_DIM, (h + 1) * HEAD_DIM)
        acc_s[rows, :] = acc_s[rows, :] * inv[:, h * tq:(h + 1) * tq]
    y_b = acc_s[...].T
    o = _out_proj(x_ref[0], mixed_ref[0], y_b, wout_ref)
    if final:
        o = _rmsnorm(o, fng_ref[...])
    out_ref[0] = o


def _attn_call(qt, kb, vt, km, mixed, x, w_out, layer, fng, *, final):
    bsz, nblk, _, tq = qt.shape
    d_model = x.shape[-1]
    assert tq == MOBA_BLOCK
    grp = next(g for g in (ATTN_GROUP, 2, 1) if nblk % g == 0)
    body = functools.partial(_attn_body, nblk=nblk, tq=tq, grp=grp, final=final)
    whole = lambda r, c: pl.BlockSpec((1, nblk, r, c), lambda b, i: (b, 0, 0, 0))
    return pl.pallas_call(
        body,
        grid=(bsz, nblk),
        in_specs=[pl.BlockSpec((1, 1, WIDTH, tq), lambda b, i: (b, i, 0, 0)),
                  whole(MOBA_BLOCK, WIDTH), whole(WIDTH, MOBA_BLOCK),
                  pl.BlockSpec((1, nblk, WIDTH), lambda b, i: (b, 0, 0)),
                  pl.BlockSpec((1, tq, 4 * WIDTH), lambda b, i: (b, i, 0)),
                  pl.BlockSpec((1, tq, d_model), lambda b, i: (b, i, 0)),
                  pl.BlockSpec((None, 4 * WIDTH, d_model), lambda b, i: (layer, 0, 0)),
                  pl.BlockSpec((1, d_model), lambda b, i: (0, 0))],
        out_specs=pl.BlockSpec((1, tq, d_model), lambda b, i: (b, i, 0)),
        out_shape=jax.ShapeDtypeStruct(x.shape, F32),
        scratch_shapes=[pltpu.VMEM((WIDTH, N_HEADS * tq), BF16),
                        pltpu.VMEM((nblk, N_HEADS * tq), F32),
                        pltpu.VMEM((WIDTH, tq), F32)]
        + [pltpu.VMEM((grp * MOBA_BLOCK, N_HEADS * tq + 128), F32)] * 2
        + [pltpu.VMEM((grp * MOBA_BLOCK, N_HEADS * tq + 128), BF16)] * 2
        + [pltpu.VMEM((1, N_HEADS * tq), F32)] * 2,
        compiler_params=pltpu.CompilerParams(dimension_semantics=("arbitrary", "arbitrary"),
                                             vmem_limit_bytes=VMEM_LIMIT),
        name="attn_prompt",
    )(qt, kb, vt, km.reshape(bsz, nblk, WIDTH), mixed, x, w_out, fng)


def _dec_body(pt_ref, *refs, pps, ns, t_valid):
    del pt_ref
    kpages, vpages = refs[0:pps], refs[pps:2 * pps]
    q_ref, kn_ref, vn_ref, yb_ref, qm_s, mp_s, lp_s, gp_s, op_s, pbd_s = refs[2 * pps:]
    step = pl.program_id(1)
    rows = N_HEADS * DEC_ROWS
    nb = pps // 2
    lane_head = lax.broadcasted_iota(jnp.int32, (DEC_ROWS, WIDTH), 1) // HEAD_DIM
    lane = lax.broadcasted_iota(jnp.int32, (rows, 128), 1)

    @pl.when(jnp.logical_and(pl.program_id(0) == 0, step == 0))
    def _():
        pbd_s[...] = jnp.zeros(pbd_s.shape, BF16)

    @pl.when(step == 0)
    def _():
        q8 = q_ref[0]
        for h in range(N_HEADS):
            qm_s[h * DEC_ROWS:(h + 1) * DEC_ROWS, :] = jnp.where(lane_head == h, q8, 0.0)
        mp_s[...] = jnp.full((rows, 128), NEG_INF, F32)
        gp_s[...] = jnp.full((rows, 128), NEG_INF, F32)
        lp_s[...] = jnp.zeros((rows, 128), F32)

    qmb = qm_s[...].astype(BF16)
    contract_last = (((1,), (1,)), ((), ()))
    kt = jnp.concatenate([kpages[c][...] for c in range(pps)], axis=1).astype(BF16)
    vt = jnp.concatenate([vpages[c][...] for c in range(pps)], axis=1).astype(BF16)
    s_all = jnp.dot(qmb, kt, preferred_element_type=F32)
    for c in range(nb):
        s = s_all[:, c * MOBA_BLOCK:(c + 1) * MOBA_BLOCK]
        gate = jnp.sum(s, axis=1, keepdims=True) * (1.0 / MOBA_BLOCK)
        m = jnp.max(s, axis=1, keepdims=True)
        p = jnp.exp(s - m)
        l = jnp.sum(p, axis=1, keepdims=True)
        j = step * nb + c
        mp_s[...] = jnp.where(lane == j, m, mp_s[...])
        lp_s[...] = jnp.where(lane == j, l, lp_s[...])
        gp_s[...] = jnp.where(lane == j, gate, gp_s[...])
        pbd_s[c * rows:(c + 1) * rows, c * MOBA_BLOCK:(c + 1) * MOBA_BLOCK] = p.astype(BF16)
    op_s[pl.ds(pl.multiple_of(step * (nb * rows), nb * rows), nb * rows), :] = lax.dot_general(
        pbd_s[...], vt, contract_last, preferred_element_type=F32)

    @pl.when(step == ns - 1)
    def _():
        nblk = ns * nb
        sel = _top_blocks_bias(gp_s[...], lane, 128, 1) == 0.0
        zpad = jnp.zeros((128 - DEC_ROWS, WIDTH), F32)
        kn = jnp.concatenate([kn_ref[0], zpad], axis=0).astype(BF16)
        vn = jnp.concatenate([vn_ref[0], zpad], axis=0).astype(BF16)
        s = lax.dot_general(qmb, kn, contract_last, preferred_element_type=F32)
        qpos = jnp.minimum(lax.broadcasted_iota(jnp.int32, (rows, 128), 0) % DEC_ROWS, t_valid - 1)
        s = jnp.where(lane <= qpos, s, NEG_INF)
        m_own = jnp.max(s, axis=1, keepdims=True)
        p = jnp.exp(s - m_own)
        l_own = jnp.sum(p, axis=1, keepdims=True)
        o_own = jnp.dot(p.astype(BF16), vn, preferred_element_type=F32)
        mp = mp_s[...]
        m_fin = jnp.maximum(m_own, jnp.max(jnp.where(sel, mp, NEG_INF), axis=1, keepdims=True))
        w = jnp.where(sel, jnp.exp(mp - m_fin), 0.0)
        a_own = jnp.exp(m_own - m_fin)
        l_fin = a_own * l_own + jnp.sum(w * lp_s[...], axis=1, keepdims=True)
        o_fin = a_own * o_own
        for j in range(nblk):
            o_fin = o_fin + w[:, j:j + 1] * op_s[j * rows:(j + 1) * rows, :]
        y = o_fin / l_fin
        y_b = None
        for h in range(N_HEADS):
            part = jnp.where(lane_head == h, y[h * DEC_ROWS:(h + 1) * DEC_ROWS, :], 0.0)
            y_b = part if y_b is None else y_b + part
        yb_ref[0] = y_b


def _dec_call(page_table, cache_k, cache_v, layer, q, kn, vn, *, t_valid):
    bsz, n_pages = page_table.shape
    depth, n_pool, page, _, _ = cache_k.shape
    assert (n_pages * page) % MOBA_BLOCK == 0 and MOBA_BLOCK == 2 * page
    nblk = n_pages * page // MOBA_BLOCK
    assert MOBA_TOPK <= nblk <= 128
    pps = next(c for c in (16, 8, 4, 2) if n_pages % c == 0)
    ns = n_pages // pps
    ck = cache_k.transpose(0, 1, 3, 4, 2).reshape(depth, n_pool, WIDTH, page)
    cv = cache_v.transpose(0, 1, 3, 4, 2).reshape(depth, n_pool, WIDTH, page)

    def page_spec(c):
        return pl.BlockSpec((None, None, WIDTH, page), lambda b, s, pt: (layer, pt[b, s * pps + c], 0, 0))

    per_b = pl.BlockSpec((1, DEC_ROWS, WIDTH), lambda b, s, pt: (b, 0, 0))
    rows = N_HEADS * DEC_ROWS
    body = functools.partial(_dec_body, pps=pps, ns=ns, t_valid=t_valid)
    grid_spec = pltpu.PrefetchScalarGridSpec(
        num_scalar_prefetch=1,
        grid=(bsz, ns),
        in_specs=[page_spec(c) for c in range(pps)] * 2 + [per_b, per_b, per_b],
        out_specs=per_b,
        scratch_shapes=[pltpu.VMEM((rows, WIDTH), F32), pltpu.VMEM((rows, 128), F32),
                        pltpu.VMEM((rows, 128), F32), pltpu.VMEM((rows, 128), F32),
                        pltpu.VMEM((nblk * rows, WIDTH), F32),
                        pltpu.VMEM((pps // 2 * rows, pps * page), BF16)],
    )
    return pl.pallas_call(
        body,
        grid_spec=grid_spec,
        out_shape=jax.ShapeDtypeStruct((bsz, DEC_ROWS, WIDTH), F32),
        compiler_params=pltpu.CompilerParams(dimension_semantics=("arbitrary", "arbitrary"),
                                             vmem_limit_bytes=VMEM_LIMIT),
        name="attn_sample",
    )(page_table, *([ck] * pps), *([cv] * pps), q, kn, vn)


def _proj_body(yb_ref, mixed_ref, x_ref, wout_ref, fng_ref, out_ref, *, final):
    o = _out_proj(x_ref[...], mixed_ref[...], yb_ref[...], wout_ref)
    if final:
        o = _rmsnorm(o, fng_ref[...])
    out_ref[...] = o


def _proj_call(y_b, mixed, x, w_out, layer, fng, *, final):
    bsz, t_len, d_model = x.shape
    n = bsz * t_len
    whole = lambda r, c: pl.BlockSpec((r, c), lambda i: (0, 0))
    out = pl.pallas_call(
        functools.partial(_proj_body, final=final),
        grid=(1,),
        in_specs=[whole(n, WIDTH), whole(n, 4 * WIDTH), whole(n, d_model),
                  pl.BlockSpec((None, 4 * WIDTH, d_model), lambda i: (layer, 0, 0)),
                  whole(1, d_model)],
        out_specs=whole(n, d_model),
        out_shape=jax.ShapeDtypeStruct((n, d_model), F32),
        compiler_params=pltpu.CompilerParams(dimension_semantics=("arbitrary",), vmem_limit_bytes=VMEM_LIMIT),
        name="proj_sample",
    )(y_b.reshape(n, WIDTH), mixed.reshape(n, 4 * WIDTH), x.reshape(n, d_model), w_out, fng)
    return out.reshape(bsz, t_len, d_model)


def _rope_tables(pos0, n):
    freqs = ROPE_THETA ** (-jnp.arange(ROT_HALF, dtype=F32) * 2.0 / (2 * ROT_HALF))
    ang = (pos0 + jnp.arange(n, dtype=jnp.int32)).astype(F32)[:, None] * freqs[None, :]
    cos, sin = jnp.cos(ang), jnp.sin(ang)
    one = jnp.ones((n, HEAD_DIM - 2 * ROT_HALF), F32)
    zero8 = jnp.zeros((n, ROT_HALF), F32)
    zero_rest = jnp.zeros((n, HEAD_DIM - 2 * ROT_HALF), F32)
    rc = jnp.concatenate([cos, cos, one], axis=1)
    rsa = jnp.concatenate([-sin, zero8, zero_rest], axis=1)
    rsb = jnp.concatenate([zero8, sin, zero_rest], axis=1)
    return tuple(jnp.tile(a, (1, N_HEADS)) for a in (rc, rsa, rsb))


def _pad_state(state):
    return jnp.pad(state, ((0, 0), (0, 0), (HALO - state.shape[2], 0), (0, 0)))


def kernel(x_prompt, x_sample, cache_k, cache_v, page_table, state_conv_a, state_conv_c, state_pool_d,
           norm_g, w_in, conv_a_w, conv_c_w, conv_c_b, ln_c_g, ln_c_b, w_pw_c, w_pool, pool_scale, w_out,
           final_norm_g):
    depth = w_in.shape[0]
    bsz, seq, d_model = x_prompt.shape
    dec_b, dec_t, _ = x_sample.shape
    past_len = page_table.shape[1] * cache_k.shape[2]
    assert seq % MOBA_BLOCK == 0 and dec_t <= DEC_ROWS and WIDTH == N_HEADS * HEAD_DIM

    pre_tile = next(t for t in (PRE_TILE, MOBA_BLOCK) if seq % t == 0)
    tabs_p = _rope_tables(0, seq)
    tabs_s = _rope_tables(past_len, DEC_ROWS)
    zero_state = jnp.zeros((1, bsz, HALO, WIDTH), F32)
    hist_a, hist_c, hist_d = _pad_state(state_conv_a), _pad_state(state_conv_c), _pad_state(state_pool_d)
    fng = final_norm_g.reshape(1, d_model)
    yp = x_prompt
    ys = jnp.pad(x_sample, ((0, 0), (0, DEC_ROWS - dec_t), (0, 0)))
    outs = [[] for _ in range(10)]
    n_groups = w_pool.shape[1]
    eye = jnp.eye(n_groups, dtype=F32)
    pool_bd = (w_pool[:, :, :, None, :] * eye[None, :, None, :, None]).reshape(depth, WIDTH, WIDTH)
    row = lambda a: a.reshape(depth, 1, a.shape[-1])
    lw = {"norm_g": row(norm_g), "w_in": w_in.astype(BF16), "conv_a_w": conv_a_w, "conv_c_w": conv_c_w,
          "conv_c_b": row(conv_c_b), "ln_c_g": row(ln_c_g), "ln_c_b": row(ln_c_b),
          "w_pw_c": w_pw_c.astype(BF16), "w_pool": pool_bd.astype(BF16), "pool_scale": row(pool_scale)}
    wo = w_out.astype(BF16)
    for l in range(depth):
        final = l == depth - 1

        mixed, kp, vp, qt, kb, vt, km, ap, cp, dp = _pre_call(
            yp, lw, l, tabs_p, zero_state, zero_state, zero_state, 0,
            ns=1, tm=pre_tile, last_rows=pre_tile, pos0=0, prompt=True)
        yp = _attn_call(qt, kb, vt, km, mixed, yp, wo, l, fng, final=final)

        mixed_s, ks, vs, qs, as_, cs, ds = _pre_call(
            ys, lw, l, tabs_s, hist_a, hist_c, hist_d, l,
            ns=dec_b, tm=DEC_ROWS, last_rows=dec_t, pos0=past_len, prompt=False)
        yb_s = _dec_call(page_table, cache_k, cache_v, l, qs, ks, vs, t_valid=dec_t)
        ys = _proj_call(yb_s, mixed_s, ys, wo, l, fng, final=final)

        for lst, val in zip(outs, (kp.reshape(bsz, seq, N_HEADS, HEAD_DIM),
                                   vp.reshape(bsz, seq, N_HEADS, HEAD_DIM),
                                   ks[:, :dec_t].reshape(dec_b, dec_t, N_HEADS, HEAD_DIM),
                                   vs[:, :dec_t].reshape(dec_b, dec_t, N_HEADS, HEAD_DIM),
                                   ap, as_, cp, cs, dp, ds)):
            lst.append(val)
    return (yp, ys[:, :dec_t]) + tuple(jnp.stack(o) for o in outs)
```

```python
import functools

import jax
import jax.numpy as jnp
from jax import lax
from jax.experimental import pallas as pl
from jax.experimental.pallas import tpu as pltpu

F32 = jnp.float32
BF16 = jnp.bfloat16

WIDTH = 256
N_HEADS = 4
HEAD_DIM = 64
ROT_HALF = 8
ROPE_THETA = 500000.0
MOBA_BLOCK = 256
MOBA_TOPK = 3
CONV_A_WIDTH = 3
CONV_C_WIDTH = 31
POOL_WINDOWS = (2, 4, 8, 16)
POOL_STATE = 15
RMS_EPS = 1e-6
LN_EPS = 1e-5
HALO = 32
DEC_ROWS = 8
ATTN_GROUP = 4
DEC_SLOTS = 3
PRE_TILE = 1024
NEG_INF = float("-inf")
LOG2_E = 1.4426950408889634
VMEM_LIMIT = 56 * 1024 * 1024


def _silu(z):
    return z * jax.nn.sigmoid(z)


def _rmsnorm(x, g):
    ms = jnp.mean(x * x, axis=-1, keepdims=True)
    return x * lax.rsqrt(ms + RMS_EPS) * g


def _out_proj(x, mixed, y_b, wout_ref):
    yb = (y_b * mixed[:, WIDTH:2 * WIDTH]).astype(BF16)
    o = x + jnp.dot(mixed[:, 0:WIDTH].astype(BF16), wout_ref[0:WIDTH, :], preferred_element_type=F32)
    o = o + jnp.dot(yb, wout_ref[WIDTH:2 * WIDTH, :], preferred_element_type=F32)
    o = o + jnp.dot(mixed[:, 2 * WIDTH:4 * WIDTH].astype(BF16), wout_ref[2 * WIDTH:4 * WIDTH, :],
                    preferred_element_type=F32)
    return o


def _pre_body(x_ref, ng_ref, win_ref, caw_ref, ccw_ref, ccb_ref, lng_ref, lnb_ref, wpw_ref, wpool_ref,
              psc_ref, rc_ref, rsa_ref, rsb_ref, prea_ref, prec_ref, pred_ref, *refs,
              ns, tm, last_rows, pos0, prompt):
    if prompt:
        (mixed_ref, k_ref, v_ref, qt_ref, kb_ref, vt_ref, km_ref,
         na_ref, nc_ref, nd_ref, bufa, bufc, bufd, shc, x2, x4, x8) = refs
    else:
        (mixed_ref, k_ref, v_ref, q_ref, na_ref, nc_ref, nd_ref, bufa, bufc, bufd, shc, x2, x4, x8) = refs
    t = pl.program_id(1)
    nt = pl.num_programs(1)
    span = tm + HALO - 8
    n = ns * tm
    seq = lambda z: z.reshape(ns, tm, WIDTH)
    flat = lambda z: z.reshape(n, WIDTH)

    @pl.when(t == 0)
    def _():
        bufa[:, 0:HALO, :] = prea_ref[...]
        bufc[:, 0:HALO, :] = prec_ref[...]
        bufd[:, 0:HALO, :] = pred_ref[...]

    x = x_ref[...].reshape(n, x_ref.shape[-1])
    h = _rmsnorm(x, ng_ref[...]).astype(BF16)

    def proj(c0, cn):
        return jnp.dot(h, win_ref[:, c0 * WIDTH:(c0 + cn) * WIDTH], preferred_element_type=F32)

    def put(col, val):
        mixed_ref[:, :, col * WIDTH:(col + 1) * WIDTH] = seq(val)

    pa = proj(0, 4)
    bufa[:, HALO:HALO + tm, :] = seq(pa[:, 2 * WIDTH:3 * WIDTH] * pa[:, 0:WIDTH])
    conv = None
    for j in range(CONV_A_WIDTH):
        off = HALO - (CONV_A_WIDTH - 1) + j
        term = caw_ref[j:j + 1, :] * bufa[:, off:off + tm, :]
        conv = term if conv is None else conv + term
    put(0, pa[:, WIDTH:2 * WIDTH] * flat(conv) * _silu(pa[:, 3 * WIDTH:4 * WIDTH]))

    pb = proj(4, 4)
    tab = lambda r: flat(jnp.broadcast_to(r[...][None], (ns, tm, WIDTH)))
    rc, rsa, rsb = tab(rc_ref), tab(rsa_ref), tab(rsb_ref)

    def rope(z):
        return z * rc + pltpu.roll(z, WIDTH - ROT_HALF, 1) * rsa + pltpu.roll(z, ROT_HALF, 1) * rsb

    qs = rope(pb[:, 0:WIDTH]) * (HEAD_DIM ** -0.5)
    kr = rope(pb[:, WIDTH:2 * WIDTH])
    v = pb[:, 2 * WIDTH:3 * WIDTH]
    k_ref[...] = seq(kr)
    v_ref[...] = seq(v)
    put(1, _silu(pb[:, 3 * WIDTH:4 * WIDTH]))
    if prompt:
        for r in range(tm // MOBA_BLOCK):
            blk = slice(r * MOBA_BLOCK, (r + 1) * MOBA_BLOCK)
            qt_ref[0, r] = qs[blk, :].T
            kb_ref[0, r] = kr[blk, :].astype(BF16)
            vt_ref[0, r] = v[blk, :].T.astype(BF16)
            km_ref[0, r] = jnp.mean(kr[blk, :], axis=0, keepdims=True)
    else:
        q_ref[...] = seq(qs)

    pc = proj(8, 3)
    bufc[:, HALO:HALO + tm, :] = seq(pc[:, 0:WIDTH] * jax.nn.sigmoid(pc[:, WIDTH:2 * WIDTH]))
    for r in range(1, 8):
        shc[r - 1] = bufc[:, r:r + span, :]
    acc = None
    for j in range(CONV_C_WIDTH):
        a, r = divmod(HALO - (CONV_C_WIDTH - 1) + j, 8)
        rows = bufc[:, 8 * a:8 * a + tm, :] if r == 0 else shc[r - 1, :, 8 * a:8 * a + tm, :]
        term = ccw_ref[j:j + 1, :] * rows
        acc = term if acc is None else acc + term
    acc = flat(acc) + ccb_ref[...]
    mu = jnp.mean(acc, axis=-1, keepdims=True)
    cen = acc - mu
    var = jnp.mean(cen * cen, axis=-1, keepdims=True)
    ln = cen * lax.rsqrt(var + LN_EPS) * lng_ref[...] + lnb_ref[...]
    yc = jnp.dot(_silu(ln).astype(BF16), wpw_ref[...], preferred_element_type=F32)
    put(2, yc * _silu(pc[:, 2 * WIDTH:3 * WIDTH]))

    pd = proj(11, 2)
    dx = pd[:, 0:WIDTH]
    bufd[:, HALO:HALO + tm, :] = seq(dx)
    assert POOL_WINDOWS == (2, 4, 8, 16) and HALO == 32
    x2[...] = bufd[:, 8:8 + span, :] + bufd[:, 7:7 + span, :]
    x4[:, 0:span - 8, :] = x2[:, 8:span, :] + x2[:, 6:span - 2, :]
    x8[:, 0:span - 16, :] = x4[:, 8:span - 8, :] + x4[:, 4:span - 12, :]
    sums = {2: x2[:, 24:24 + tm, :], 4: x4[:, 16:16 + tm, :], 8: x8[:, 8:8 + tm, :],
            16: x8[:, 8:8 + tm, :] + x8[:, 0:tm, :]}
    group = lax.broadcasted_iota(jnp.int32, (ns, tm, WIDTH), 2) // (WIDTH // len(POOL_WINDOWS))
    pos = pos0 + t * tm + lax.broadcasted_iota(jnp.int32, (ns, tm, WIDTH), 1)
    wsum = sums[POOL_WINDOWS[-1]]
    win = jnp.full((ns, tm, WIDTH), POOL_WINDOWS[-1], jnp.int32)
    for g in range(len(POOL_WINDOWS) - 2, -1, -1):
        wsum = jnp.where(group == g, sums[POOL_WINDOWS[g]], wsum)
        win = jnp.where(group == g, POOL_WINDOWS[g], win)
    cnt = jnp.minimum(pos + 1, win).astype(F32)
    diff = flat(wsum / cnt) - dx
    yd = jnp.dot(diff.astype(BF16), wpool_ref[...], preferred_element_type=F32) * psc_ref[...]
    put(3, yd * _silu(pd[:, WIDTH:2 * WIDTH]))

    @pl.when(t == nt - 1)
    def _():
        end = HALO + last_rows
        na_ref[...] = bufa[:, end - (CONV_A_WIDTH - 1):end, :]
        nc_ref[...] = bufc[:, end - (CONV_C_WIDTH - 1):end, :]
        nd_ref[...] = bufd[:, end - POOL_STATE:end, :]

    if tm >= HALO:
        @pl.when(t < nt - 1)
        def _():
            bufa[:, 0:HALO, :] = bufa[:, tm:tm + HALO, :]
            bufc[:, 0:HALO, :] = bufc[:, tm:tm + HALO, :]
            bufd[:, 0:HALO, :] = bufd[:, tm:tm + HALO, :]


def _pre_call(x, lw, layer, rope_tabs, pre_a, pre_c, pre_d, state_layer, *, ns, tm, last_rows, pos0, prompt):
    bsz, t_len, d_model = x.shape
    nt = t_len // tm
    assert nt * tm == t_len and (nt == 1 or tm >= HALO) and bsz % ns == 0 and tm % 8 == 0
    rc, rsa, rsb = rope_tabs
    d_in = lw["w_in"].shape[-1]

    def vec(n):
        return pl.BlockSpec((None, 1, n), lambda b, t: (layer, 0, 0))

    def mat(r, c):
        return pl.BlockSpec((None, r, c), lambda b, t: (layer, 0, 0))

    def per_b(shape):
        return pl.BlockSpec((ns,) + shape, lambda b, t: (b,) + (0,) * len(shape))

    hist = pl.BlockSpec((None, ns, HALO, WIDTH), lambda b, t: (state_layer, b, 0, 0))
    tok = lambda w: pl.BlockSpec((ns, tm, w), lambda b, t: (b, t, 0))
    tab = pl.BlockSpec((tm, WIDTH), lambda b, t: (t, 0))
    in_specs = [tok(d_model), vec(d_model), mat(d_model, d_in),
                mat(CONV_A_WIDTH, WIDTH), mat(CONV_C_WIDTH, WIDTH), vec(WIDTH),
                vec(WIDTH), vec(WIDTH), mat(WIDTH, WIDTH), mat(WIDTH, WIDTH),
                vec(WIDTH), tab, tab, tab, hist, hist, hist]
    out_shape = [jax.ShapeDtypeStruct((bsz, t_len, 4 * WIDTH), F32),
                 jax.ShapeDtypeStruct((bsz, t_len, WIDTH), F32),
                 jax.ShapeDtypeStruct((bsz, t_len, WIDTH), F32)]
    out_specs = [tok(4 * WIDTH), tok(WIDTH), tok(WIDTH)]
    if prompt:
        assert tm % MOBA_BLOCK == 0 and ns == 1
        per_tile = tm // MOBA_BLOCK
        nblk = t_len // MOBA_BLOCK
        blk = lambda r, c: pl.BlockSpec((1, per_tile, r, c), lambda b, t: (b, t, 0, 0))
        out_shape += [jax.ShapeDtypeStruct((bsz, nblk, WIDTH, MOBA_BLOCK), F32),
                      jax.ShapeDtypeStruct((bsz, nblk, MOBA_BLOCK, WIDTH), BF16),
                      jax.ShapeDtypeStruct((bsz, nblk, WIDTH, MOBA_BLOCK), BF16),
                      jax.ShapeDtypeStruct((bsz, nblk, 1, WIDTH), F32)]
        out_specs += [blk(WIDTH, MOBA_BLOCK), blk(MOBA_BLOCK, WIDTH), blk(WIDTH, MOBA_BLOCK), blk(1, WIDTH)]
    else:
        out_shape += [jax.ShapeDtypeStruct((bsz, t_len, WIDTH), F32)]
        out_specs += [tok(WIDTH)]
    out_shape += [jax.ShapeDtypeStruct((bsz, CONV_A_WIDTH - 1, WIDTH), F32),
                  jax.ShapeDtypeStruct((bsz, CONV_C_WIDTH - 1, WIDTH), F32),
                  jax.ShapeDtypeStruct((bsz, POOL_STATE, WIDTH), F32)]
    out_specs += [per_b((CONV_A_WIDTH - 1, WIDTH)), per_b((CONV_C_WIDTH - 1, WIDTH)),
                  per_b((POOL_STATE, WIDTH))]
    body = functools.partial(_pre_body, ns=ns, tm=tm, last_rows=last_rows, pos0=pos0, prompt=prompt)
    return pl.pallas_call(
        body,
        grid=(bsz // ns, nt),
        in_specs=in_specs,
        out_specs=out_specs,
        out_shape=out_shape,
        scratch_shapes=[pltpu.VMEM((ns, HALO + tm, WIDTH), F32)] * 3
        + [pltpu.VMEM((7, ns, HALO + tm - 8, WIDTH), F32)] + [pltpu.VMEM((ns, HALO + tm - 8, WIDTH), F32)] * 3,
        compiler_params=pltpu.CompilerParams(dimension_semantics=("arbitrary", "arbitrary"),
                                             vmem_limit_bytes=VMEM_LIMIT),
        name="pre_prompt" if prompt else "pre_sample",
    )(x, lw["norm_g"], lw["w_in"], lw["conv_a_w"], lw["conv_c_w"], lw["conv_c_b"], lw["ln_c_g"],
      lw["ln_c_b"], lw["w_pw_c"], lw["w_pool"], lw["pool_scale"], rc, rsa, rsb, pre_a, pre_c, pre_d)


def _top_blocks_bias(gate, index, n_index, axis):
    bias = jnp.full(gate.shape, NEG_INF, F32)
    for _ in range(MOBA_TOPK):
        top = jnp.max(gate, axis=axis, keepdims=True)
        cand = jnp.logical_and(gate == top, top > NEG_INF)
        first = jnp.min(jnp.where(cand, index, n_index), axis=axis, keepdims=True)
        pick = index == first
        bias = jnp.where(pick, 0.0, bias)
        gate = jnp.where(pick, NEG_INF, gate)
    return bias


def _attn_body(qt_ref, kb_ref, vt_ref, km_ref, mixed_ref, x_ref, wout_ref, fng_ref, out_ref,
               qm_s, bias_s, acc_s, s_a, s_b, p_a, p_b, al_a, al_b, *, nblk, tq, grp, final):
    s_bufs, p_bufs, al_bufs = (s_a, s_b), (p_a, p_b), (al_a, al_b)
    w4 = N_HEADS * tq
    i = pl.program_id(1)
    qt = qt_ref[0, 0]
    row_head = lax.broadcasted_iota(jnp.int32, (WIDTH, tq), 0) // HEAD_DIM
    km = km_ref[0]
    lane_head = lax.broadcasted_iota(jnp.int32, (nblk, WIDTH), 1) // HEAD_DIM
    blk = lax.broadcasted_iota(jnp.int32, (nblk, tq), 0)
    km_heads = jnp.concatenate([jnp.where(lane_head == h, km, 0.0) for h in range(N_HEADS)], axis=0)
    gates = jnp.dot(km_heads, qt, precision=lax.Precision.HIGHEST, preferred_element_type=F32)
    for h in range(N_HEADS):
        cols = slice(h * tq, (h + 1) * tq)
        qm_s[:, cols] = jnp.where(row_head == h, qt * LOG2_E, 0.0).astype(BF16)
        gate = jnp.where(blk < i, gates[h * nblk:(h + 1) * nblk, :], NEG_INF)
        bias_s[:, cols] = _top_blocks_bias(gate, blk, nblk, 0)

    n_groups = nblk // grp
    sub = [slice(c * MOBA_BLOCK, (c + 1) * MOBA_BLOCK) for c in range(grp)]

    def qk(group, slot):
        kb = kb_ref[0, pl.ds(group * grp, grp)].reshape(grp * MOBA_BLOCK, WIDTH)
        s_bufs[slot][:, 0:w4] = jnp.dot(kb, qm_s[...], preferred_element_type=F32)

    def pv(group, slot):
        for h in range(N_HEADS):
            rows = slice(h * HEAD_DIM, (h + 1) * HEAD_DIM)
            cols = slice(h * tq, (h + 1) * tq)
            vt = jnp.concatenate([vt_ref[0, group * grp + c, rows, :] for c in range(grp)], axis=1)
            o = jnp.dot(vt, p_bufs[slot][:, cols], preferred_element_type=F32)
            acc_s[rows, :] = al_bufs[slot][:, cols] * acc_s[rows, :] + o

    qk(0, 1)
    key = lax.broadcasted_iota(jnp.int32, (MOBA_BLOCK, w4), 0)
    qry = lax.broadcasted_iota(jnp.int32, (MOBA_BLOCK, w4), 1) % tq
    s = jnp.dot(kb_ref[0, i], qm_s[...], preferred_element_type=F32)
    s = jnp.where(key <= qry, s, NEG_INF)
    m = jnp.max(s, axis=0, keepdims=True)
    p = jnp.exp2(s - m)
    l = jnp.sum(p, axis=0, keepdims=True)
    pb = p.astype(BF16)
    for h in range(N_HEADS):
        rows = slice(h * HEAD_DIM, (h + 1) * HEAD_DIM)
        acc_s[rows, :] = jnp.dot(vt_ref[0, i, rows, :], pb[:, h * tq:(h + 1) * tq], preferred_element_type=F32)
    p_bufs[0][:, 0:w4] = jnp.zeros((grp * MOBA_BLOCK, w4), BF16)
    al_bufs[0][...] = jnp.ones((1, w4), F32)

    def half(n, m, l, cur, nxt):
        picked = [bias_s[pl.ds((n - 1) * grp + c, 1), :] for c in range(grp)]
        m_new = m
        for c in range(grp):
            m_new = jnp.maximum(m_new, jnp.max(s_bufs[cur][sub[c], 0:w4], axis=0, keepdims=True) + picked[c])
        alpha = jnp.exp2(m - m_new)
        l = alpha * l
        for c in range(grp):
            p = jnp.exp2(s_bufs[cur][sub[c], 0:w4] - (m_new - picked[c]))
            l = l + jnp.sum(p, axis=0, keepdims=True)
            p_bufs[cur][sub[c], 0:w4] = p.astype(BF16)
        al_bufs[cur][...] = alpha
        pv(jnp.maximum(n - 2, 0), nxt)
        qk(jnp.minimum(n, n_groups - 1), nxt)
        return m_new, l

    def visit(n, carry):
        return lax.cond(n % 2 == 1, lambda: half(n, carry[0], carry[1], 1, 0),
                        lambda: half(n, carry[0], carry[1], 0, 1))

    n_visits = (i + grp - 1) // grp
    m, l = lax.fori_loop(1, n_visits + 1, visit, (m, l))
    odd = n_visits % 2 == 1

    @pl.when(odd)
    def _():
        pv(n_visits - 1, 1)

    @pl.when(jnp.logical_not(odd))
    def _():
        pv(jnp.maximum(n_visits - 1, 0), 0)

    inv = 1.0 / l
    for h in range(N_HEADS):
        rows = slice(h * HEAD_DIM, (h + 1) * HEAD_DIM)
        acc_s[rows, :] = acc_s[rows, :] * inv[:, h * tq:(h + 1) * tq]
    y_b = acc_s[...].T
    o = _out_proj(x_ref[0], mixed_ref[0], y_b, wout_ref)
    if final:
        o = _rmsnorm(o, fng_ref[...])
    out_ref[0] = o


def _attn_call(qt, kb, vt, km, mixed, x, w_out, layer, fng, *, final):
    bsz, nblk, _, tq = qt.shape
    d_model = x.shape[-1]
    assert tq == MOBA_BLOCK
    grp = next(g for g in (ATTN_GROUP, 2, 1) if nblk % g == 0)
    body = functools.partial(_attn_body, nblk=nblk, tq=tq, grp=grp, final=final)
    whole = lambda r, c: pl.BlockSpec((1, nblk, r, c), lambda b, i: (b, 0, 0, 0))
    return pl.pallas_call(
        body,
        grid=(bsz, nblk),
        in_specs=[pl.BlockSpec((1, 1, WIDTH, tq), lambda b, i: (b, i, 0, 0)),
                  whole(MOBA_BLOCK, WIDTH), whole(WIDTH, MOBA_BLOCK),
                  pl.BlockSpec((1, nblk, WIDTH), lambda b, i: (b, 0, 0)),
                  pl.BlockSpec((1, tq, 4 * WIDTH), lambda b, i: (b, i, 0)),
                  pl.BlockSpec((1, tq, d_model), lambda b, i: (b, i, 0)),
                  pl.BlockSpec((None, 4 * WIDTH, d_model), lambda b, i: (layer, 0, 0)),
                  pl.BlockSpec((1, d_model), lambda b, i: (0, 0))],
        out_specs=pl.BlockSpec((1, tq, d_model), lambda b, i: (b, i, 0)),
        out_shape=jax.ShapeDtypeStruct(x.shape, F32),
        scratch_shapes=[pltpu.VMEM((WIDTH, N_HEADS * tq), BF16),
                        pltpu.VMEM((nblk, N_HEADS * tq), F32),
                        pltpu.VMEM((WIDTH, tq), F32)]
        + [pltpu.VMEM((grp * MOBA_BLOCK, N_HEADS * tq + 128), F32)] * 2
        + [pltpu.VMEM((grp * MOBA_BLOCK, N_HEADS * tq + 128), BF16)] * 2
        + [pltpu.VMEM((1, N_HEADS * tq), F32)] * 2,
        compiler_params=pltpu.CompilerParams(dimension_semantics=("arbitrary", "arbitrary"),
                                             vmem_limit_bytes=VMEM_LIMIT),
        name="attn_prompt",
    )(qt, kb, vt, km.reshape(bsz, nblk, WIDTH), mixed, x, w_out, fng)


def _dec_body(pt_ref, *refs, layer, n_seq, pps, ns, t_valid):
    (ck_ref, cv_ref, q_ref, kn_ref, vn_ref, yb_ref,
     qm_s, mp_s, lp_s, gp_s, op_s, pbd_s, kbuf, vbuf, sems) = refs
    step = pl.program_id(1)
    rows = N_HEADS * DEC_ROWS
    nb = pps // 2
    lane_head = lax.broadcasted_iota(jnp.int32, (DEC_ROWS, WIDTH), 1) // HEAD_DIM
    lane = lax.broadcasted_iota(jnp.int32, (rows, 128), 1)

    g = pl.program_id(0) * ns + step
    n_steps = pl.num_programs(0) * ns

    def page_copies(gg, page_of):
        slot = gg % DEC_SLOTS
        for c in range(pps):
            page = page_of(c)
            yield pltpu.make_async_copy(ck_ref.at[layer, page], kbuf.at[slot, c], sems.at[0, slot])
            yield pltpu.make_async_copy(cv_ref.at[layer, page], vbuf.at[slot, c], sems.at[1, slot])

    def request(gg):
        seq_id, seq_step = gg // ns, gg % ns
        for copy in page_copies(gg, lambda c: pt_ref[seq_id, seq_step * pps + c]):
            copy.start()

    @pl.when(g == 0)
    def _():
        pbd_s[...] = jnp.zeros(pbd_s.shape, BF16)
        for ahead in range(min(DEC_SLOTS - 1, n_seq * ns)):
            request(ahead)

    @pl.when(g + (DEC_SLOTS - 1) < n_steps)
    def _():
        request(g + (DEC_SLOTS - 1))

    for copy in page_copies(g, lambda c: 0):
        copy.wait()
    slot = g % DEC_SLOTS

    @pl.when(step == 0)
    def _():
        q8 = q_ref[0]
        for h in range(N_HEADS):
            qm_s[h * DEC_ROWS:(h + 1) * DEC_ROWS, :] = jnp.where(lane_head == h, q8, 0.0)
        mp_s[...] = jnp.full((rows, 128), NEG_INF, F32)
        gp_s[...] = jnp.full((rows, 128), NEG_INF, F32)
        lp_s[...] = jnp.zeros((rows, 128), F32)

    qmb = qm_s[...].astype(BF16)
    contract_last = (((1,), (1,)), ((), ()))
    kt = jnp.concatenate([kbuf[slot, c] for c in range(pps)], axis=1).astype(BF16)
    vt = jnp.concatenate([vbuf[slot, c] for c in range(pps)], axis=1).astype(BF16)
    s_all = jnp.dot(qmb, kt, preferred_element_type=F32)
    for c in range(nb):
        s = s_all[:, c * MOBA_BLOCK:(c + 1) * MOBA_BLOCK]
        gate = jnp.sum(s, axis=1, keepdims=True) * (1.0 / MOBA_BLOCK)
        m = jnp.max(s, axis=1, keepdims=True)
        p = jnp.exp(s - m)
        l = jnp.sum(p, axis=1, keepdims=True)
        j = step * nb + c
        mp_s[...] = jnp.where(lane == j, m, mp_s[...])
        lp_s[...] = jnp.where(lane == j, l, lp_s[...])
        gp_s[...] = jnp.where(lane == j, gate, gp_s[...])
        pbd_s[c * rows:(c + 1) * rows, c * MOBA_BLOCK:(c + 1) * MOBA_BLOCK] = p.astype(BF16)
    op_s[pl.ds(pl.multiple_of(step * (nb * rows), nb * rows), nb * rows), :] = lax.dot_general(
        pbd_s[...], vt, contract_last, preferred_element_type=F32)

    @pl.when(step == ns - 1)
    def _():
        nblk = ns * nb
        sel = _top_blocks_bias(gp_s[...], lane, 128, 1) == 0.0
        zpad = jnp.zeros((128 - DEC_ROWS, WIDTH), F32)
        kn = jnp.concatenate([kn_ref[0], zpad], axis=0).astype(BF16)
        vn = jnp.concatenate([vn_ref[0], zpad], axis=0).astype(BF16)
        s = lax.dot_general(qmb, kn, contract_last, preferred_element_type=F32)
        qpos = jnp.minimum(lax.broadcasted_iota(jnp.int32, (rows, 128), 0) % DEC_ROWS, t_valid - 1)
        s = jnp.where(lane <= qpos, s, NEG_INF)
        m_own = jnp.max(s, axis=1, keepdims=True)
        p = jnp.exp(s - m_own)
        l_own = jnp.sum(p, axis=1, keepdims=True)
        o_own = jnp.dot(p.astype(BF16), vn, preferred_element_type=F32)
        mp = mp_s[...]
        m_fin = jnp.maximum(m_own, jnp.max(jnp.where(sel, mp, NEG_INF), axis=1, keepdims=True))
        w = jnp.where(sel, jnp.exp(mp - m_fin), 0.0)
        a_own = jnp.exp(m_own - m_fin)
        l_fin = a_own * l_own + jnp.sum(w * lp_s[...], axis=1, keepdims=True)
        o_fin = a_own * o_own
        for j in range(nblk):
            o_fin = o_fin + w[:, j:j + 1] * op_s[j * rows:(j + 1) * rows, :]
        y = o_fin / l_fin
        y_b = None
        for h in range(N_HEADS):
            part = jnp.where(lane_head == h, y[h * DEC_ROWS:(h + 1) * DEC_ROWS, :], 0.0)
            y_b = part if y_b is None else y_b + part
        yb_ref[0] = y_b


def _dec_call(page_table, cache_k, cache_v, layer, q, kn, vn, *, t_valid):
    bsz, n_pages = page_table.shape
    depth, n_pool, page, _, _ = cache_k.shape
    assert (n_pages * page) % MOBA_BLOCK == 0 and MOBA_BLOCK == 2 * page
    nblk = n_pages * page // MOBA_BLOCK
    assert MOBA_TOPK <= nblk <= 128
    pps = next(c for c in (16, 8, 4, 2) if n_pages % c == 0)
    ns = n_pages // pps
    ck = cache_k.transpose(0, 1, 3, 4, 2).reshape(depth, n_pool, WIDTH, page)
    cv = cache_v.transpose(0, 1, 3, 4, 2).reshape(depth, n_pool, WIDTH, page)

    in_hbm = pl.BlockSpec(memory_space=pl.ANY)
    per_b = pl.BlockSpec((1, DEC_ROWS, WIDTH), lambda b, s, pt: (b, 0, 0))
    rows = N_HEADS * DEC_ROWS
    body = functools.partial(_dec_body, layer=layer, n_seq=bsz, pps=pps, ns=ns, t_valid=t_valid)
    grid_spec = pltpu.PrefetchScalarGridSpec(
        num_scalar_prefetch=1,
        grid=(bsz, ns),
        in_specs=[in_hbm, in_hbm, per_b, per_b, per_b],
        out_specs=per_b,
        scratch_shapes=[pltpu.VMEM((rows, WIDTH), F32), pltpu.VMEM((rows, 128), F32),
                        pltpu.VMEM((rows, 128), F32), pltpu.VMEM((rows, 128), F32),
                        pltpu.VMEM((nblk * rows, WIDTH), F32),
                        pltpu.VMEM((pps // 2 * rows, pps * page), BF16),
                        pltpu.VMEM((DEC_SLOTS, pps, WIDTH, page), F32),
                        pltpu.VMEM((DEC_SLOTS, pps, WIDTH, page), F32),
                        pltpu.SemaphoreType.DMA((2, DEC_SLOTS))],
    )
    return pl.pallas_call(
        body,
        grid_spec=grid_spec,
        out_shape=jax.ShapeDtypeStruct((bsz, DEC_ROWS, WIDTH), F32),
        compiler_params=pltpu.CompilerParams(dimension_semantics=("arbitrary", "arbitrary"),
                                             vmem_limit_bytes=VMEM_LIMIT),
        name="attn_sample",
    )(page_table, ck, cv, q, kn, vn)


def _proj_body(yb_ref, mixed_ref, x_ref, wout_ref, fng_ref, out_ref, *, final):
    o = _out_proj(x_ref[...], mixed_ref[...], yb_ref[...], wout_ref)
    if final:
        o = _rmsnorm(o, fng_ref[...])
    out_ref[...] = o


def _proj_call(y_b, mixed, x, w_out, layer, fng, *, final):
    bsz, t_len, d_model = x.shape
    n = bsz * t_len
    whole = lambda r, c: pl.BlockSpec((r, c), lambda i: (0, 0))
    out = pl.pallas_call(
        functools.partial(_proj_body, final=final),
        grid=(1,),
        in_specs=[whole(n, WIDTH), whole(n, 4 * WIDTH), whole(n, d_model),
                  pl.BlockSpec((None, 4 * WIDTH, d_model), lambda i: (layer, 0, 0)),
                  whole(1, d_model)],
        out_specs=whole(n, d_model),
        out_shape=jax.ShapeDtypeStruct((n, d_model), F32),
        compiler_params=pltpu.CompilerParams(dimension_semantics=("arbitrary",), vmem_limit_bytes=VMEM_LIMIT),
        name="proj_sample",
    )(y_b.reshape(n, WIDTH), mixed.reshape(n, 4 * WIDTH), x.reshape(n, d_model), w_out, fng)
    return out.reshape(bsz, t_len, d_model)


def _rope_tables(pos0, n):
    freqs = ROPE_THETA ** (-jnp.arange(ROT_HALF, dtype=F32) * 2.0 / (2 * ROT_HALF))
    ang = (pos0 + jnp.arange(n, dtype=jnp.int32)).astype(F32)[:, None] * freqs[None, :]
    cos, sin = jnp.cos(ang), jnp.sin(ang)
    one = jnp.ones((n, HEAD_DIM - 2 * ROT_HALF), F32)
    zero8 = jnp.zeros((n, ROT_HALF), F32)
    zero_rest = jnp.zeros((n, HEAD_DIM - 2 * ROT_HALF), F32)
    rc = jnp.concatenate([cos, cos, one], axis=1)
    rsa = jnp.concatenate([-sin, zero8, zero_rest], axis=1)
    rsb = jnp.concatenate([zero8, sin, zero_rest], axis=1)
    return tuple(jnp.tile(a, (1, N_HEADS)) for a in (rc, rsa, rsb))


def _pad_state(state):
    return jnp.pad(state, ((0, 0), (0, 0), (HALO - state.shape[2], 0), (0, 0)))


def kernel(x_prompt, x_sample, cache_k, cache_v, page_table, state_conv_a, state_conv_c, state_pool_d,
           norm_g, w_in, conv_a_w, conv_c_w, conv_c_b, ln_c_g, ln_c_b, w_pw_c, w_pool, pool_scale, w_out,
           final_norm_g):
    depth = w_in.shape[0]
    bsz, seq, d_model = x_prompt.shape
    dec_b, dec_t, _ = x_sample.shape
    past_len = page_table.shape[1] * cache_k.shape[2]
    assert seq % MOBA_BLOCK == 0 and dec_t <= DEC_ROWS and WIDTH == N_HEADS * HEAD_DIM

    pre_tile = next(t for t in (PRE_TILE, MOBA_BLOCK) if seq % t == 0)
    tabs_p = _rope_tables(0, seq)
    tabs_s = _rope_tables(past_len, DEC_ROWS)
    zero_state = jnp.zeros((1, bsz, HALO, WIDTH), F32)
    hist_a, hist_c, hist_d = _pad_state(state_conv_a), _pad_state(state_conv_c), _pad_state(state_pool_d)
    fng = final_norm_g.reshape(1, d_model)
    yp = x_prompt
    ys = jnp.pad(x_sample, ((0, 0), (0, DEC_ROWS - dec_t), (0, 0)))
    outs = [[] for _ in range(10)]
    n_groups = w_pool.shape[1]
    eye = jnp.eye(n_groups, dtype=F32)
    pool_bd = (w_pool[:, :, :, None, :] * eye[None, :, None, :, None]).reshape(depth, WIDTH, WIDTH)
    row = lambda a: a.reshape(depth, 1, a.shape[-1])
    lw = {"norm_g": row(norm_g), "w_in": w_in.astype(BF16), "conv_a_w": conv_a_w, "conv_c_w": conv_c_w,
          "conv_c_b": row(conv_c_b), "ln_c_g": row(ln_c_g), "ln_c_b": row(ln_c_b),
          "w_pw_c": w_pw_c.astype(BF16), "w_pool": pool_bd.astype(BF16), "pool_scale": row(pool_scale)}
    wo = w_out.astype(BF16)
    for l in range(depth):
        final = l == depth - 1

        mixed, kp, vp, qt, kb, vt, km, ap, cp, dp = _pre_call(
            yp, lw, l, tabs_p, zero_state, zero_state, zero_state, 0,
            ns=1, tm=pre_tile, last_rows=pre_tile, pos0=0, prompt=True)
        yp = _attn_call(qt, kb, vt, km, mixed, yp, wo, l, fng, final=final)

        mixed_s, ks, vs, qs, as_, cs, ds = _pre_call(
            ys, lw, l, tabs_s, hist_a, hist_c, hist_d, l,
            ns=dec_b, tm=DEC_ROWS, last_rows=dec_t, pos0=past_len, prompt=False)
        yb_s = _dec_call(page_table, cache_k, cache_v, l, qs, ks, vs, t_valid=dec_t)
        ys = _proj_call(yb_s, mixed_s, ys, wo, l, fng, final=final)

        for lst, val in zip(outs, (kp.reshape(bsz, seq, N_HEADS, HEAD_DIM),
                                   vp.reshape(bsz, seq, N_HEADS, HEAD_DIM),
                                   ks[:, :dec_t].reshape(dec_b, dec_t, N_HEADS, HEAD_DIM),
                                   vs[:, :dec_t].reshape(dec_b, dec_t, N_HEADS, HEAD_DIM),
                                   ap, as_, cp, cs, dp, ds)):
            lst.append(val)
    return (yp, ys[:, :dec_t]) + tuple(jnp.stack(o) for o in outs)
```

```python
import functools

import jax
import jax.numpy as jnp
from jax import lax
from jax.experimental import pallas as pl
from jax.experimental.pallas import tpu as pltpu

F32 = jnp.float32
BF16 = jnp.bfloat16

WIDTH = 256
N_HEADS = 4
HEAD_DIM = 64
ROT_HALF = 8
ROPE_THETA = 500000.0
MOBA_BLOCK = 256
MOBA_TOPK = 3
CONV_A_WIDTH = 3
CONV_C_WIDTH = 31
POOL_WINDOWS = (2, 4, 8, 16)
POOL_STATE = 15
RMS_EPS = 1e-6
LN_EPS = 1e-5
HALO = 32
DEC_ROWS = 8
ATTN_GROUP = 4
DEC_SLOTS = 3
PRE_TILE = 1024
NEG_INF = float("-inf")
LOG2_E = 1.4426950408889634
LANES = 128
V7X_VMEM_BYTES = 64 * 1024 * 1024
VMEM_LIMIT = V7X_VMEM_BYTES - 8 * 1024 * 1024


def _silu(z):
    return z * jax.nn.sigmoid(z)


def _rmsnorm(x, g):
    ms = jnp.mean(x * x, axis=-1, keepdims=True)
    return x * lax.rsqrt(ms + RMS_EPS) * g


def _out_proj(x, mixed, y_b, wout_ref):
    yb = (y_b * mixed[:, WIDTH:2 * WIDTH]).astype(BF16)
    o = x + jnp.dot(mixed[:, 0:WIDTH].astype(BF16), wout_ref[0:WIDTH, :], preferred_element_type=F32)
    o = o + jnp.dot(yb, wout_ref[WIDTH:2 * WIDTH, :], preferred_element_type=F32)
    o = o + jnp.dot(mixed[:, 2 * WIDTH:4 * WIDTH].astype(BF16), wout_ref[2 * WIDTH:4 * WIDTH, :],
                    preferred_element_type=F32)
    return o


def _pre_body(x_ref, ng_ref, win_ref, caw_ref, ccw_ref, ccb_ref, lng_ref, lnb_ref, wpw_ref, wpool_ref,
              psc_ref, rc_ref, rsa_ref, rsb_ref, prea_ref, prec_ref, pred_ref, *refs,
              ns, tm, last_rows, pos0, prompt):
    if prompt:
        (_, _, mixed_ref, kst_ref, vst_ref, qt_ref, kb_ref, vt_ref, km_ref,
         na_ref, nc_ref, nd_ref, bufa, bufc, bufd, shc, x2, x4, x8) = refs
    else:
        (mixed_ref, k_ref, v_ref, q_ref, na_ref, nc_ref, nd_ref, bufa, bufc, bufd, shc, x2, x4, x8) = refs
    t = pl.program_id(1)
    nt = pl.num_programs(1)
    span = tm + HALO - 8
    n = ns * tm
    seq = lambda z: z.reshape(ns, tm, WIDTH)
    flat = lambda z: z.reshape(n, WIDTH)

    @pl.when(t == 0)
    def _():
        bufa[:, 0:HALO, :] = prea_ref[...]
        bufc[:, 0:HALO, :] = prec_ref[...]
        bufd[:, 0:HALO, :] = pred_ref[...]

    x = x_ref[...].reshape(n, x_ref.shape[-1])
    h = _rmsnorm(x, ng_ref[...]).astype(BF16)

    def proj(c0, cn):
        return jnp.dot(h, win_ref[:, c0 * WIDTH:(c0 + cn) * WIDTH], preferred_element_type=F32)

    def put(col, val):
        mixed_ref[:, :, col * WIDTH:(col + 1) * WIDTH] = seq(val)

    pa = proj(0, 4)
    bufa[:, HALO:HALO + tm, :] = seq(pa[:, 2 * WIDTH:3 * WIDTH] * pa[:, 0:WIDTH])
    conv = None
    for j in range(CONV_A_WIDTH):
        off = HALO - (CONV_A_WIDTH - 1) + j
        term = caw_ref[j:j + 1, :] * bufa[:, off:off + tm, :]
        conv = term if conv is None else conv + term
    put(0, pa[:, WIDTH:2 * WIDTH] * flat(conv) * _silu(pa[:, 3 * WIDTH:4 * WIDTH]))

    pb = proj(4, 4)
    tab = lambda r: flat(jnp.broadcast_to(r[...][None], (ns, tm, WIDTH)))
    rc, rsa, rsb = tab(rc_ref), tab(rsa_ref), tab(rsb_ref)

    def rope(z):
        return z * rc + pltpu.roll(z, WIDTH - ROT_HALF, 1) * rsa + pltpu.roll(z, ROT_HALF, 1) * rsb

    qs = rope(pb[:, 0:WIDTH]) * (HEAD_DIM ** -0.5)
    kr = rope(pb[:, WIDTH:2 * WIDTH])
    v = pb[:, 2 * WIDTH:3 * WIDTH]
    put(1, _silu(pb[:, 3 * WIDTH:4 * WIDTH]))
    if prompt:
        kst_ref[0] = kr.T
        v_t = v.T
        vst_ref[0] = v_t
        for r in range(tm // MOBA_BLOCK):
            blk = slice(r * MOBA_BLOCK, (r + 1) * MOBA_BLOCK)
            qt_ref[0, r] = qs[blk, :].T
            kb_ref[0, r] = kr[blk, :].astype(BF16)
            vt_ref[0, r] = v_t[:, blk].astype(BF16)
            km_ref[0, r] = jnp.mean(kr[blk, :], axis=0, keepdims=True)
    else:
        k_ref[...] = seq(kr)
        v_ref[...] = seq(v)
        q_ref[...] = seq(qs)

    pc = proj(8, 3)
    bufc[:, HALO:HALO + tm, :] = seq(pc[:, 0:WIDTH] * jax.nn.sigmoid(pc[:, WIDTH:2 * WIDTH]))
    for r in range(1, 8):
        shc[r - 1] = bufc[:, r:r + span, :]
    acc = None
    for j in range(CONV_C_WIDTH):
        a, r = divmod(HALO - (CONV_C_WIDTH - 1) + j, 8)
        rows = bufc[:, 8 * a:8 * a + tm, :] if r == 0 else shc[r - 1, :, 8 * a:8 * a + tm, :]
        term = ccw_ref[j:j + 1, :] * rows
        acc = term if acc is None else acc + term
    acc = flat(acc) + ccb_ref[...]
    mu = jnp.mean(acc, axis=-1, keepdims=True)
    cen = acc - mu
    var = jnp.mean(cen * cen, axis=-1, keepdims=True)
    ln = cen * lax.rsqrt(var + LN_EPS) * lng_ref[...] + lnb_ref[...]
    yc = jnp.dot(_silu(ln).astype(BF16), wpw_ref[...], preferred_element_type=F32)
    put(2, yc * _silu(pc[:, 2 * WIDTH:3 * WIDTH]))

    pd = proj(11, 2)
    dx = pd[:, 0:WIDTH]
    bufd[:, HALO:HALO + tm, :] = seq(dx)
    assert POOL_WINDOWS == (2, 4, 8, 16) and HALO == 32
    x2[...] = bufd[:, 8:8 + span, :] + bufd[:, 7:7 + span, :]
    x4[:, 0:span - 8, :] = x2[:, 8:span, :] + x2[:, 6:span - 2, :]
    x8[:, 0:span - 16, :] = x4[:, 8:span - 8, :] + x4[:, 4:span - 12, :]
    sums = {2: x2[:, 24:24 + tm, :], 4: x4[:, 16:16 + tm, :], 8: x8[:, 8:8 + tm, :],
            16: x8[:, 8:8 + tm, :] + x8[:, 0:tm, :]}
    group = lax.broadcasted_iota(jnp.int32, (ns, tm, WIDTH), 2) // (WIDTH // len(POOL_WINDOWS))
    pos = pos0 + t * tm + lax.broadcasted_iota(jnp.int32, (ns, tm, WIDTH), 1)
    wsum = sums[POOL_WINDOWS[-1]]
    win = jnp.full((ns, tm, WIDTH), POOL_WINDOWS[-1], jnp.int32)
    for g in range(len(POOL_WINDOWS) - 2, -1, -1):
        wsum = jnp.where(group == g, sums[POOL_WINDOWS[g]], wsum)
        win = jnp.where(group == g, POOL_WINDOWS[g], win)
    cnt = jnp.minimum(pos + 1, win).astype(F32)
    diff = flat(wsum / cnt) - dx
    yd = jnp.dot(diff.astype(BF16), wpool_ref[...], preferred_element_type=F32) * psc_ref[...]
    put(3, yd * _silu(pd[:, WIDTH:2 * WIDTH]))

    @pl.when(t == nt - 1)
    def _():
        end = HALO + last_rows
        na_ref[...] = bufa[:, end - (CONV_A_WIDTH - 1):end, :]
        nc_ref[...] = bufc[:, end - (CONV_C_WIDTH - 1):end, :]
        nd_ref[...] = bufd[:, end - POOL_STATE:end, :]

    if tm >= HALO:
        @pl.when(t < nt - 1)
        def _():
            bufa[:, 0:HALO, :] = bufa[:, tm:tm + HALO, :]
            bufc[:, 0:HALO, :] = bufc[:, tm:tm + HALO, :]
            bufd[:, 0:HALO, :] = bufd[:, tm:tm + HALO, :]


def _pre_call(x, lw, layer, rope_tabs, pre_a, pre_c, pre_d, state_layer, kv_stacks=None, *,
              ns, tm, last_rows, pos0, prompt):
    bsz, t_len, d_model = x.shape
    nt = t_len // tm
    assert nt * tm == t_len and (nt == 1 or tm >= HALO) and bsz % ns == 0 and tm % 8 == 0
    rc, rsa, rsb = rope_tabs
    d_in = lw["w_in"].shape[-1]

    def vec(n):
        return pl.BlockSpec((None, 1, n), lambda b, t: (layer, 0, 0))

    def mat(r, c):
        return pl.BlockSpec((None, r, c), lambda b, t: (layer, 0, 0))

    def per_b(shape):
        return pl.BlockSpec((ns,) + shape, lambda b, t: (b,) + (0,) * len(shape))

    hist = pl.BlockSpec((None, ns, HALO, WIDTH), lambda b, t: (state_layer, b, 0, 0))
    tok = lambda w: pl.BlockSpec((ns, tm, w), lambda b, t: (b, t, 0))
    tab = pl.BlockSpec((tm, WIDTH), lambda b, t: (t, 0))
    in_specs = [tok(d_model), vec(d_model), mat(d_model, d_in),
                mat(CONV_A_WIDTH, WIDTH), mat(CONV_C_WIDTH, WIDTH), vec(WIDTH),
                vec(WIDTH), vec(WIDTH), mat(WIDTH, WIDTH), mat(WIDTH, WIDTH),
                vec(WIDTH), tab, tab, tab, hist, hist, hist]
    out_shape = [jax.ShapeDtypeStruct((bsz, t_len, 4 * WIDTH), F32),
                 jax.ShapeDtypeStruct((bsz, t_len, WIDTH), F32),
                 jax.ShapeDtypeStruct((bsz, t_len, WIDTH), F32)]
    out_specs = [tok(4 * WIDTH), tok(WIDTH), tok(WIDTH)]
    operands = [x, lw["norm_g"], lw["w_in"], lw["conv_a_w"], lw["conv_c_w"], lw["conv_c_b"], lw["ln_c_g"],
                lw["ln_c_b"], lw["w_pw_c"], lw["w_pool"], lw["pool_scale"], rc, rsa, rsb, pre_a, pre_c, pre_d]
    aliases = {}
    if prompt:
        stack_spec = pl.BlockSpec((None, 1, WIDTH, tm), lambda b, t: (layer, b, 0, t))
        for pos, stack in enumerate(kv_stacks):
            aliases[len(operands)] = 1 + pos
            operands.append(stack)
            in_specs.append(pl.BlockSpec(memory_space=pl.ANY))
            out_shape[1 + pos] = jax.ShapeDtypeStruct(stack.shape, F32)
            out_specs[1 + pos] = stack_spec
        assert tm % MOBA_BLOCK == 0 and ns == 1
        per_tile = tm // MOBA_BLOCK
        nblk = t_len // MOBA_BLOCK
        blk = lambda r, c: pl.BlockSpec((1, per_tile, r, c), lambda b, t: (b, t, 0, 0))
        out_shape += [jax.ShapeDtypeStruct((bsz, nblk, WIDTH, MOBA_BLOCK), F32),
                      jax.ShapeDtypeStruct((bsz, nblk, MOBA_BLOCK, WIDTH), BF16),
                      jax.ShapeDtypeStruct((bsz, nblk, WIDTH, MOBA_BLOCK), BF16),
                      jax.ShapeDtypeStruct((bsz, nblk, 1, WIDTH), F32)]
        out_specs += [blk(WIDTH, MOBA_BLOCK), blk(MOBA_BLOCK, WIDTH), blk(WIDTH, MOBA_BLOCK), blk(1, WIDTH)]
    else:
        out_shape += [jax.ShapeDtypeStruct((bsz, t_len, WIDTH), F32)]
        out_specs += [tok(WIDTH)]
    out_shape += [jax.ShapeDtypeStruct((bsz, CONV_A_WIDTH - 1, WIDTH), F32),
                  jax.ShapeDtypeStruct((bsz, CONV_C_WIDTH - 1, WIDTH), F32),
                  jax.ShapeDtypeStruct((bsz, POOL_STATE, WIDTH), F32)]
    out_specs += [per_b((CONV_A_WIDTH - 1, WIDTH)), per_b((CONV_C_WIDTH - 1, WIDTH)),
                  per_b((POOL_STATE, WIDTH))]
    body = functools.partial(_pre_body, ns=ns, tm=tm, last_rows=last_rows, pos0=pos0, prompt=prompt)
    return pl.pallas_call(
        body,
        grid=(bsz // ns, nt),
        in_specs=in_specs,
        out_specs=out_specs,
        out_shape=out_shape,
        scratch_shapes=[pltpu.VMEM((ns, HALO + tm, WIDTH), F32)] * 3
        + [pltpu.VMEM((7, ns, HALO + tm - 8, WIDTH), F32)] + [pltpu.VMEM((ns, HALO + tm - 8, WIDTH), F32)] * 3,
        input_output_aliases=aliases,
        compiler_params=pltpu.CompilerParams(dimension_semantics=("arbitrary", "arbitrary"),
                                             vmem_limit_bytes=VMEM_LIMIT),
        name="pre_prompt" if prompt else "pre_sample",
    )(*operands)


def _top_blocks_bias(gate, index, n_index, axis):
    bias = jnp.full(gate.shape, NEG_INF, F32)
    for _ in range(MOBA_TOPK):
        top = jnp.max(gate, axis=axis, keepdims=True)
        cand = jnp.logical_and(gate == top, top > NEG_INF)
        first = jnp.min(jnp.where(cand, index, n_index), axis=axis, keepdims=True)
        pick = index == first
        bias = jnp.where(pick, 0.0, bias)
        gate = jnp.where(pick, NEG_INF, gate)
    return bias


def _attn_body(qt_ref, kb_ref, vt_ref, km_ref, mixed_ref, x_ref, wout_ref, fng_ref, out_ref,
               qm_s, bias_s, acc_s, s_a, s_b, p_a, p_b, al_a, al_b, *, nblk, tq, grp, final):
    s_bufs, p_bufs, al_bufs = (s_a, s_b), (p_a, p_b), (al_a, al_b)
    w4 = N_HEADS * tq
    i = pl.program_id(1)
    qt = qt_ref[0, 0]
    row_head = lax.broadcasted_iota(jnp.int32, (WIDTH, tq), 0) // HEAD_DIM
    km = km_ref[0]
    lane_head = lax.broadcasted_iota(jnp.int32, (nblk, WIDTH), 1) // HEAD_DIM
    blk = lax.broadcasted_iota(jnp.int32, (nblk, tq), 0)
    km_heads = jnp.concatenate([jnp.where(lane_head == h, km, 0.0) for h in range(N_HEADS)], axis=0)
    gates = jnp.dot(km_heads, qt, precision=lax.Precision.HIGHEST, preferred_element_type=F32)
    for h in range(N_HEADS):
        cols = slice(h * tq, (h + 1) * tq)
        qm_s[:, cols] = jnp.where(row_head == h, qt * LOG2_E, 0.0).astype(BF16)
        gate = jnp.where(blk < i, gates[h * nblk:(h + 1) * nblk, :], NEG_INF)
        bias_s[:, cols] = _top_blocks_bias(gate, blk, nblk, 0)

    n_groups = nblk // grp
    sub = [slice(c * MOBA_BLOCK, (c + 1) * MOBA_BLOCK) for c in range(grp)]

    def qk(group, slot):
        kb = kb_ref[0, pl.ds(group * grp, grp)].reshape(grp * MOBA_BLOCK, WIDTH)
        s_bufs[slot][...] = jnp.dot(kb, qm_s[...], preferred_element_type=F32)

    def pv(group, slot):
        for h in range(N_HEADS):
            rows = slice(h * HEAD_DIM, (h + 1) * HEAD_DIM)
            cols = slice(h * tq, (h + 1) * tq)
            vt = jnp.concatenate([vt_ref[0, group * grp + c, rows, :] for c in range(grp)], axis=1)
            o = jnp.dot(vt, p_bufs[slot][:, cols], preferred_element_type=F32)
            acc_s[rows, :] = al_bufs[slot][:, cols] * acc_s[rows, :] + o

    qk(0, 1)
    key = lax.broadcasted_iota(jnp.int32, (MOBA_BLOCK, w4), 0)
    qry = lax.broadcasted_iota(jnp.int32, (MOBA_BLOCK, w4), 1) % tq
    s = jnp.dot(kb_ref[0, i], qm_s[...], preferred_element_type=F32)
    s = jnp.where(key <= qry, s, NEG_INF)
    m = jnp.max(s, axis=0, keepdims=True)
    p = jnp.exp2(s - m)
    l = jnp.sum(p, axis=0, keepdims=True)
    pb = p.astype(BF16)
    for h in range(N_HEADS):
        rows = slice(h * HEAD_DIM, (h + 1) * HEAD_DIM)
        acc_s[rows, :] = jnp.dot(vt_ref[0, i, rows, :], pb[:, h * tq:(h + 1) * tq], preferred_element_type=F32)
    p_bufs[0][...] = jnp.zeros((grp * MOBA_BLOCK, w4), BF16)
    al_bufs[0][...] = jnp.ones((1, w4), F32)

    def half(n, m, l, cur, nxt):
        picked = [bias_s[pl.ds((n - 1) * grp + c, 1), :] for c in range(grp)]
        m_new = m
        for c in range(grp):
            m_new = jnp.maximum(m_new, jnp.max(s_bufs[cur][sub[c], :], axis=0, keepdims=True) + picked[c])
        alpha = jnp.exp2(m - m_new)
        l = alpha * l
        for c in range(grp):
            p = jnp.exp2(s_bufs[cur][sub[c], :] - (m_new - picked[c]))
            l = l + jnp.sum(p, axis=0, keepdims=True)
            p_bufs[cur][sub[c], :] = p.astype(BF16)
        al_bufs[cur][...] = alpha
        pv(jnp.maximum(n - 2, 0), nxt)
        qk(jnp.minimum(n, n_groups - 1), nxt)
        return m_new, l

    def visit(n, carry):
        return lax.cond(n % 2 == 1, lambda: half(n, carry[0], carry[1], 1, 0),
                        lambda: half(n, carry[0], carry[1], 0, 1))

    n_visits = (i + grp - 1) // grp
    m, l = lax.fori_loop(1, n_visits + 1, visit, (m, l))
    odd = n_visits % 2 == 1

    @pl.when(odd)
    def _():
        pv(n_visits - 1, 1)

    @pl.when(jnp.logical_not(odd))
    def _():
        pv(jnp.maximum(n_visits - 1, 0), 0)

    inv = 1.0 / l
    for h in range(N_HEADS):
        rows = slice(h * HEAD_DIM, (h + 1) * HEAD_DIM)
        acc_s[rows, :] = acc_s[rows, :] * inv[:, h * tq:(h + 1) * tq]
    y_b = acc_s[...].T
    o = _out_proj(x_ref[0], mixed_ref[0], y_b, wout_ref)
    if final:
        o = _rmsnorm(o, fng_ref[...])
    out_ref[0] = o


def _attn_call(qt, kb, vt, km, mixed, x, w_out, layer, fng, *, final):
    bsz, nblk, _, tq = qt.shape
    d_model = x.shape[-1]
    assert tq == MOBA_BLOCK
    grp = next(g for g in (ATTN_GROUP, 2, 1) if nblk % g == 0)
    body = functools.partial(_attn_body, nblk=nblk, tq=tq, grp=grp, final=final)
    whole = lambda r, c: pl.BlockSpec((1, nblk, r, c), lambda b, i: (b, 0, 0, 0))
    return pl.pallas_call(
        body,
        grid=(bsz, nblk),
        in_specs=[pl.BlockSpec((1, 1, WIDTH, tq), lambda b, i: (b, i, 0, 0)),
                  whole(MOBA_BLOCK, WIDTH), whole(WIDTH, MOBA_BLOCK),
                  pl.BlockSpec((1, nblk, WIDTH), lambda b, i: (b, 0, 0)),
                  pl.BlockSpec((1, tq, 4 * WIDTH), lambda b, i: (b, i, 0)),
                  pl.BlockSpec((1, tq, d_model), lambda b, i: (b, i, 0)),
                  pl.BlockSpec((None, 4 * WIDTH, d_model), lambda b, i: (layer, 0, 0)),
                  pl.BlockSpec((1, d_model), lambda b, i: (0, 0))],
        out_specs=pl.BlockSpec((1, tq, d_model), lambda b, i: (b, i, 0)),
        out_shape=jax.ShapeDtypeStruct(x.shape, F32),
        scratch_shapes=[pltpu.VMEM((WIDTH, N_HEADS * tq), BF16),
                        pltpu.VMEM((nblk, N_HEADS * tq), F32),
                        pltpu.VMEM((WIDTH, tq), F32)]
        + [pltpu.VMEM((grp * MOBA_BLOCK, N_HEADS * tq), F32)] * 2
        + [pltpu.VMEM((grp * MOBA_BLOCK, N_HEADS * tq), BF16)] * 2
        + [pltpu.VMEM((1, N_HEADS * tq), F32)] * 2,
        compiler_params=pltpu.CompilerParams(dimension_semantics=("arbitrary", "arbitrary"),
                                             vmem_limit_bytes=VMEM_LIMIT),
        name="attn_prompt",
    )(qt, kb, vt, km.reshape(bsz, nblk, WIDTH), mixed, x, w_out, fng)


def _dec_body(pt_ref, *refs, layer, n_seq, pps, ns, t_valid):
    (ck_ref, cv_ref, q_ref, kn_ref, vn_ref, yb_ref,
     qm_s, mp_s, lp_s, gp_s, op_s, pbd_s, kbuf, vbuf, sems) = refs
    step = pl.program_id(1)
    rows = N_HEADS * DEC_ROWS
    nb = pps // 2
    lane_head = lax.broadcasted_iota(jnp.int32, (DEC_ROWS, WIDTH), 1) // HEAD_DIM
    lane = lax.broadcasted_iota(jnp.int32, (rows, LANES), 1)

    g = pl.program_id(0) * ns + step
    n_steps = pl.num_programs(0) * ns

    def page_copies(gg, page_of):
        slot = gg % DEC_SLOTS
        for c in range(pps):
            page = page_of(c)
            yield pltpu.make_async_copy(ck_ref.at[layer, page], kbuf.at[slot, c], sems.at[0, slot])
            yield pltpu.make_async_copy(cv_ref.at[layer, page], vbuf.at[slot, c], sems.at[1, slot])

    def request(gg):
        seq_id, seq_step = gg // ns, gg % ns
        for copy in page_copies(gg, lambda c: pt_ref[seq_id, seq_step * pps + c]):
            copy.start()

    @pl.when(g == 0)
    def _():
        pbd_s[...] = jnp.zeros(pbd_s.shape, BF16)
        for ahead in range(min(DEC_SLOTS - 1, n_seq * ns)):
            request(ahead)

    @pl.when(g + (DEC_SLOTS - 1) < n_steps)
    def _():
        request(g + (DEC_SLOTS - 1))

    for copy in page_copies(g, lambda c: 0):
        copy.wait()
    slot = g % DEC_SLOTS

    @pl.when(step == 0)
    def _():
        q8 = q_ref[0]
        for h in range(N_HEADS):
            qm_s[h * DEC_ROWS:(h + 1) * DEC_ROWS, :] = jnp.where(lane_head == h, q8, 0.0)
        mp_s[...] = jnp.full((rows, LANES), NEG_INF, F32)
        gp_s[...] = jnp.full((rows, LANES), NEG_INF, F32)
        lp_s[...] = jnp.zeros((rows, LANES), F32)

    qmb = qm_s[...].astype(BF16)
    contract_last = (((1,), (1,)), ((), ()))
    kt = jnp.concatenate([kbuf[slot, c] for c in range(pps)], axis=1).astype(BF16)
    vt = jnp.concatenate([vbuf[slot, c] for c in range(pps)], axis=1).astype(BF16)
    s_all = jnp.dot(qmb, kt, preferred_element_type=F32)
    for c in range(nb):
        s = s_all[:, c * MOBA_BLOCK:(c + 1) * MOBA_BLOCK]
        gate = jnp.sum(s, axis=1, keepdims=True) * (1.0 / MOBA_BLOCK)
        m = jnp.max(s, axis=1, keepdims=True)
        p = jnp.exp(s - m)
        l = jnp.sum(p, axis=1, keepdims=True)
        j = step * nb + c
        mp_s[...] = jnp.where(lane == j, m, mp_s[...])
        lp_s[...] = jnp.where(lane == j, l, lp_s[...])
        gp_s[...] = jnp.where(lane == j, gate, gp_s[...])
        pbd_s[c * rows:(c + 1) * rows, c * MOBA_BLOCK:(c + 1) * MOBA_BLOCK] = p.astype(BF16)
    op_s[pl.ds(pl.multiple_of(step * (nb * rows), nb * rows), nb * rows), :] = lax.dot_general(
        pbd_s[...], vt, contract_last, preferred_element_type=F32)

    @pl.when(step == ns - 1)
    def _():
        nblk = ns * nb
        sel = _top_blocks_bias(gp_s[...], lane, LANES, 1) == 0.0
        zpad = jnp.zeros((LANES - DEC_ROWS, WIDTH), F32)
        kn = jnp.concatenate([kn_ref[0], zpad], axis=0).astype(BF16)
        vn = jnp.concatenate([vn_ref[0], zpad], axis=0).astype(BF16)
        s = lax.dot_general(qmb, kn, contract_last, preferred_element_type=F32)
        qpos = jnp.minimum(lax.broadcasted_iota(jnp.int32, (rows, LANES), 0) % DEC_ROWS, t_valid - 1)
        s = jnp.where(lane <= qpos, s, NEG_INF)
        m_own = jnp.max(s, axis=1, keepdims=True)
        p = jnp.exp(s - m_own)
        l_own = jnp.sum(p, axis=1, keepdims=True)
        o_own = jnp.dot(p.astype(BF16), vn, preferred_element_type=F32)
        mp = mp_s[...]
        m_fin = jnp.maximum(m_own, jnp.max(jnp.where(sel, mp, NEG_INF), axis=1, keepdims=True))
        w = jnp.where(sel, jnp.exp(mp - m_fin), 0.0)
        a_own = jnp.exp(m_own - m_fin)
        l_fin = a_own * l_own + jnp.sum(w * lp_s[...], axis=1, keepdims=True)
        o_fin = a_own * o_own
        for j in range(nblk):
            o_fin = o_fin + w[:, j:j + 1] * op_s[j * rows:(j + 1) * rows, :]
        y = o_fin / l_fin
        y_b = None
        for h in range(N_HEADS):
            part = jnp.where(lane_head == h, y[h * DEC_ROWS:(h + 1) * DEC_ROWS, :], 0.0)
            y_b = part if y_b is None else y_b + part
        yb_ref[0] = y_b


def _dec_call(page_table, cache_k, cache_v, layer, q, kn, vn, *, t_valid):
    bsz, n_pages = page_table.shape
    depth, n_pool, page, _, _ = cache_k.shape
    assert (n_pages * page) % MOBA_BLOCK == 0 and MOBA_BLOCK == 2 * page
    nblk = n_pages * page // MOBA_BLOCK
    assert MOBA_TOPK <= nblk <= LANES
    pps = next(c for c in (16, 8, 4, 2) if n_pages % c == 0)
    ns = n_pages // pps
    ck = cache_k.transpose(0, 1, 3, 4, 2).reshape(depth, n_pool, WIDTH, page)
    cv = cache_v.transpose(0, 1, 3, 4, 2).reshape(depth, n_pool, WIDTH, page)

    in_hbm = pl.BlockSpec(memory_space=pl.ANY)
    per_b = pl.BlockSpec((1, DEC_ROWS, WIDTH), lambda b, s, pt: (b, 0, 0))
    rows = N_HEADS * DEC_ROWS
    body = functools.partial(_dec_body, layer=layer, n_seq=bsz, pps=pps, ns=ns, t_valid=t_valid)
    grid_spec = pltpu.PrefetchScalarGridSpec(
        num_scalar_prefetch=1,
        grid=(bsz, ns),
        in_specs=[in_hbm, in_hbm, per_b, per_b, per_b],
        out_specs=per_b,
        scratch_shapes=[pltpu.VMEM((rows, WIDTH), F32), pltpu.VMEM((rows, LANES), F32),
                        pltpu.VMEM((rows, LANES), F32), pltpu.VMEM((rows, LANES), F32),
                        pltpu.VMEM((nblk * rows, WIDTH), F32),
                        pltpu.VMEM((pps // 2 * rows, pps * page), BF16),
                        pltpu.VMEM((DEC_SLOTS, pps, WIDTH, page), F32),
                        pltpu.VMEM((DEC_SLOTS, pps, WIDTH, page), F32),
                        pltpu.SemaphoreType.DMA((2, DEC_SLOTS))],
    )
    return pl.pallas_call(
        body,
        grid_spec=grid_spec,
        out_shape=jax.ShapeDtypeStruct((bsz, DEC_ROWS, WIDTH), F32),
        compiler_params=pltpu.CompilerParams(dimension_semantics=("arbitrary", "arbitrary"),
                                             vmem_limit_bytes=VMEM_LIMIT),
        name="attn_sample",
    )(page_table, ck, cv, q, kn, vn)


def _proj_body(yb_ref, mixed_ref, x_ref, wout_ref, fng_ref, out_ref, *, final):
    o = _out_proj(x_ref[...], mixed_ref[...], yb_ref[...], wout_ref)
    if final:
        o = _rmsnorm(o, fng_ref[...])
    out_ref[...] = o


def _proj_call(y_b, mixed, x, w_out, layer, fng, *, final):
    bsz, t_len, d_model = x.shape
    n = bsz * t_len
    whole = lambda r, c: pl.BlockSpec((r, c), lambda i: (0, 0))
    out = pl.pallas_call(
        functools.partial(_proj_body, final=final),
        grid=(1,),
        in_specs=[whole(n, WIDTH), whole(n, 4 * WIDTH), whole(n, d_model),
                  pl.BlockSpec((None, 4 * WIDTH, d_model), lambda i: (layer, 0, 0)),
                  whole(1, d_model)],
        out_specs=whole(n, d_model),
        out_shape=jax.ShapeDtypeStruct((n, d_model), F32),
        compiler_params=pltpu.CompilerParams(dimension_semantics=("arbitrary",), vmem_limit_bytes=VMEM_LIMIT),
        name="proj_sample",
    )(y_b.reshape(n, WIDTH), mixed.reshape(n, 4 * WIDTH), x.reshape(n, d_model), w_out, fng)
    return out.reshape(bsz, t_len, d_model)


def _rope_tables(pos0, n):
    freqs = ROPE_THETA ** (-jnp.arange(ROT_HALF, dtype=F32) * 2.0 / (2 * ROT_HALF))
    ang = (pos0 + jnp.arange(n, dtype=jnp.int32)).astype(F32)[:, None] * freqs[None, :]
    cos, sin = jnp.cos(ang), jnp.sin(ang)
    one = jnp.ones((n, HEAD_DIM - 2 * ROT_HALF), F32)
    zero8 = jnp.zeros((n, ROT_HALF), F32)
    zero_rest = jnp.zeros((n, HEAD_DIM - 2 * ROT_HALF), F32)
    rc = jnp.concatenate([cos, cos, one], axis=1)
    rsa = jnp.concatenate([-sin, zero8, zero_rest], axis=1)
    rsb = jnp.concatenate([zero8, sin, zero_rest], axis=1)
    return tuple(jnp.tile(a, (1, N_HEADS)) for a in (rc, rsa, rsb))


def _pad_state(state):
    return jnp.pad(state, ((0, 0), (0, 0), (HALO - state.shape[2], 0), (0, 0)))


def kernel(x_prompt, x_sample, cache_k, cache_v, page_table, state_conv_a, state_conv_c, state_pool_d,
           norm_g, w_in, conv_a_w, conv_c_w, conv_c_b, ln_c_g, ln_c_b, w_pw_c, w_pool, pool_scale, w_out,
           final_norm_g):
    depth = w_in.shape[0]
    bsz, seq, d_model = x_prompt.shape
    dec_b, dec_t, _ = x_sample.shape
    past_len = page_table.shape[1] * cache_k.shape[2]
    assert seq % MOBA_BLOCK == 0 and dec_t <= DEC_ROWS and WIDTH == N_HEADS * HEAD_DIM

    pre_tile = next(t for t in (PRE_TILE, MOBA_BLOCK) if seq % t == 0)
    tabs_p = _rope_tables(0, seq)
    tabs_s = _rope_tables(past_len, DEC_ROWS)
    zero_state = jnp.zeros((1, bsz, HALO, WIDTH), F32)
    hist_a, hist_c, hist_d = _pad_state(state_conv_a), _pad_state(state_conv_c), _pad_state(state_pool_d)
    fng = final_norm_g.reshape(1, d_model)
    yp = x_prompt
    ys = jnp.pad(x_sample, ((0, 0), (0, DEC_ROWS - dec_t), (0, 0)))
    outs = [[] for _ in range(8)]
    kstack = jnp.zeros((depth, bsz, WIDTH, seq), F32)
    vstack = jnp.zeros((depth, bsz, WIDTH, seq), F32)
    n_groups = w_pool.shape[1]
    eye = jnp.eye(n_groups, dtype=F32)
    pool_bd = (w_pool[:, :, :, None, :] * eye[None, :, None, :, None]).reshape(depth, WIDTH, WIDTH)
    row = lambda a: a.reshape(depth, 1, a.shape[-1])
    lw = {"norm_g": row(norm_g), "w_in": w_in.astype(BF16), "conv_a_w": conv_a_w, "conv_c_w": conv_c_w,
          "conv_c_b": row(conv_c_b), "ln_c_g": row(ln_c_g), "ln_c_b": row(ln_c_b),
          "w_pw_c": w_pw_c.astype(BF16), "w_pool": pool_bd.astype(BF16), "pool_scale": row(pool_scale)}
    wo = w_out.astype(BF16)
    for l in range(depth):
        final = l == depth - 1

        mixed, kstack, vstack, qt, kb, vt, km, ap, cp, dp = _pre_call(
            yp, lw, l, tabs_p, zero_state, zero_state, zero_state, 0, (kstack, vstack),
            ns=1, tm=pre_tile, last_rows=pre_tile, pos0=0, prompt=True)
        yp = _attn_call(qt, kb, vt, km, mixed, yp, wo, l, fng, final=final)

        mixed_s, ks, vs, qs, as_, cs, ds = _pre_call(
            ys, lw, l, tabs_s, hist_a, hist_c, hist_d, l,
            ns=dec_b, tm=DEC_ROWS, last_rows=dec_t, pos0=past_len, prompt=False)
        yb_s = _dec_call(page_table, cache_k, cache_v, l, qs, ks, vs, t_valid=dec_t)
        ys = _proj_call(yb_s, mixed_s, ys, wo, l, fng, final=final)

        for lst, val in zip(outs, (ks[:, :dec_t].reshape(dec_b, dec_t, N_HEADS, HEAD_DIM),
                                   vs[:, :dec_t].reshape(dec_b, dec_t, N_HEADS, HEAD_DIM),
                                   ap, as_, cp, cs, dp, ds)):
            lst.append(val)
    to_rows = lambda st: st.reshape(depth, bsz, N_HEADS, HEAD_DIM, seq).transpose(0, 1, 4, 2, 3)
    return (yp, ys[:, :dec_t], to_rows(kstack), to_rows(vstack)) + tuple(jnp.stack(o) for o in outs)
```

```python
import functools

import jax
import jax.numpy as jnp
from jax import lax
from jax.experimental import pallas as pl
from jax.experimental.pallas import tpu as pltpu

F32 = jnp.float32
BF16 = jnp.bfloat16

WIDTH = 256
N_HEADS = 4
HEAD_DIM = 64
ROT_HALF = 8
ROPE_THETA = 500000.0
MOBA_BLOCK = 256
MOBA_TOPK = 3
CONV_A_WIDTH = 3
CONV_C_WIDTH = 31
POOL_WINDOWS = (2, 4, 8, 16)
POOL_STATE = 15
RMS_EPS = 1e-6
LN_EPS = 1e-5
HALO = 32
DEC_ROWS = 8
ATTN_GROUP = 4
DEC_SLOTS = 3
PRE_TILE = 1024
NEG_INF = float("-inf")
LOG2_E = 1.4426950408889634
LANES = 128
V7X_VMEM_BYTES = 64 * 1024 * 1024
VMEM_LIMIT = V7X_VMEM_BYTES - 8 * 1024 * 1024


def _silu(z):
    return z * jax.nn.sigmoid(z)


def _rmsnorm(x, g):
    ms = jnp.mean(x * x, axis=-1, keepdims=True)
    return x * lax.rsqrt(ms + RMS_EPS) * g


def _out_proj(x, mixed, y_b, wout_ref):
    yb = (y_b * mixed[:, WIDTH:2 * WIDTH]).astype(BF16)
    o = x + jnp.dot(mixed[:, 0:WIDTH].astype(BF16), wout_ref[0:WIDTH, :], preferred_element_type=F32)
    o = o + jnp.dot(yb, wout_ref[WIDTH:2 * WIDTH, :], preferred_element_type=F32)
    o = o + jnp.dot(mixed[:, 2 * WIDTH:4 * WIDTH].astype(BF16), wout_ref[2 * WIDTH:4 * WIDTH, :],
                    preferred_element_type=F32)
    return o


def _pre_body(x_ref, ng_ref, win_ref, caw_ref, ccw_ref, ccb_ref, lng_ref, lnb_ref, wpw_ref, wpool_ref,
              psc_ref, rc_ref, rsa_ref, rsb_ref, prea_ref, prec_ref, pred_ref, *refs,
              ns, tm, last_rows, pos0, prompt):
    if prompt:
        (_, _, mixed_ref, kst_ref, vst_ref, qt_ref, kb_ref, vt_ref, km_ref,
         na_ref, nc_ref, nd_ref, bufa, bufc, bufd, shc, x2, x4, x8) = refs
    else:
        (mixed_ref, k_ref, v_ref, q_ref, na_ref, nc_ref, nd_ref, bufa, bufc, bufd, shc, x2, x4, x8) = refs
    t = pl.program_id(1)
    nt = pl.num_programs(1)
    span = tm + HALO - 8
    n = ns * tm
    seq = lambda z: z.reshape(ns, tm, WIDTH)
    flat = lambda z: z.reshape(n, WIDTH)

    @pl.when(t == 0)
    def _():
        bufa[:, 0:HALO, :] = prea_ref[...]
        bufc[:, 0:HALO, :] = prec_ref[...]
        bufd[:, 0:HALO, :] = pred_ref[...]

    x = x_ref[...].reshape(n, x_ref.shape[-1])
    h = _rmsnorm(x, ng_ref[...]).astype(BF16)

    def proj(c0, cn):
        return jnp.dot(h, win_ref[:, c0 * WIDTH:(c0 + cn) * WIDTH], preferred_element_type=F32)

    def put(col, val):
        mixed_ref[:, :, col * WIDTH:(col + 1) * WIDTH] = seq(val)

    pa = proj(0, 4)
    bufa[:, HALO:HALO + tm, :] = seq(pa[:, 2 * WIDTH:3 * WIDTH] * pa[:, 0:WIDTH])
    conv = None
    for j in range(CONV_A_WIDTH):
        off = HALO - (CONV_A_WIDTH - 1) + j
        term = caw_ref[j:j + 1, :] * bufa[:, off:off + tm, :]
        conv = term if conv is None else conv + term
    put(0, pa[:, WIDTH:2 * WIDTH] * flat(conv) * _silu(pa[:, 3 * WIDTH:4 * WIDTH]))

    pb = proj(4, 4)
    tab = lambda r: flat(jnp.broadcast_to(r[...][None], (ns, tm, WIDTH)))
    rc, rsa, rsb = tab(rc_ref), tab(rsa_ref), tab(rsb_ref)

    def rope(z):
        return z * rc + pltpu.roll(z, WIDTH - ROT_HALF, 1) * rsa + pltpu.roll(z, ROT_HALF, 1) * rsb

    qs = rope(pb[:, 0:WIDTH]) * (HEAD_DIM ** -0.5)
    kr = rope(pb[:, WIDTH:2 * WIDTH])
    v = pb[:, 2 * WIDTH:3 * WIDTH]
    put(1, _silu(pb[:, 3 * WIDTH:4 * WIDTH]))
    if prompt:
        kst_ref[0] = kr.T
        v_t = v.T
        vst_ref[0] = v_t
        for r in range(tm // MOBA_BLOCK):
            blk = slice(r * MOBA_BLOCK, (r + 1) * MOBA_BLOCK)
            qt_ref[0, r] = qs[blk, :].T
            kb_ref[0, r] = kr[blk, :].astype(BF16)
            vt_ref[0, r] = v_t[:, blk].astype(BF16)
            km_ref[0, r] = jnp.mean(kr[blk, :], axis=0, keepdims=True)
    else:
        k_ref[...] = seq(kr)
        v_ref[...] = seq(v)
        q_ref[...] = seq(qs)

    pc = proj(8, 3)
    bufc[:, HALO:HALO + tm, :] = seq(pc[:, 0:WIDTH] * jax.nn.sigmoid(pc[:, WIDTH:2 * WIDTH]))
    for r in range(1, 8):
        shc[r - 1] = bufc[:, r:r + span, :]
    acc = None
    for j in range(CONV_C_WIDTH):
        a, r = divmod(HALO - (CONV_C_WIDTH - 1) + j, 8)
        rows = bufc[:, 8 * a:8 * a + tm, :] if r == 0 else shc[r - 1, :, 8 * a:8 * a + tm, :]
        term = ccw_ref[j:j + 1, :] * rows
        acc = term if acc is None else acc + term
    acc = flat(acc) + ccb_ref[...]
    mu = jnp.mean(acc, axis=-1, keepdims=True)
    cen = acc - mu
    var = jnp.mean(cen * cen, axis=-1, keepdims=True)
    ln = cen * lax.rsqrt(var + LN_EPS) * lng_ref[...] + lnb_ref[...]
    yc = jnp.dot(_silu(ln).astype(BF16), wpw_ref[...], preferred_element_type=F32)
    put(2, yc * _silu(pc[:, 2 * WIDTH:3 * WIDTH]))

    pd = proj(11, 2)
    dx = pd[:, 0:WIDTH]
    bufd[:, HALO:HALO + tm, :] = seq(dx)
    assert POOL_WINDOWS == (2, 4, 8, 16) and HALO == 32
    x2[...] = bufd[:, 8:8 + span, :] + bufd[:, 7:7 + span, :]
    x4[:, 0:span - 8, :] = x2[:, 8:span, :] + x2[:, 6:span - 2, :]
    x8[:, 0:span - 16, :] = x4[:, 8:span - 8, :] + x4[:, 4:span - 12, :]
    sums = {2: x2[:, 24:24 + tm, :], 4: x4[:, 16:16 + tm, :], 8: x8[:, 8:8 + tm, :],
            16: x8[:, 8:8 + tm, :] + x8[:, 0:tm, :]}
    group = lax.broadcasted_iota(jnp.int32, (ns, tm, WIDTH), 2) // (WIDTH // len(POOL_WINDOWS))
    pos = pos0 + t * tm + lax.broadcasted_iota(jnp.int32, (ns, tm, WIDTH), 1)
    wsum = sums[POOL_WINDOWS[-1]]
    win = jnp.full((ns, tm, WIDTH), POOL_WINDOWS[-1], jnp.int32)
    for g in range(len(POOL_WINDOWS) - 2, -1, -1):
        wsum = jnp.where(group == g, sums[POOL_WINDOWS[g]], wsum)
        win = jnp.where(group == g, POOL_WINDOWS[g], win)
    cnt = jnp.minimum(pos + 1, win).astype(F32)
    diff = flat(wsum / cnt) - dx
    yd = jnp.dot(diff.astype(BF16), wpool_ref[...], preferred_element_type=F32) * psc_ref[...]
    put(3, yd * _silu(pd[:, WIDTH:2 * WIDTH]))

    @pl.when(t == nt - 1)
    def _():
        end = HALO + last_rows
        na_ref[...] = bufa[:, end - (CONV_A_WIDTH - 1):end, :]
        nc_ref[...] = bufc[:, end - (CONV_C_WIDTH - 1):end, :]
        nd_ref[...] = bufd[:, end - POOL_STATE:end, :]

    if tm >= HALO:
        @pl.when(t < nt - 1)
        def _():
            bufa[:, 0:HALO, :] = bufa[:, tm:tm + HALO, :]
            bufc[:, 0:HALO, :] = bufc[:, tm:tm + HALO, :]
            bufd[:, 0:HALO, :] = bufd[:, tm:tm + HALO, :]


def _pre_call(x, lw, layer, rope_tabs, pre_a, pre_c, pre_d, state_layer, kv_stacks=None, *,
              ns, tm, last_rows, pos0, prompt):
    bsz, t_len, d_model = x.shape
    nt = t_len // tm
    assert nt * tm == t_len and (nt == 1 or tm >= HALO) and bsz % ns == 0 and tm % 8 == 0
    rc, rsa, rsb = rope_tabs
    d_in = lw["w_in"].shape[-1]

    def vec(n):
        return pl.BlockSpec((None, 1, n), lambda b, t: (layer, 0, 0))

    def mat(r, c):
        return pl.BlockSpec((None, r, c), lambda b, t: (layer, 0, 0))

    def per_b(shape):
        return pl.BlockSpec((ns,) + shape, lambda b, t: (b,) + (0,) * len(shape))

    hist = pl.BlockSpec((None, ns, HALO, WIDTH), lambda b, t: (state_layer, b, 0, 0))
    tok = lambda w: pl.BlockSpec((ns, tm, w), lambda b, t: (b, t, 0))
    tab = pl.BlockSpec((tm, WIDTH), lambda b, t: (t, 0))
    in_specs = [tok(d_model), vec(d_model), mat(d_model, d_in),
                mat(CONV_A_WIDTH, WIDTH), mat(CONV_C_WIDTH, WIDTH), vec(WIDTH),
                vec(WIDTH), vec(WIDTH), mat(WIDTH, WIDTH), mat(WIDTH, WIDTH),
                vec(WIDTH), tab, tab, tab, hist, hist, hist]
    out_shape = [jax.ShapeDtypeStruct((bsz, t_len, 4 * WIDTH), F32),
                 jax.ShapeDtypeStruct((bsz, t_len, WIDTH), F32),
                 jax.ShapeDtypeStruct((bsz, t_len, WIDTH), F32)]
    out_specs = [tok(4 * WIDTH), tok(WIDTH), tok(WIDTH)]
    operands = [x, lw["norm_g"], lw["w_in"], lw["conv_a_w"], lw["conv_c_w"], lw["conv_c_b"], lw["ln_c_g"],
                lw["ln_c_b"], lw["w_pw_c"], lw["w_pool"], lw["pool_scale"], rc, rsa, rsb, pre_a, pre_c, pre_d]
    aliases = {}
    if prompt:
        stack_spec = pl.BlockSpec((None, 1, WIDTH, tm), lambda b, t: (layer, b, 0, t))
        for pos, stack in enumerate(kv_stacks):
            aliases[len(operands)] = 1 + pos
            operands.append(stack)
            in_specs.append(pl.BlockSpec(memory_space=pl.ANY))
            out_shape[1 + pos] = jax.ShapeDtypeStruct(stack.shape, F32)
            out_specs[1 + pos] = stack_spec
        assert tm % MOBA_BLOCK == 0 and ns == 1
        per_tile = tm // MOBA_BLOCK
        nblk = t_len // MOBA_BLOCK
        blk = lambda r, c: pl.BlockSpec((1, per_tile, r, c), lambda b, t: (b, t, 0, 0))
        out_shape += [jax.ShapeDtypeStruct((bsz, nblk, WIDTH, MOBA_BLOCK), F32),
                      jax.ShapeDtypeStruct((bsz, nblk, MOBA_BLOCK, WIDTH), BF16),
                      jax.ShapeDtypeStruct((bsz, nblk, WIDTH, MOBA_BLOCK), BF16),
                      jax.ShapeDtypeStruct((bsz, nblk, 1, WIDTH), F32)]
        out_specs += [blk(WIDTH, MOBA_BLOCK), blk(MOBA_BLOCK, WIDTH), blk(WIDTH, MOBA_BLOCK), blk(1, WIDTH)]
    else:
        out_shape += [jax.ShapeDtypeStruct((bsz, t_len, WIDTH), F32)]
        out_specs += [tok(WIDTH)]
    out_shape += [jax.ShapeDtypeStruct((bsz, CONV_A_WIDTH - 1, WIDTH), F32),
                  jax.ShapeDtypeStruct((bsz, CONV_C_WIDTH - 1, WIDTH), F32),
                  jax.ShapeDtypeStruct((bsz, POOL_STATE, WIDTH), F32)]
    out_specs += [per_b((CONV_A_WIDTH - 1, WIDTH)), per_b((CONV_C_WIDTH - 1, WIDTH)),
                  per_b((POOL_STATE, WIDTH))]
    body = functools.partial(_pre_body, ns=ns, tm=tm, last_rows=last_rows, pos0=pos0, prompt=prompt)
    return pl.pallas_call(
        body,
        grid=(bsz // ns, nt),
        in_specs=in_specs,
        out_specs=out_specs,
        out_shape=out_shape,
        scratch_shapes=[pltpu.VMEM((ns, HALO + tm, WIDTH), F32)] * 3
        + [pltpu.VMEM((7, ns, HALO + tm - 8, WIDTH), F32)] + [pltpu.VMEM((ns, HALO + tm - 8, WIDTH), F32)] * 3,
        input_output_aliases=aliases,
        compiler_params=pltpu.CompilerParams(dimension_semantics=("arbitrary", "arbitrary"),
                                             vmem_limit_bytes=VMEM_LIMIT),
        name="pre_prompt" if prompt else "pre_sample",
    )(*operands)


def _top_blocks_bias(gate, index, n_index, axis):
    bias = jnp.full(gate.shape, NEG_INF, F32)
    for _ in range(MOBA_TOPK):
        top = jnp.max(gate, axis=axis, keepdims=True)
        cand = jnp.logical_and(gate == top, top > NEG_INF)
        first = jnp.min(jnp.where(cand, index, n_index), axis=axis, keepdims=True)
        pick = index == first
        bias = jnp.where(pick, 0.0, bias)
        gate = jnp.where(pick, NEG_INF, gate)
    return bias


def _attn_body(qt_ref, kb_ref, vt_ref, km_ref, mixed_ref, x_ref, wout_ref, fng_ref, out_ref,
               qm_s, bias_s, acc_s, s_a, s_b, p_a, p_b, al_a, al_b, *, nblk, tq, grp, final):
    s_bufs, p_bufs, al_bufs = (s_a, s_b), (p_a, p_b), (al_a, al_b)
    w4 = N_HEADS * tq
    i = pl.program_id(1)
    qt = qt_ref[0, 0]
    row_head = lax.broadcasted_iota(jnp.int32, (WIDTH, tq), 0) // HEAD_DIM
    km = km_ref[0]
    lane_head = lax.broadcasted_iota(jnp.int32, (nblk, WIDTH), 1) // HEAD_DIM
    blk = lax.broadcasted_iota(jnp.int32, (nblk, tq), 0)
    km_heads = jnp.concatenate([jnp.where(lane_head == h, km, 0.0) for h in range(N_HEADS)], axis=0)
    gates = jnp.dot(km_heads, qt, precision=lax.Precision.HIGHEST, preferred_element_type=F32)
    for h in range(N_HEADS):
        cols = slice(h * tq, (h + 1) * tq)
        qm_s[:, cols] = jnp.where(row_head == h, qt * LOG2_E, 0.0).astype(BF16)
        gate = jnp.where(blk < i, gates[h * nblk:(h + 1) * nblk, :], NEG_INF)
        bias_s[:, cols] = _top_blocks_bias(gate, blk, nblk, 0)

    n_groups = nblk // grp
    sub = [slice(c * MOBA_BLOCK, (c + 1) * MOBA_BLOCK) for c in range(grp)]

    def qk(group, slot):
        kb = kb_ref[0, pl.ds(group * grp, grp)].reshape(grp * MOBA_BLOCK, WIDTH)
        s_bufs[slot][:, 0:w4] = jnp.dot(kb, qm_s[...], preferred_element_type=F32)

    def pv(group, slot):
        for h in range(N_HEADS):
            rows = slice(h * HEAD_DIM, (h + 1) * HEAD_DIM)
            cols = slice(h * tq, (h + 1) * tq)
            vt = jnp.concatenate([vt_ref[0, group * grp + c, rows, :] for c in range(grp)], axis=1)
            o = jnp.dot(vt, p_bufs[slot][:, cols], preferred_element_type=F32)
            acc_s[rows, :] = al_bufs[slot][:, cols] * acc_s[rows, :] + o

    qk(0, 1)
    key = lax.broadcasted_iota(jnp.int32, (MOBA_BLOCK, w4), 0)
    qry = lax.broadcasted_iota(jnp.int32, (MOBA_BLOCK, w4), 1) % tq
    s = jnp.dot(kb_ref[0, i], qm_s[...], preferred_element_type=F32)
    s = jnp.where(key <= qry, s, NEG_INF)
    m = jnp.max(s, axis=0, keepdims=True)
    p = jnp.exp2(s - m)
    l = jnp.sum(p, axis=0, keepdims=True)
    pb = p.astype(BF16)
    for h in range(N_HEADS):
        rows = slice(h * HEAD_DIM, (h + 1) * HEAD_DIM)
        acc_s[rows, :] = jnp.dot(vt_ref[0, i, rows, :], pb[:, h * tq:(h + 1) * tq], preferred_element_type=F32)
    p_bufs[0][:, 0:w4] = jnp.zeros((grp * MOBA_BLOCK, w4), BF16)
    al_bufs[0][...] = jnp.ones((1, w4), F32)

    def half(n, m, l, cur, nxt):
        picked = [bias_s[pl.ds((n - 1) * grp + c, 1), :] for c in range(grp)]
        m_new = m
        for c in range(grp):
            m_new = jnp.maximum(m_new, jnp.max(s_bufs[cur][sub[c], 0:w4], axis=0, keepdims=True) + picked[c])
        alpha = jnp.exp2(m - m_new)
        l = alpha * l
        for c in range(grp):
            p = jnp.exp2(s_bufs[cur][sub[c], 0:w4] - (m_new - picked[c]))
            l = l + jnp.sum(p, axis=0, keepdims=True)
            p_bufs[cur][sub[c], 0:w4] = p.astype(BF16)
        al_bufs[cur][...] = alpha
        pv(jnp.maximum(n - 2, 0), nxt)
        qk(jnp.minimum(n, n_groups - 1), nxt)
        return m_new, l

    def visit(n, carry):
        return lax.cond(n % 2 == 1, lambda: half(n, carry[0], carry[1], 1, 0),
                        lambda: half(n, carry[0], carry[1], 0, 1))

    n_visits = (i + grp - 1) // grp
    m, l = lax.fori_loop(1, n_visits + 1, visit, (m, l))
    odd = n_visits % 2 == 1

    @pl.when(odd)
    def _():
        pv(n_visits - 1, 1)

    @pl.when(jnp.logical_not(odd))
    def _():
        pv(jnp.maximum(n_visits - 1, 0), 0)

    inv = 1.0 / l
    for h in range(N_HEADS):
        rows = slice(h * HEAD_DIM, (h + 1) * HEAD_DIM)
        acc_s[rows, :] = acc_s[rows, :] * inv[:, h * tq:(h + 1) * tq]
    y_b = acc_s[...].T
    o = _out_proj(x_ref[0], mixed_ref[0], y_b, wout_ref)
    if final:
        o = _rmsnorm(o, fng_ref[...])
    out_ref[0] = o


def _attn_call(qt, kb, vt, km, mixed, x, w_out, layer, fng, *, final):
    bsz, nblk, _, tq = qt.shape
    d_model = x.shape[-1]
    assert tq == MOBA_BLOCK
    grp = next(g for g in (ATTN_GROUP, 2, 1) if nblk % g == 0)
    body = functools.partial(_attn_body, nblk=nblk, tq=tq, grp=grp, final=final)
    whole = lambda r, c: pl.BlockSpec((1, nblk, r, c), lambda b, i: (b, 0, 0, 0))
    return pl.pallas_call(
        body,
        grid=(bsz, nblk),
        in_specs=[pl.BlockSpec((1, 1, WIDTH, tq), lambda b, i: (b, i, 0, 0)),
                  whole(MOBA_BLOCK, WIDTH), whole(WIDTH, MOBA_BLOCK),
                  pl.BlockSpec((1, nblk, WIDTH), lambda b, i: (b, 0, 0)),
                  pl.BlockSpec((1, tq, 4 * WIDTH), lambda b, i: (b, i, 0)),
                  pl.BlockSpec((1, tq, d_model), lambda b, i: (b, i, 0)),
                  pl.BlockSpec((None, 4 * WIDTH, d_model), lambda b, i: (layer, 0, 0)),
                  pl.BlockSpec((1, d_model), lambda b, i: (0, 0))],
        out_specs=pl.BlockSpec((1, tq, d_model), lambda b, i: (b, i, 0)),
        out_shape=jax.ShapeDtypeStruct(x.shape, F32),
        scratch_shapes=[pltpu.VMEM((WIDTH, N_HEADS * tq), BF16),
                        pltpu.VMEM((nblk, N_HEADS * tq), F32),
                        pltpu.VMEM((WIDTH, tq), F32)]
        + [pltpu.VMEM((grp * MOBA_BLOCK, N_HEADS * tq + LANES), F32)] * 2
        + [pltpu.VMEM((grp * MOBA_BLOCK, N_HEADS * tq + LANES), BF16)] * 2
        + [pltpu.VMEM((1, N_HEADS * tq), F32)] * 2,
        compiler_params=pltpu.CompilerParams(dimension_semantics=("arbitrary", "arbitrary"),
                                             vmem_limit_bytes=VMEM_LIMIT),
        name="attn_prompt",
    )(qt, kb, vt, km.reshape(bsz, nblk, WIDTH), mixed, x, w_out, fng)


def _dec_body(pt_ref, *refs, layer, n_seq, pps, ns, t_valid):
    (ck_ref, cv_ref, q_ref, kn_ref, vn_ref, yb_ref,
     qm_s, mp_s, lp_s, gp_s, op_s, pbd_s, kbuf, vbuf, sems) = refs
    step = pl.program_id(1)
    rows = N_HEADS * DEC_ROWS
    nb = pps // 2
    lane_head = lax.broadcasted_iota(jnp.int32, (DEC_ROWS, WIDTH), 1) // HEAD_DIM
    lane = lax.broadcasted_iota(jnp.int32, (rows, LANES), 1)

    g = pl.program_id(0) * ns + step
    n_steps = pl.num_programs(0) * ns

    def page_copies(gg, page_of):
        slot = gg % DEC_SLOTS
        for c in range(pps):
            page = page_of(c)
            yield pltpu.make_async_copy(ck_ref.at[layer, page], kbuf.at[slot, c], sems.at[0, slot])
            yield pltpu.make_async_copy(cv_ref.at[layer, page], vbuf.at[slot, c], sems.at[1, slot])

    def request(gg):
        seq_id, seq_step = gg // ns, gg % ns
        for copy in page_copies(gg, lambda c: pt_ref[seq_id, seq_step * pps + c]):
            copy.start()

    @pl.when(g == 0)
    def _():
        pbd_s[...] = jnp.zeros(pbd_s.shape, BF16)
        for ahead in range(min(DEC_SLOTS - 1, n_seq * ns)):
            request(ahead)

    @pl.when(g + (DEC_SLOTS - 1) < n_steps)
    def _():
        request(g + (DEC_SLOTS - 1))

    for copy in page_copies(g, lambda c: 0):
        copy.wait()
    slot = g % DEC_SLOTS

    @pl.when(step == 0)
    def _():
        q8 = q_ref[0]
        for h in range(N_HEADS):
            qm_s[h * DEC_ROWS:(h + 1) * DEC_ROWS, :] = jnp.where(lane_head == h, q8, 0.0)
        mp_s[...] = jnp.full((rows, LANES), NEG_INF, F32)
        gp_s[...] = jnp.full((rows, LANES), NEG_INF, F32)
        lp_s[...] = jnp.zeros((rows, LANES), F32)

    qmb = qm_s[...].astype(BF16)
    contract_last = (((1,), (1,)), ((), ()))
    kt = jnp.concatenate([kbuf[slot, c] for c in range(pps)], axis=1).astype(BF16)
    vt = jnp.concatenate([vbuf[slot, c] for c in range(pps)], axis=1).astype(BF16)
    s_all = jnp.dot(qmb, kt, preferred_element_type=F32)
    for c in range(nb):
        s = s_all[:, c * MOBA_BLOCK:(c + 1) * MOBA_BLOCK]
        gate = jnp.sum(s, axis=1, keepdims=True) * (1.0 / MOBA_BLOCK)
        m = jnp.max(s, axis=1, keepdims=True)
        p = jnp.exp(s - m)
        l = jnp.sum(p, axis=1, keepdims=True)
        j = step * nb + c
        mp_s[...] = jnp.where(lane == j, m, mp_s[...])
        lp_s[...] = jnp.where(lane == j, l, lp_s[...])
        gp_s[...] = jnp.where(lane == j, gate, gp_s[...])
        pbd_s[c * rows:(c + 1) * rows, c * MOBA_BLOCK:(c + 1) * MOBA_BLOCK] = p.astype(BF16)
    op_s[pl.ds(pl.multiple_of(step * (nb * rows), nb * rows), nb * rows), :] = lax.dot_general(
        pbd_s[...], vt, contract_last, preferred_element_type=F32)

    @pl.when(step == ns - 1)
    def _():
        nblk = ns * nb
        sel = _top_blocks_bias(gp_s[...], lane, LANES, 1) == 0.0
        zpad = jnp.zeros((LANES - DEC_ROWS, WIDTH), F32)
        kn = jnp.concatenate([kn_ref[0], zpad], axis=0).astype(BF16)
        vn = jnp.concatenate([vn_ref[0], zpad], axis=0).astype(BF16)
        s = lax.dot_general(qmb, kn, contract_last, preferred_element_type=F32)
        qpos = jnp.minimum(lax.broadcasted_iota(jnp.int32, (rows, LANES), 0) % DEC_ROWS, t_valid - 1)
        s = jnp.where(lane <= qpos, s, NEG_INF)
        m_own = jnp.max(s, axis=1, keepdims=True)
        p = jnp.exp(s - m_own)
        l_own = jnp.sum(p, axis=1, keepdims=True)
        o_own = jnp.dot(p.astype(BF16), vn, preferred_element_type=F32)
        mp = mp_s[...]
        m_fin = jnp.maximum(m_own, jnp.max(jnp.where(sel, mp, NEG_INF), axis=1, keepdims=True))
        w = jnp.where(sel, jnp.exp(mp - m_fin), 0.0)
        a_own = jnp.exp(m_own - m_fin)
        l_fin = a_own * l_own + jnp.sum(w * lp_s[...], axis=1, keepdims=True)
        o_fin = a_own * o_own
        for j in range(nblk):
            o_fin = o_fin + w[:, j:j + 1] * op_s[j * rows:(j + 1) * rows, :]
        y = o_fin / l_fin
        y_b = None
        for h in range(N_HEADS):
            part = jnp.where(lane_head == h, y[h * DEC_ROWS:(h + 1) * DEC_ROWS, :], 0.0)
            y_b = part if y_b is None else y_b + part
        yb_ref[0] = y_b


def _dec_call(page_table, cache_k, cache_v, layer, q, kn, vn, *, t_valid):
    bsz, n_pages = page_table.shape
    depth, n_pool, page, _, _ = cache_k.shape
    assert (n_pages * page) % MOBA_BLOCK == 0 and MOBA_BLOCK == 2 * page
    nblk = n_pages * page // MOBA_BLOCK
    assert MOBA_TOPK <= nblk <= LANES
    pps = next(c for c in (16, 8, 4, 2) if n_pages % c == 0)
    ns = n_pages // pps
    ck = cache_k.transpose(0, 1, 3, 4, 2).reshape(depth, n_pool, WIDTH, page)
    cv = cache_v.transpose(0, 1, 3, 4, 2).reshape(depth, n_pool, WIDTH, page)

    in_hbm = pl.BlockSpec(memory_space=pl.ANY)
    per_b = pl.BlockSpec((1, DEC_ROWS, WIDTH), lambda b, s, pt: (b, 0, 0))
    rows = N_HEADS * DEC_ROWS
    body = functools.partial(_dec_body, layer=layer, n_seq=bsz, pps=pps, ns=ns, t_valid=t_valid)
    grid_spec = pltpu.PrefetchScalarGridSpec(
        num_scalar_prefetch=1,
        grid=(bsz, ns),
        in_specs=[in_hbm, in_hbm, per_b, per_b, per_b],
        out_specs=per_b,
        scratch_shapes=[pltpu.VMEM((rows, WIDTH), F32), pltpu.VMEM((rows, LANES), F32),
                        pltpu.VMEM((rows, LANES), F32), pltpu.VMEM((rows, LANES), F32),
                        pltpu.VMEM((nblk * rows, WIDTH), F32),
                        pltpu.VMEM((pps // 2 * rows, pps * page), BF16),
                        pltpu.VMEM((DEC_SLOTS, pps, WIDTH, page), F32),
                        pltpu.VMEM((DEC_SLOTS, pps, WIDTH, page), F32),
                        pltpu.SemaphoreType.DMA((2, DEC_SLOTS))],
    )
    return pl.pallas_call(
        body,
        grid_spec=grid_spec,
        out_shape=jax.ShapeDtypeStruct((bsz, DEC_ROWS, WIDTH), F32),
        compiler_params=pltpu.CompilerParams(dimension_semantics=("arbitrary", "arbitrary"),
                                             vmem_limit_bytes=VMEM_LIMIT),
        name="attn_sample",
    )(page_table, ck, cv, q, kn, vn)


def _proj_body(yb_ref, mixed_ref, x_ref, wout_ref, fng_ref, out_ref, *, final):
    o = _out_proj(x_ref[...], mixed_ref[...], yb_ref[...], wout_ref)
    if final:
        o = _rmsnorm(o, fng_ref[...])
    out_ref[...] = o


def _proj_call(y_b, mixed, x, w_out, layer, fng, *, final):
    bsz, t_len, d_model = x.shape
    n = bsz * t_len
    whole = lambda r, c: pl.BlockSpec((r, c), lambda i: (0, 0))
    out = pl.pallas_call(
        functools.partial(_proj_body, final=final),
        grid=(1,),
        in_specs=[whole(n, WIDTH), whole(n, 4 * WIDTH), whole(n, d_model),
                  pl.BlockSpec((None, 4 * WIDTH, d_model), lambda i: (layer, 0, 0)),
                  whole(1, d_model)],
        out_specs=whole(n, d_model),
        out_shape=jax.ShapeDtypeStruct((n, d_model), F32),
        compiler_params=pltpu.CompilerParams(dimension_semantics=("arbitrary",), vmem_limit_bytes=VMEM_LIMIT),
        name="proj_sample",
    )(y_b.reshape(n, WIDTH), mixed.reshape(n, 4 * WIDTH), x.reshape(n, d_model), w_out, fng)
    return out.reshape(bsz, t_len, d_model)


def _rope_tables(pos0, n):
    freqs = ROPE_THETA ** (-jnp.arange(ROT_HALF, dtype=F32) * 2.0 / (2 * ROT_HALF))
    ang = (pos0 + jnp.arange(n, dtype=jnp.int32)).astype(F32)[:, None] * freqs[None, :]
    cos, sin = jnp.cos(ang), jnp.sin(ang)
    one = jnp.ones((n, HEAD_DIM - 2 * ROT_HALF), F32)
    zero8 = jnp.zeros((n, ROT_HALF), F32)
    zero_rest = jnp.zeros((n, HEAD_DIM - 2 * ROT_HALF), F32)
    rc = jnp.concatenate([cos, cos, one], axis=1)
    rsa = jnp.concatenate([-sin, zero8, zero_rest], axis=1)
    rsb = jnp.concatenate([zero8, sin, zero_rest], axis=1)
    return tuple(jnp.tile(a, (1, N_HEADS)) for a in (rc, rsa, rsb))


def _pad_state(state):
    return jnp.pad(state, ((0, 0), (0, 0), (HALO - state.shape[2], 0), (0, 0)))


def kernel(x_prompt, x_sample, cache_k, cache_v, page_table, state_conv_a, state_conv_c, state_pool_d,
           norm_g, w_in, conv_a_w, conv_c_w, conv_c_b, ln_c_g, ln_c_b, w_pw_c, w_pool, pool_scale, w_out,
           final_norm_g):
    depth = w_in.shape[0]
    bsz, seq, d_model = x_prompt.shape
    dec_b, dec_t, _ = x_sample.shape
    past_len = page_table.shape[1] * cache_k.shape[2]
    assert seq % MOBA_BLOCK == 0 and dec_t <= DEC_ROWS and WIDTH == N_HEADS * HEAD_DIM

    pre_tile = next(t for t in (PRE_TILE, MOBA_BLOCK) if seq % t == 0)
    tabs_p = _rope_tables(0, seq)
    tabs_s = _rope_tables(past_len, DEC_ROWS)
    zero_state = jnp.zeros((1, bsz, HALO, WIDTH), F32)
    hist_a, hist_c, hist_d = _pad_state(state_conv_a), _pad_state(state_conv_c), _pad_state(state_pool_d)
    fng = final_norm_g.reshape(1, d_model)
    yp = x_prompt
    ys = jnp.pad(x_sample, ((0, 0), (0, DEC_ROWS - dec_t), (0, 0)))
    outs = [[] for _ in range(8)]
    kstack = jnp.zeros((depth, bsz, WIDTH, seq), F32)
    vstack = jnp.zeros((depth, bsz, WIDTH, seq), F32)
    n_groups = w_pool.shape[1]
    eye = jnp.eye(n_groups, dtype=F32)
    pool_bd = (w_pool[:, :, :, None, :] * eye[None, :, None, :, None]).reshape(depth, WIDTH, WIDTH)
    row = lambda a: a.reshape(depth, 1, a.shape[-1])
    lw = {"norm_g": row(norm_g), "w_in": w_in.astype(BF16), "conv_a_w": conv_a_w, "conv_c_w": conv_c_w,
          "conv_c_b": row(conv_c_b), "ln_c_g": row(ln_c_g), "ln_c_b": row(ln_c_b),
          "w_pw_c": w_pw_c.astype(BF16), "w_pool": pool_bd.astype(BF16), "pool_scale": row(pool_scale)}
    wo = w_out.astype(BF16)
    for l in range(depth):
        final = l == depth - 1

        mixed, kstack, vstack, qt, kb, vt, km, ap, cp, dp = _pre_call(
            yp, lw, l, tabs_p, zero_state, zero_state, zero_state, 0, (kstack, vstack),
            ns=1, tm=pre_tile, last_rows=pre_tile, pos0=0, prompt=True)
        yp = _attn_call(qt, kb, vt, km, mixed, yp, wo, l, fng, final=final)

        mixed_s, ks, vs, qs, as_, cs, ds = _pre_call(
            ys, lw, l, tabs_s, hist_a, hist_c, hist_d, l,
            ns=dec_b, tm=DEC_ROWS, last_rows=dec_t, pos0=past_len, prompt=False)
        yb_s = _dec_call(page_table, cache_k, cache_v, l, qs, ks, vs, t_valid=dec_t)
        ys = _proj_call(yb_s, mixed_s, ys, wo, l, fng, final=final)

        for lst, val in zip(outs, (ks[:, :dec_t].reshape(dec_b, dec_t, N_HEADS, HEAD_DIM),
                                   vs[:, :dec_t].reshape(dec_b, dec_t, N_HEADS, HEAD_DIM),
                                   ap, as_, cp, cs, dp, ds)):
            lst.append(val)
    to_rows = lambda st: st.reshape(depth, bsz, N_HEADS, HEAD_DIM, seq).transpose(0, 1, 4, 2, 3)
    return (yp, ys[:, :dec_t], to_rows(kstack), to_rows(vstack)) + tuple(jnp.stack(o) for o in outs)
```

```python
import functools

import jax
import jax.numpy as jnp
from jax import lax
from jax.experimental import pallas as pl
from jax.experimental.pallas import tpu as pltpu

F32 = jnp.float32
BF16 = jnp.bfloat16

WIDTH = 256
N_HEADS = 4
HEAD_DIM = 64
ROT_HALF = 8
ROPE_THETA = 500000.0
MOBA_BLOCK = 256
MOBA_TOPK = 3
CONV_A_WIDTH = 3
CONV_C_WIDTH = 31
POOL_WINDOWS = (2, 4, 8, 16)
POOL_STATE = 15
RMS_EPS = 1e-6
LN_EPS = 1e-5
HALO = 32
DEC_ROWS = 8
ATTN_GROUP = 4
DEC_SLOTS = 4
PRE_TILE = 1024
NEG_INF = float("-inf")
LOG2_E = 1.4426950408889634
LANES = 128
V7X_VMEM_BYTES = 64 * 1024 * 1024
VMEM_LIMIT = V7X_VMEM_BYTES - 8 * 1024 * 1024


def _silu(z):
    return z * jax.nn.sigmoid(z)


def _rmsnorm(x, g):
    ms = jnp.mean(x * x, axis=-1, keepdims=True)
    return x * lax.rsqrt(ms + RMS_EPS) * g


def _out_proj(x, mixed, y_b, wout_ref):
    yb = (y_b * mixed[:, WIDTH:2 * WIDTH]).astype(BF16)
    o = x + jnp.dot(mixed[:, 0:WIDTH].astype(BF16), wout_ref[0:WIDTH, :], preferred_element_type=F32)
    o = o + jnp.dot(yb, wout_ref[WIDTH:2 * WIDTH, :], preferred_element_type=F32)
    o = o + jnp.dot(mixed[:, 2 * WIDTH:4 * WIDTH].astype(BF16), wout_ref[2 * WIDTH:4 * WIDTH, :],
                    preferred_element_type=F32)
    return o


def _pre_body(x_ref, ng_ref, win_ref, caw_ref, ccw_ref, ccb_ref, lng_ref, lnb_ref, wpw_ref, wpool_ref,
              psc_ref, rc_ref, rsa_ref, rsb_ref, prea_ref, prec_ref, pred_ref, *refs,
              ns, tm, last_rows, pos0, prompt):
    if prompt:
        (_, _, mixed_ref, kst_ref, vst_ref, qt_ref, kb_ref, vt_ref, km_ref,
         na_ref, nc_ref, nd_ref, bufa, bufc, bufd, shc, x2, x4, x8) = refs
    else:
        (mixed_ref, k_ref, v_ref, q_ref, na_ref, nc_ref, nd_ref, bufa, bufc, bufd, shc, x2, x4, x8) = refs
    t = pl.program_id(1)
    nt = pl.num_programs(1)
    span = tm + HALO - 8
    n = ns * tm
    seq = lambda z: z.reshape(ns, tm, WIDTH)
    flat = lambda z: z.reshape(n, WIDTH)

    @pl.when(t == 0)
    def _():
        bufa[:, 0:HALO, :] = prea_ref[...]
        bufc[:, 0:HALO, :] = prec_ref[...]
        bufd[:, 0:HALO, :] = pred_ref[...]

    x = x_ref[...].reshape(n, x_ref.shape[-1])
    h = _rmsnorm(x, ng_ref[...]).astype(BF16)

    def proj(c0, cn):
        return jnp.dot(h, win_ref[:, c0 * WIDTH:(c0 + cn) * WIDTH], preferred_element_type=F32)

    def put(col, val):
        mixed_ref[:, :, col * WIDTH:(col + 1) * WIDTH] = seq(val)

    pa = proj(0, 4)
    bufa[:, HALO:HALO + tm, :] = seq(pa[:, 2 * WIDTH:3 * WIDTH] * pa[:, 0:WIDTH])
    conv = None
    for j in range(CONV_A_WIDTH):
        off = HALO - (CONV_A_WIDTH - 1) + j
        term = caw_ref[j:j + 1, :] * bufa[:, off:off + tm, :]
        conv = term if conv is None else conv + term
    put(0, pa[:, WIDTH:2 * WIDTH] * flat(conv) * _silu(pa[:, 3 * WIDTH:4 * WIDTH]))

    pb = proj(4, 4)
    tab = lambda r: flat(jnp.broadcast_to(r[...][None], (ns, tm, WIDTH)))
    rc, rsa, rsb = tab(rc_ref), tab(rsa_ref), tab(rsb_ref)

    def rope(z):
        return z * rc + pltpu.roll(z, WIDTH - ROT_HALF, 1) * rsa + pltpu.roll(z, ROT_HALF, 1) * rsb

    qs = rope(pb[:, 0:WIDTH]) * (HEAD_DIM ** -0.5)
    kr = rope(pb[:, WIDTH:2 * WIDTH])
    v = pb[:, 2 * WIDTH:3 * WIDTH]
    put(1, _silu(pb[:, 3 * WIDTH:4 * WIDTH]))
    if prompt:
        kst_ref[0] = kr.T
        v_t = v.T
        vst_ref[0] = v_t
        for r in range(tm // MOBA_BLOCK):
            blk = slice(r * MOBA_BLOCK, (r + 1) * MOBA_BLOCK)
            qt_ref[0, r] = qs[blk, :].T
            kb_ref[0, r] = kr[blk, :].astype(BF16)
            vt_ref[0, r] = v_t[:, blk].astype(BF16)
            km_ref[0, r] = jnp.mean(kr[blk, :], axis=0, keepdims=True)
    else:
        k_ref[...] = seq(kr)
        v_ref[...] = seq(v)
        q_ref[...] = seq(qs)

    pc = proj(8, 3)
    bufc[:, HALO:HALO + tm, :] = seq(pc[:, 0:WIDTH] * jax.nn.sigmoid(pc[:, WIDTH:2 * WIDTH]))
    for r in range(1, 8):
        shc[r - 1, :, :, 0:WIDTH] = bufc[:, r:r + span, :]
    acc = None
    for j in range(CONV_C_WIDTH):
        a, r = divmod(HALO - (CONV_C_WIDTH - 1) + j, 8)
        rows = bufc[:, 8 * a:8 * a + tm, :] if r == 0 else shc[r - 1, :, 8 * a:8 * a + tm, 0:WIDTH]
        term = ccw_ref[j:j + 1, :] * rows
        acc = term if acc is None else acc + term
    acc = flat(acc) + ccb_ref[...]
    mu = jnp.mean(acc, axis=-1, keepdims=True)
    cen = acc - mu
    var = jnp.mean(cen * cen, axis=-1, keepdims=True)
    ln = cen * lax.rsqrt(var + LN_EPS) * lng_ref[...] + lnb_ref[...]
    yc = jnp.dot(_silu(ln).astype(BF16), wpw_ref[...], preferred_element_type=F32)
    put(2, yc * _silu(pc[:, 2 * WIDTH:3 * WIDTH]))

    pd = proj(11, 2)
    dx = pd[:, 0:WIDTH]
    bufd[:, HALO:HALO + tm, :] = seq(dx)
    assert POOL_WINDOWS == (2, 4, 8, 16) and HALO == 32
    x2[...] = bufd[:, 8:8 + span, :] + bufd[:, 7:7 + span, :]
    x4[:, 0:span - 8, :] = x2[:, 8:span, :] + x2[:, 6:span - 2, :]
    x8[:, 0:span - 16, :] = x4[:, 8:span - 8, :] + x4[:, 4:span - 12, :]
    sums = {2: x2[:, 24:24 + tm, :], 4: x4[:, 16:16 + tm, :], 8: x8[:, 8:8 + tm, :],
            16: x8[:, 8:8 + tm, :] + x8[:, 0:tm, :]}
    group = lax.broadcasted_iota(jnp.int32, (ns, tm, WIDTH), 2) // (WIDTH // len(POOL_WINDOWS))
    pos = pos0 + t * tm + lax.broadcasted_iota(jnp.int32, (ns, tm, WIDTH), 1)
    wsum = sums[POOL_WINDOWS[-1]]
    win = jnp.full((ns, tm, WIDTH), POOL_WINDOWS[-1], jnp.int32)
    for g in range(len(POOL_WINDOWS) - 2, -1, -1):
        wsum = jnp.where(group == g, sums[POOL_WINDOWS[g]], wsum)
        win = jnp.where(group == g, POOL_WINDOWS[g], win)
    cnt = jnp.minimum(pos + 1, win).astype(F32)
    diff = flat(wsum / cnt) - dx
    yd = jnp.dot(diff.astype(BF16), wpool_ref[...], preferred_element_type=F32) * psc_ref[...]
    put(3, yd * _silu(pd[:, WIDTH:2 * WIDTH]))

    @pl.when(t == nt - 1)
    def _():
        end = HALO + last_rows
        na_ref[...] = bufa[:, end - (CONV_A_WIDTH - 1):end, :]
        nc_ref[...] = bufc[:, end - (CONV_C_WIDTH - 1):end, :]
        nd_ref[...] = bufd[:, end - POOL_STATE:end, :]

    if tm >= HALO:
        @pl.when(t < nt - 1)
        def _():
            bufa[:, 0:HALO, :] = bufa[:, tm:tm + HALO, :]
            bufc[:, 0:HALO, :] = bufc[:, tm:tm + HALO, :]
            bufd[:, 0:HALO, :] = bufd[:, tm:tm + HALO, :]


def _pre_call(x, lw, layer, rope_tabs, pre_a, pre_c, pre_d, state_layer, kv_stacks=None, *,
              ns, tm, last_rows, pos0, prompt):
    bsz, t_len, d_model = x.shape
    nt = t_len // tm
    assert nt * tm == t_len and (nt == 1 or tm >= HALO) and bsz % ns == 0 and tm % 8 == 0
    rc, rsa, rsb = rope_tabs
    d_in = lw["w_in"].shape[-1]

    def vec(n):
        return pl.BlockSpec((None, 1, n), lambda b, t: (layer, 0, 0))

    def mat(r, c):
        return pl.BlockSpec((None, r, c), lambda b, t: (layer, 0, 0))

    def per_b(shape):
        return pl.BlockSpec((ns,) + shape, lambda b, t: (b,) + (0,) * len(shape))

    hist = pl.BlockSpec((None, ns, HALO, WIDTH), lambda b, t: (state_layer, b, 0, 0))
    tok = lambda w: pl.BlockSpec((ns, tm, w), lambda b, t: (b, t, 0))
    tab = pl.BlockSpec((tm, WIDTH), lambda b, t: (t, 0))
    in_specs = [tok(d_model), vec(d_model), mat(d_model, d_in),
                mat(CONV_A_WIDTH, WIDTH), mat(CONV_C_WIDTH, WIDTH), vec(WIDTH),
                vec(WIDTH), vec(WIDTH), mat(WIDTH, WIDTH), mat(WIDTH, WIDTH),
                vec(WIDTH), tab, tab, tab, hist, hist, hist]
    out_shape = [jax.ShapeDtypeStruct((bsz, t_len, 4 * WIDTH), F32),
                 jax.ShapeDtypeStruct((bsz, t_len, WIDTH), F32),
                 jax.ShapeDtypeStruct((bsz, t_len, WIDTH), F32)]
    out_specs = [tok(4 * WIDTH), tok(WIDTH), tok(WIDTH)]
    operands = [x, lw["norm_g"], lw["w_in"], lw["conv_a_w"], lw["conv_c_w"], lw["conv_c_b"], lw["ln_c_g"],
                lw["ln_c_b"], lw["w_pw_c"], lw["w_pool"], lw["pool_scale"], rc, rsa, rsb, pre_a, pre_c, pre_d]
    aliases = {}
    if prompt:
        stack_spec = pl.BlockSpec((None, 1, WIDTH, tm), lambda b, t: (layer, b, 0, t))
        for pos, stack in enumerate(kv_stacks):
            aliases[len(operands)] = 1 + pos
            operands.append(stack)
            in_specs.append(pl.BlockSpec(memory_space=pl.ANY))
            out_shape[1 + pos] = jax.ShapeDtypeStruct(stack.shape, F32)
            out_specs[1 + pos] = stack_spec
        assert tm % MOBA_BLOCK == 0 and ns == 1
        per_tile = tm // MOBA_BLOCK
        nblk = t_len // MOBA_BLOCK
        blk = lambda r, c: pl.BlockSpec((1, per_tile, r, c), lambda b, t: (b, t, 0, 0))
        out_shape += [jax.ShapeDtypeStruct((bsz, nblk, WIDTH, MOBA_BLOCK), F32),
                      jax.ShapeDtypeStruct((bsz, nblk, MOBA_BLOCK, WIDTH), BF16),
                      jax.ShapeDtypeStruct((bsz, nblk, WIDTH, MOBA_BLOCK), BF16),
                      jax.ShapeDtypeStruct((bsz, nblk, 1, WIDTH), F32)]
        out_specs += [blk(WIDTH, MOBA_BLOCK), blk(MOBA_BLOCK, WIDTH), blk(WIDTH, MOBA_BLOCK), blk(1, WIDTH)]
    else:
        out_shape += [jax.ShapeDtypeStruct((bsz, t_len, WIDTH), F32)]
        out_specs += [tok(WIDTH)]
    out_shape += [jax.ShapeDtypeStruct((bsz, CONV_A_WIDTH - 1, WIDTH), F32),
                  jax.ShapeDtypeStruct((bsz, CONV_C_WIDTH - 1, WIDTH), F32),
                  jax.ShapeDtypeStruct((bsz, POOL_STATE, WIDTH), F32)]
    out_specs += [per_b((CONV_A_WIDTH - 1, WIDTH)), per_b((CONV_C_WIDTH - 1, WIDTH)),
                  per_b((POOL_STATE, WIDTH))]
    body = functools.partial(_pre_body, ns=ns, tm=tm, last_rows=last_rows, pos0=pos0, prompt=prompt)
    return pl.pallas_call(
        body,
        grid=(bsz // ns, nt),
        in_specs=in_specs,
        out_specs=out_specs,
        out_shape=out_shape,
        scratch_shapes=[pltpu.VMEM((ns, HALO + tm, WIDTH), F32)] * 3
        + [pltpu.VMEM((7, ns, HALO + tm - 8, WIDTH + LANES), F32)]
        + [pltpu.VMEM((ns, HALO + tm - 8, WIDTH), F32)] * 3,
        input_output_aliases=aliases,
        compiler_params=pltpu.CompilerParams(dimension_semantics=("arbitrary", "arbitrary"),
                                             vmem_limit_bytes=VMEM_LIMIT),
        name="pre_prompt" if prompt else "pre_sample",
    )(*operands)


def _top_blocks_bias(gate, index, n_index, axis):
    bias = jnp.full(gate.shape, NEG_INF, F32)
    for _ in range(MOBA_TOPK):
        top = jnp.max(gate, axis=axis, keepdims=True)
        cand = jnp.logical_and(gate == top, top > NEG_INF)
        first = jnp.min(jnp.where(cand, index, n_index), axis=axis, keepdims=True)
        pick = index == first
        bias = jnp.where(pick, 0.0, bias)
        gate = jnp.where(pick, NEG_INF, gate)
    return bias


def _attn_body(qt_ref, kb_ref, vt_ref, km_ref, mixed_ref, x_ref, wout_ref, fng_ref, out_ref,
               qm_s, bias_s, acc_s, s_a, s_b, p_a, p_b, al_a, al_b, *, nblk, tq, grp, final):
    s_bufs, p_bufs, al_bufs = (s_a, s_b), (p_a, p_b), (al_a, al_b)
    w4 = N_HEADS * tq
    i = pl.program_id(1)
    qt = qt_ref[0, 0]
    row_head = lax.broadcasted_iota(jnp.int32, (WIDTH, tq), 0) // HEAD_DIM
    km = km_ref[0]
    lane_head = lax.broadcasted_iota(jnp.int32, (nblk, WIDTH), 1) // HEAD_DIM
    blk = lax.broadcasted_iota(jnp.int32, (nblk, tq), 0)
    km_heads = jnp.concatenate([jnp.where(lane_head == h, km, 0.0) for h in range(N_HEADS)], axis=0)
    gates = jnp.dot(km_heads, qt, precision=lax.Precision.HIGHEST, preferred_element_type=F32)
    for h in range(N_HEADS):
        cols = slice(h * tq, (h + 1) * tq)
        qm_s[:, cols] = jnp.where(row_head == h, qt * LOG2_E, 0.0).astype(BF16)
        gate = jnp.where(blk < i, gates[h * nblk:(h + 1) * nblk, :], NEG_INF)
        bias_s[:, cols] = _top_blocks_bias(gate, blk, nblk, 0)

    n_groups = nblk // grp
    sub = [slice(c * MOBA_BLOCK, (c + 1) * MOBA_BLOCK) for c in range(grp)]

    def qk(group, slot):
        kb = kb_ref[0, pl.ds(group * grp, grp)].reshape(grp * MOBA_BLOCK, WIDTH)
        s_bufs[slot][:, 0:w4] = jnp.dot(kb, qm_s[...], preferred_element_type=F32)

    def pv(group, slot):
        for h in range(N_HEADS):
            rows = slice(h * HEAD_DIM, (h + 1) * HEAD_DIM)
            cols = slice(h * tq, (h + 1) * tq)
            vt = jnp.concatenate([vt_ref[0, group * grp + c, rows, :] for c in range(grp)], axis=1)
            o = jnp.dot(vt, p_bufs[slot][:, cols], preferred_element_type=F32)
            acc_s[rows, :] = al_bufs[slot][:, cols] * acc_s[rows, :] + o

    qk(0, 1)
    key = lax.broadcasted_iota(jnp.int32, (MOBA_BLOCK, w4), 0)
    qry = lax.broadcasted_iota(jnp.int32, (MOBA_BLOCK, w4), 1) % tq
    s = jnp.dot(kb_ref[0, i], qm_s[...], preferred_element_type=F32)
    s = jnp.where(key <= qry, s, NEG_INF)
    m = jnp.max(s, axis=0, keepdims=True)
    p = jnp.exp2(s - m)
    l = jnp.sum(p, axis=0, keepdims=True)
    pb = p.astype(BF16)
    for h in range(N_HEADS):
        rows = slice(h * HEAD_DIM, (h + 1) * HEAD_DIM)
        acc_s[rows, :] = jnp.dot(vt_ref[0, i, rows, :], pb[:, h * tq:(h + 1) * tq], preferred_element_type=F32)
    p_bufs[0][:, 0:w4] = jnp.zeros((grp * MOBA_BLOCK, w4), BF16)
    al_bufs[0][...] = jnp.ones((1, w4), F32)

    def half(n, m, l, cur, nxt):
        picked = [bias_s[pl.ds((n - 1) * grp + c, 1), :] for c in range(grp)]
        m_new = m
        for c in range(grp):
            m_new = jnp.maximum(m_new, jnp.max(s_bufs[cur][sub[c], 0:w4], axis=0, keepdims=True) + picked[c])
        alpha = jnp.exp2(m - m_new)
        l = alpha * l
        for c in range(grp):
            p = jnp.exp2(s_bufs[cur][sub[c], 0:w4] - (m_new - picked[c]))
            l = l + jnp.sum(p, axis=0, keepdims=True)
            p_bufs[cur][sub[c], 0:w4] = p.astype(BF16)
        al_bufs[cur][...] = alpha
        pv(jnp.maximum(n - 2, 0), nxt)
        qk(jnp.minimum(n, n_groups - 1), nxt)
        return m_new, l

    def visit(n, carry):
        return lax.cond(n % 2 == 1, lambda: half(n, carry[0], carry[1], 1, 0),
                        lambda: half(n, carry[0], carry[1], 0, 1))

    n_visits = (i + grp - 1) // grp
    m, l = lax.fori_loop(1, n_visits + 1, visit, (m, l))
    odd = n_visits % 2 == 1

    @pl.when(odd)
    def _():
        pv(n_visits - 1, 1)

    @pl.when(jnp.logical_not(odd))
    def _():
        pv(jnp.maximum(n_visits - 1, 0), 0)

    inv = 1.0 / l
    for h in range(N_HEADS):
        rows = slice(h * HEAD_DIM, (h + 1) * HEAD_DIM)
        acc_s[rows, :] = acc_s[rows, :] * inv[:, h * tq:(h + 1) * tq]
    y_b = acc_s[...].T
    o = _out_proj(x_ref[0], mixed_ref[0], y_b, wout_ref)
    if final:
        o = _rmsnorm(o, fng_ref[...])
    out_ref[0] = o


def _attn_call(qt, kb, vt, km, mixed, x, w_out, layer, fng, *, final):
    bsz, nblk, _, tq = qt.shape
    d_model = x.shape[-1]
    assert tq == MOBA_BLOCK
    grp = next(g for g in (ATTN_GROUP, 2, 1) if nblk % g == 0)
    body = functools.partial(_attn_body, nblk=nblk, tq=tq, grp=grp, final=final)
    whole = lambda r, c: pl.BlockSpec((1, nblk, r, c), lambda b, i: (b, 0, 0, 0))
    return pl.pallas_call(
        body,
        grid=(bsz, nblk),
        in_specs=[pl.BlockSpec((1, 1, WIDTH, tq), lambda b, i: (b, i, 0, 0)),
                  whole(MOBA_BLOCK, WIDTH), whole(WIDTH, MOBA_BLOCK),
                  pl.BlockSpec((1, nblk, WIDTH), lambda b, i: (b, 0, 0)),
                  pl.BlockSpec((1, tq, 4 * WIDTH), lambda b, i: (b, i, 0)),
                  pl.BlockSpec((1, tq, d_model), lambda b, i: (b, i, 0)),
                  pl.BlockSpec((None, 4 * WIDTH, d_model), lambda b, i: (layer, 0, 0)),
                  pl.BlockSpec((1, d_model), lambda b, i: (0, 0))],
        out_specs=pl.BlockSpec((1, tq, d_model), lambda b, i: (b, i, 0)),
        out_shape=jax.ShapeDtypeStruct(x.shape, F32),
        scratch_shapes=[pltpu.VMEM((WIDTH, N_HEADS * tq), BF16),
                        pltpu.VMEM((nblk, N_HEADS * tq), F32),
                        pltpu.VMEM((WIDTH, tq), F32)]
        + [pltpu.VMEM((grp * MOBA_BLOCK, N_HEADS * tq + LANES), F32)] * 2
        + [pltpu.VMEM((grp * MOBA_BLOCK, N_HEADS * tq + LANES), BF16)] * 2
        + [pltpu.VMEM((1, N_HEADS * tq), F32)] * 2,
        compiler_params=pltpu.CompilerParams(dimension_semantics=("arbitrary", "arbitrary"),
                                             vmem_limit_bytes=VMEM_LIMIT),
        name="attn_prompt",
    )(qt, kb, vt, km.reshape(bsz, nblk, WIDTH), mixed, x, w_out, fng)


def _dec_body(pt_ref, *refs, layer, n_seq, pps, ns, t_valid):
    (ck_ref, cv_ref, q_ref, kn_ref, vn_ref, yb_ref,
     qm_s, mp_s, lp_s, gp_s, op_s, pbd_s, kbuf, vbuf, sems) = refs
    step = pl.program_id(1)
    rows = N_HEADS * DEC_ROWS
    nb = pps // 2
    lane_head = lax.broadcasted_iota(jnp.int32, (DEC_ROWS, WIDTH), 1) // HEAD_DIM
    lane = lax.broadcasted_iota(jnp.int32, (rows, LANES), 1)

    g = pl.program_id(0) * ns + step
    n_steps = pl.num_programs(0) * ns

    def page_copies(gg, page_of):
        slot = gg % DEC_SLOTS
        for c in range(pps):
            page = page_of(c)
            yield pltpu.make_async_copy(ck_ref.at[layer, page], kbuf.at[slot, c], sems.at[0, slot])
            yield pltpu.make_async_copy(cv_ref.at[layer, page], vbuf.at[slot, c], sems.at[1, slot])

    def request(gg):
        seq_id, seq_step = gg // ns, gg % ns
        for copy in page_copies(gg, lambda c: pt_ref[seq_id, seq_step * pps + c]):
            copy.start()

    @pl.when(g == 0)
    def _():
        pbd_s[...] = jnp.zeros(pbd_s.shape, BF16)
        for ahead in range(min(DEC_SLOTS - 1, n_seq * ns)):
            request(ahead)

    @pl.when(g + (DEC_SLOTS - 1) < n_steps)
    def _():
        request(g + (DEC_SLOTS - 1))

    for copy in page_copies(g, lambda c: 0):
        copy.wait()
    slot = g % DEC_SLOTS

    @pl.when(step == 0)
    def _():
        q8 = q_ref[0]
        for h in range(N_HEADS):
            qm_s[h * DEC_ROWS:(h + 1) * DEC_ROWS, :] = jnp.where(lane_head == h, q8, 0.0)
        mp_s[...] = jnp.full((rows, LANES), NEG_INF, F32)
        gp_s[...] = jnp.full((rows, LANES), NEG_INF, F32)
        lp_s[...] = jnp.zeros((rows, LANES), F32)

    qmb = qm_s[...].astype(BF16)
    contract_last = (((1,), (1,)), ((), ()))
    kt = jnp.concatenate([kbuf[slot, c] for c in range(pps)], axis=1).astype(BF16)
    vt = jnp.concatenate([vbuf[slot, c] for c in range(pps)], axis=1).astype(BF16)
    s_all = jnp.dot(qmb, kt, preferred_element_type=F32)
    for c in range(nb):
        s = s_all[:, c * MOBA_BLOCK:(c + 1) * MOBA_BLOCK]
        gate = jnp.sum(s, axis=1, keepdims=True) * (1.0 / MOBA_BLOCK)
        m = jnp.max(s, axis=1, keepdims=True)
        p = jnp.exp(s - m)
        l = jnp.sum(p, axis=1, keepdims=True)
        j = step * nb + c
        mp_s[...] = jnp.where(lane == j, m, mp_s[...])
        lp_s[...] = jnp.where(lane == j, l, lp_s[...])
        gp_s[...] = jnp.where(lane == j, gate, gp_s[...])
        pbd_s[c * rows:(c + 1) * rows, c * MOBA_BLOCK:(c + 1) * MOBA_BLOCK] = p.astype(BF16)
    op_s[pl.ds(pl.multiple_of(step * (nb * rows), nb * rows), nb * rows), :] = lax.dot_general(
        pbd_s[...], vt, contract_last, preferred_element_type=F32)

    @pl.when(step == ns - 1)
    def _():
        nblk = ns * nb
        sel = _top_blocks_bias(gp_s[...], lane, LANES, 1) == 0.0
        zpad = jnp.zeros((LANES - DEC_ROWS, WIDTH), F32)
        kn = jnp.concatenate([kn_ref[0], zpad], axis=0).astype(BF16)
        vn = jnp.concatenate([vn_ref[0], zpad], axis=0).astype(BF16)
        s = lax.dot_general(qmb, kn, contract_last, preferred_element_type=F32)
        qpos = jnp.minimum(lax.broadcasted_iota(jnp.int32, (rows, LANES), 0) % DEC_ROWS, t_valid - 1)
        s = jnp.where(lane <= qpos, s, NEG_INF)
        m_own = jnp.max(s, axis=1, keepdims=True)
        p = jnp.exp(s - m_own)
        l_own = jnp.sum(p, axis=1, keepdims=True)
        o_own = jnp.dot(p.astype(BF16), vn, preferred_element_type=F32)
        mp = mp_s[...]
        m_fin = jnp.maximum(m_own, jnp.max(jnp.where(sel, mp, NEG_INF), axis=1, keepdims=True))
        w = jnp.where(sel, jnp.exp(mp - m_fin), 0.0)
        a_own = jnp.exp(m_own - m_fin)
        l_fin = a_own * l_own + jnp.sum(w * lp_s[...], axis=1, keepdims=True)
        o_fin = a_own * o_own
        for j in range(nblk):
            o_fin = o_fin + w[:, j:j + 1] * op_s[j * rows:(j + 1) * rows, :]
        y = o_fin / l_fin
        y_b = None
        for h in range(N_HEADS):
            part = jnp.where(lane_head == h, y[h * DEC_ROWS:(h + 1) * DEC_ROWS, :], 0.0)
            y_b = part if y_b is None else y_b + part
        yb_ref[0] = y_b


def _dec_call(page_table, cache_k, cache_v, layer, q, kn, vn, *, t_valid):
    bsz, n_pages = page_table.shape
    depth, n_pool, page, _, _ = cache_k.shape
    assert (n_pages * page) % MOBA_BLOCK == 0 and MOBA_BLOCK == 2 * page
    nblk = n_pages * page // MOBA_BLOCK
    assert MOBA_TOPK <= nblk <= LANES
    pps = next(c for c in (16, 8, 4, 2) if n_pages % c == 0)
    ns = n_pages // pps
    ck = cache_k.transpose(0, 1, 3, 4, 2).reshape(depth, n_pool, WIDTH, page)
    cv = cache_v.transpose(0, 1, 3, 4, 2).reshape(depth, n_pool, WIDTH, page)

    in_hbm = pl.BlockSpec(memory_space=pl.ANY)
    per_b = pl.BlockSpec((1, DEC_ROWS, WIDTH), lambda b, s, pt: (b, 0, 0))
    rows = N_HEADS * DEC_ROWS
    body = functools.partial(_dec_body, layer=layer, n_seq=bsz, pps=pps, ns=ns, t_valid=t_valid)
    grid_spec = pltpu.PrefetchScalarGridSpec(
        num_scalar_prefetch=1,
        grid=(bsz, ns),
        in_specs=[in_hbm, in_hbm, per_b, per_b, per_b],
        out_specs=per_b,
        scratch_shapes=[pltpu.VMEM((rows, WIDTH), F32), pltpu.VMEM((rows, LANES), F32),
                        pltpu.VMEM((rows, LANES), F32), pltpu.VMEM((rows, LANES), F32),
                        pltpu.VMEM((nblk * rows, WIDTH), F32),
                        pltpu.VMEM((pps // 2 * rows, pps * page), BF16),
                        pltpu.VMEM((DEC_SLOTS, pps, WIDTH, page), F32),
                        pltpu.VMEM((DEC_SLOTS, pps, WIDTH, page), F32),
                        pltpu.SemaphoreType.DMA((2, DEC_SLOTS))],
    )
    return pl.pallas_call(
        body,
        grid_spec=grid_spec,
        out_shape=jax.ShapeDtypeStruct((bsz, DEC_ROWS, WIDTH), F32),
        compiler_params=pltpu.CompilerParams(dimension_semantics=("arbitrary", "arbitrary"),
                                             vmem_limit_bytes=VMEM_LIMIT),
        name="attn_sample",
    )(page_table, ck, cv, q, kn, vn)


def _proj_body(yb_ref, mixed_ref, x_ref, wout_ref, fng_ref, out_ref, *, final):
    o = _out_proj(x_ref[...], mixed_ref[...], yb_ref[...], wout_ref)
    if final:
        o = _rmsnorm(o, fng_ref[...])
    out_ref[...] = o


def _proj_call(y_b, mixed, x, w_out, layer, fng, *, final):
    bsz, t_len, d_model = x.shape
    n = bsz * t_len
    whole = lambda r, c: pl.BlockSpec((r, c), lambda i: (0, 0))
    out = pl.pallas_call(
        functools.partial(_proj_body, final=final),
        grid=(1,),
        in_specs=[whole(n, WIDTH), whole(n, 4 * WIDTH), whole(n, d_model),
                  pl.BlockSpec((None, 4 * WIDTH, d_model), lambda i: (layer, 0, 0)),
                  whole(1, d_model)],
        out_specs=whole(n, d_model),
        out_shape=jax.ShapeDtypeStruct((n, d_model), F32),
        compiler_params=pltpu.CompilerParams(dimension_semantics=("arbitrary",), vmem_limit_bytes=VMEM_LIMIT),
        name="proj_sample",
    )(y_b.reshape(n, WIDTH), mixed.reshape(n, 4 * WIDTH), x.reshape(n, d_model), w_out, fng)
    return out.reshape(bsz, t_len, d_model)


def _rope_tables(pos0, n):
    freqs = ROPE_THETA ** (-jnp.arange(ROT_HALF, dtype=F32) * 2.0 / (2 * ROT_HALF))
    ang = (pos0 + jnp.arange(n, dtype=jnp.int32)).astype(F32)[:, None] * freqs[None, :]
    cos, sin = jnp.cos(ang), jnp.sin(ang)
    one = jnp.ones((n, HEAD_DIM - 2 * ROT_HALF), F32)
    zero8 = jnp.zeros((n, ROT_HALF), F32)
    zero_rest = jnp.zeros((n, HEAD_DIM - 2 * ROT_HALF), F32)
    rc = jnp.concatenate([cos, cos, one], axis=1)
    rsa = jnp.concatenate([-sin, zero8, zero_rest], axis=1)
    rsb = jnp.concatenate([zero8, sin, zero_rest], axis=1)
    return tuple(jnp.tile(a, (1, N_HEADS)) for a in (rc, rsa, rsb))


def _pad_state(state):
    return jnp.pad(state, ((0, 0), (0, 0), (HALO - state.shape[2], 0), (0, 0)))


def kernel(x_prompt, x_sample, cache_k, cache_v, page_table, state_conv_a, state_conv_c, state_pool_d,
           norm_g, w_in, conv_a_w, conv_c_w, conv_c_b, ln_c_g, ln_c_b, w_pw_c, w_pool, pool_scale, w_out,
           final_norm_g):
    depth = w_in.shape[0]
    bsz, seq, d_model = x_prompt.shape
    dec_b, dec_t, _ = x_sample.shape
    past_len = page_table.shape[1] * cache_k.shape[2]
    assert seq % MOBA_BLOCK == 0 and dec_t <= DEC_ROWS and WIDTH == N_HEADS * HEAD_DIM

    pre_tile = next(t for t in (PRE_TILE, MOBA_BLOCK) if seq % t == 0)
    tabs_p = _rope_tables(0, seq)
    tabs_s = _rope_tables(past_len, DEC_ROWS)
    zero_state = jnp.zeros((1, bsz, HALO, WIDTH), F32)
    hist_a, hist_c, hist_d = _pad_state(state_conv_a), _pad_state(state_conv_c), _pad_state(state_pool_d)
    fng = final_norm_g.reshape(1, d_model)
    yp = x_prompt
    ys = jnp.pad(x_sample, ((0, 0), (0, DEC_ROWS - dec_t), (0, 0)))
    outs = [[] for _ in range(8)]
    kstack = jnp.zeros((depth, bsz, WIDTH, seq), F32)
    vstack = jnp.zeros((depth, bsz, WIDTH, seq), F32)
    n_groups = w_pool.shape[1]
    eye = jnp.eye(n_groups, dtype=F32)
    pool_bd = (w_pool[:, :, :, None, :] * eye[None, :, None, :, None]).reshape(depth, WIDTH, WIDTH)
    row = lambda a: a.reshape(depth, 1, a.shape[-1])
    lw = {"norm_g": row(norm_g), "w_in": w_in.astype(BF16), "conv_a_w": conv_a_w, "conv_c_w": conv_c_w,
          "conv_c_b": row(conv_c_b), "ln_c_g": row(ln_c_g), "ln_c_b": row(ln_c_b),
          "w_pw_c": w_pw_c.astype(BF16), "w_pool": pool_bd.astype(BF16), "pool_scale": row(pool_scale)}
    wo = w_out.astype(BF16)
    for l in range(depth):
        final = l == depth - 1

        mixed, kstack, vstack, qt, kb, vt, km, ap, cp, dp = _pre_call(
            yp, lw, l, tabs_p, zero_state, zero_state, zero_state, 0, (kstack, vstack),
            ns=1, tm=pre_tile, last_rows=pre_tile, pos0=0, prompt=True)
        yp = _attn_call(qt, kb, vt, km, mixed, yp, wo, l, fng, final=final)

        mixed_s, ks, vs, qs, as_, cs, ds = _pre_call(
            ys, lw, l, tabs_s, hist_a, hist_c, hist_d, l,
            ns=dec_b, tm=DEC_ROWS, last_rows=dec_t, pos0=past_len, prompt=False)
        yb_s = _dec_call(page_table, cache_k, cache_v, l, qs, ks, vs, t_valid=dec_t)
        ys = _proj_call(yb_s, mixed_s, ys, wo, l, fng, final=final)

        for lst, val in zip(outs, (ks[:, :dec_t].reshape(dec_b, dec_t, N_HEADS, HEAD_DIM),
                                   vs[:, :dec_t].reshape(dec_b, dec_t, N_HEADS, HEAD_DIM),
                                   ap, as_, cp, cs, dp, ds)):
            lst.append(val)
    to_rows = lambda st: st.reshape(depth, bsz, N_HEADS, HEAD_DIM, seq).transpose(0, 1, 4, 2, 3)
    return (yp, ys[:, :dec_t], to_rows(kstack), to_rows(vstack)) + tuple(jnp.stack(o) for o in outs)
```

```python
import functools

import jax
import jax.numpy as jnp
from jax import lax
from jax.experimental import pallas as pl
from jax.experimental.pallas import tpu as pltpu

F32 = jnp.float32
BF16 = jnp.bfloat16

WIDTH = 256
N_HEADS = 4
HEAD_DIM = 64
ROT_HALF = 8
ROPE_THETA = 500000.0
MOBA_BLOCK = 256
MOBA_TOPK = 3
CONV_A_WIDTH = 3
CONV_C_WIDTH = 31
POOL_WINDOWS = (2, 4, 8, 16)
POOL_STATE = 15
RMS_EPS = 1e-6
LN_EPS = 1e-5
HALO = 32
DEC_ROWS = 8
ATTN_GROUP = 4
DEC_SLOTS = 5
PRE_TILE = 1024
NEG_INF = float("-inf")
LOG2_E = 1.4426950408889634
LANES = 128
V7X_VMEM_BYTES = 64 * 1024 * 1024
VMEM_LIMIT = V7X_VMEM_BYTES - 8 * 1024 * 1024


def _silu(z):
    return z * jax.nn.sigmoid(z)


def _rmsnorm(x, g):
    ms = jnp.mean(x * x, axis=-1, keepdims=True)
    return x * lax.rsqrt(ms + RMS_EPS) * g


def _out_proj(x, mixed, y_b, wout_ref):
    yb = (y_b * mixed[:, WIDTH:2 * WIDTH]).astype(BF16)
    o = x + jnp.dot(mixed[:, 0:WIDTH].astype(BF16), wout_ref[0:WIDTH, :], preferred_element_type=F32)
    o = o + jnp.dot(yb, wout_ref[WIDTH:2 * WIDTH, :], preferred_element_type=F32)
    o = o + jnp.dot(mixed[:, 2 * WIDTH:4 * WIDTH].astype(BF16), wout_ref[2 * WIDTH:4 * WIDTH, :],
                    preferred_element_type=F32)
    return o


def _pre_body(x_ref, ng_ref, win_ref, caw_ref, ccw_ref, ccb_ref, lng_ref, lnb_ref, wpw_ref, wpool_ref,
              psc_ref, rc_ref, rsa_ref, rsb_ref, prea_ref, prec_ref, pred_ref, *refs,
              ns, tm, last_rows, pos0, prompt):
    if prompt:
        (_, _, mixed_ref, kst_ref, vst_ref, qt_ref, kb_ref, vt_ref, km_ref,
         na_ref, nc_ref, nd_ref, bufa, bufc, bufd, shc, x2, x4, x8) = refs
    else:
        (mixed_ref, k_ref, v_ref, q_ref, na_ref, nc_ref, nd_ref, bufa, bufc, bufd, shc, x2, x4, x8) = refs
    t = pl.program_id(1)
    nt = pl.num_programs(1)
    span = tm + HALO - 8
    n = ns * tm
    seq = lambda z: z.reshape(ns, tm, WIDTH)
    flat = lambda z: z.reshape(n, WIDTH)

    @pl.when(t == 0)
    def _():
        bufa[:, 0:HALO, :] = prea_ref[...]
        bufc[:, 0:HALO, :] = prec_ref[...]
        bufd[:, 0:HALO, :] = pred_ref[...]

    x = x_ref[...].reshape(n, x_ref.shape[-1])
    h = _rmsnorm(x, ng_ref[...]).astype(BF16)

    def proj(c0, cn):
        return jnp.dot(h, win_ref[:, c0 * WIDTH:(c0 + cn) * WIDTH], preferred_element_type=F32)

    def put(col, val):
        mixed_ref[:, :, col * WIDTH:(col + 1) * WIDTH] = seq(val)

    pa = proj(0, 4)
    bufa[:, HALO:HALO + tm, :] = seq(pa[:, 2 * WIDTH:3 * WIDTH] * pa[:, 0:WIDTH])
    conv = None
    for j in range(CONV_A_WIDTH):
        off = HALO - (CONV_A_WIDTH - 1) + j
        term = caw_ref[j:j + 1, :] * bufa[:, off:off + tm, :]
        conv = term if conv is None else conv + term
    put(0, pa[:, WIDTH:2 * WIDTH] * flat(conv) * _silu(pa[:, 3 * WIDTH:4 * WIDTH]))

    pb = proj(4, 4)
    tab = lambda r: flat(jnp.broadcast_to(r[...][None], (ns, tm, WIDTH)))
    rc, rsa, rsb = tab(rc_ref), tab(rsa_ref), tab(rsb_ref)

    def rope(z):
        return z * rc + pltpu.roll(z, WIDTH - ROT_HALF, 1) * rsa + pltpu.roll(z, ROT_HALF, 1) * rsb

    qs = rope(pb[:, 0:WIDTH]) * (HEAD_DIM ** -0.5)
    kr = rope(pb[:, WIDTH:2 * WIDTH])
    v = pb[:, 2 * WIDTH:3 * WIDTH]
    put(1, _silu(pb[:, 3 * WIDTH:4 * WIDTH]))
    if prompt:
        kst_ref[0] = kr.T
        v_t = v.T
        vst_ref[0] = v_t
        for r in range(tm // MOBA_BLOCK):
            blk = slice(r * MOBA_BLOCK, (r + 1) * MOBA_BLOCK)
            qt_ref[0, r] = qs[blk, :].T
            kb_ref[0, r] = kr[blk, :].astype(BF16)
            vt_ref[0, r] = v_t[:, blk].astype(BF16)
            km_ref[0, r] = jnp.mean(kr[blk, :], axis=0, keepdims=True)
    else:
        k_ref[...] = seq(kr)
        v_ref[...] = seq(v)
        q_ref[...] = seq(qs)

    pc = proj(8, 3)
    bufc[:, HALO:HALO + tm, :] = seq(pc[:, 0:WIDTH] * jax.nn.sigmoid(pc[:, WIDTH:2 * WIDTH]))
    for r in range(1, 8):
        shc[r - 1] = bufc[:, r:r + span, :]
    acc = None
    for j in range(CONV_C_WIDTH):
        a, r = divmod(HALO - (CONV_C_WIDTH - 1) + j, 8)
        rows = bufc[:, 8 * a:8 * a + tm, :] if r == 0 else shc[r - 1, :, 8 * a:8 * a + tm, :]
        term = ccw_ref[j:j + 1, :] * rows
        acc = term if acc is None else acc + term
    acc = flat(acc) + ccb_ref[...]
    mu = jnp.mean(acc, axis=-1, keepdims=True)
    cen = acc - mu
    var = jnp.mean(cen * cen, axis=-1, keepdims=True)
    ln = cen * lax.rsqrt(var + LN_EPS) * lng_ref[...] + lnb_ref[...]
    yc = jnp.dot(_silu(ln).astype(BF16), wpw_ref[...], preferred_element_type=F32)
    put(2, yc * _silu(pc[:, 2 * WIDTH:3 * WIDTH]))

    pd = proj(11, 2)
    dx = pd[:, 0:WIDTH]
    bufd[:, HALO:HALO + tm, :] = seq(dx)
    assert POOL_WINDOWS == (2, 4, 8, 16) and HALO == 32
    x2[...] = bufd[:, 8:8 + span, :] + bufd[:, 7:7 + span, :]
    x4[:, 0:span - 8, :] = x2[:, 8:span, :] + x2[:, 6:span - 2, :]
    x8[:, 0:span - 16, :] = x4[:, 8:span - 8, :] + x4[:, 4:span - 12, :]
    sums = {2: x2[:, 24:24 + tm, :], 4: x4[:, 16:16 + tm, :], 8: x8[:, 8:8 + tm, :],
            16: x8[:, 8:8 + tm, :] + x8[:, 0:tm, :]}
    group = lax.broadcasted_iota(jnp.int32, (ns, tm, WIDTH), 2) // (WIDTH // len(POOL_WINDOWS))
    pos = pos0 + t * tm + lax.broadcasted_iota(jnp.int32, (ns, tm, WIDTH), 1)
    wsum = sums[POOL_WINDOWS[-1]]
    win = jnp.full((ns, tm, WIDTH), POOL_WINDOWS[-1], jnp.int32)
    for g in range(len(POOL_WINDOWS) - 2, -1, -1):
        wsum = jnp.where(group == g, sums[POOL_WINDOWS[g]], wsum)
        win = jnp.where(group == g, POOL_WINDOWS[g], win)
    cnt = jnp.minimum(pos + 1, win).astype(F32)
    diff = flat(wsum / cnt) - dx
    yd = jnp.dot(diff.astype(BF16), wpool_ref[...], preferred_element_type=F32) * psc_ref[...]
    put(3, yd * _silu(pd[:, WIDTH:2 * WIDTH]))

    @pl.when(t == nt - 1)
    def _():
        end = HALO + last_rows
        na_ref[...] = bufa[:, end - (CONV_A_WIDTH - 1):end, :]
        nc_ref[...] = bufc[:, end - (CONV_C_WIDTH - 1):end, :]
        nd_ref[...] = bufd[:, end - POOL_STATE:end, :]

    if tm >= HALO:
        @pl.when(t < nt - 1)
        def _():
            bufa[:, 0:HALO, :] = bufa[:, tm:tm + HALO, :]
            bufc[:, 0:HALO, :] = bufc[:, tm:tm + HALO, :]
            bufd[:, 0:HALO, :] = bufd[:, tm:tm + HALO, :]


def _pre_call(x, lw, layer, rope_tabs, pre_a, pre_c, pre_d, state_layer, kv_stacks=None, *,
              ns, tm, last_rows, pos0, prompt):
    bsz, t_len, d_model = x.shape
    nt = t_len // tm
    assert nt * tm == t_len and (nt == 1 or tm >= HALO) and bsz % ns == 0 and tm % 8 == 0
    rc, rsa, rsb = rope_tabs
    d_in = lw["w_in"].shape[-1]

    def vec(n):
        return pl.BlockSpec((None, 1, n), lambda b, t: (layer, 0, 0))

    def mat(r, c):
        return pl.BlockSpec((None, r, c), lambda b, t: (layer, 0, 0))

    def per_b(shape):
        return pl.BlockSpec((ns,) + shape, lambda b, t: (b,) + (0,) * len(shape))

    hist = pl.BlockSpec((None, ns, HALO, WIDTH), lambda b, t: (state_layer, b, 0, 0))
    tok = lambda w: pl.BlockSpec((ns, tm, w), lambda b, t: (b, t, 0))
    tab = pl.BlockSpec((tm, WIDTH), lambda b, t: (t, 0))
    in_specs = [tok(d_model), vec(d_model), mat(d_model, d_in),
                mat(CONV_A_WIDTH, WIDTH), mat(CONV_C_WIDTH, WIDTH), vec(WIDTH),
                vec(WIDTH), vec(WIDTH), mat(WIDTH, WIDTH), mat(WIDTH, WIDTH),
                vec(WIDTH), tab, tab, tab, hist, hist, hist]
    out_shape = [jax.ShapeDtypeStruct((bsz, t_len, 4 * WIDTH), F32),
                 jax.ShapeDtypeStruct((bsz, t_len, WIDTH), F32),
                 jax.ShapeDtypeStruct((bsz, t_len, WIDTH), F32)]
    out_specs = [tok(4 * WIDTH), tok(WIDTH), tok(WIDTH)]
    operands = [x, lw["norm_g"], lw["w_in"], lw["conv_a_w"], lw["conv_c_w"], lw["conv_c_b"], lw["ln_c_g"],
                lw["ln_c_b"], lw["w_pw_c"], lw["w_pool"], lw["pool_scale"], rc, rsa, rsb, pre_a, pre_c, pre_d]
    aliases = {}
    if prompt:
        stack_spec = pl.BlockSpec((None, 1, WIDTH, tm), lambda b, t: (layer, b, 0, t))
        for pos, stack in enumerate(kv_stacks):
            aliases[len(operands)] = 1 + pos
            operands.append(stack)
            in_specs.append(pl.BlockSpec(memory_space=pl.ANY))
            out_shape[1 + pos] = jax.ShapeDtypeStruct(stack.shape, F32)
            out_specs[1 + pos] = stack_spec
        assert tm % MOBA_BLOCK == 0 and ns == 1
        per_tile = tm // MOBA_BLOCK
        nblk = t_len // MOBA_BLOCK
        blk = lambda r, c: pl.BlockSpec((1, per_tile, r, c), lambda b, t: (b, t, 0, 0))
        out_shape += [jax.ShapeDtypeStruct((bsz, nblk, WIDTH, MOBA_BLOCK), F32),
                      jax.ShapeDtypeStruct((bsz, nblk, MOBA_BLOCK, WIDTH), BF16),
                      jax.ShapeDtypeStruct((bsz, nblk, WIDTH, MOBA_BLOCK), BF16),
                      jax.ShapeDtypeStruct((bsz, nblk, 1, WIDTH), F32)]
        out_specs += [blk(WIDTH, MOBA_BLOCK), blk(MOBA_BLOCK, WIDTH), blk(WIDTH, MOBA_BLOCK), blk(1, WIDTH)]
    else:
        out_shape += [jax.ShapeDtypeStruct((bsz, t_len, WIDTH), F32)]
        out_specs += [tok(WIDTH)]
    out_shape += [jax.ShapeDtypeStruct((bsz, CONV_A_WIDTH - 1, WIDTH), F32),
                  jax.ShapeDtypeStruct((bsz, CONV_C_WIDTH - 1, WIDTH), F32),
                  jax.ShapeDtypeStruct((bsz, POOL_STATE, WIDTH), F32)]
    out_specs += [per_b((CONV_A_WIDTH - 1, WIDTH)), per_b((CONV_C_WIDTH - 1, WIDTH)),
                  per_b((POOL_STATE, WIDTH))]
    body = functools.partial(_pre_body, ns=ns, tm=tm, last_rows=last_rows, pos0=pos0, prompt=prompt)
    return pl.pallas_call(
        body,
        grid=(bsz // ns, nt),
        in_specs=in_specs,
        out_specs=out_specs,
        out_shape=out_shape,
        scratch_shapes=[pltpu.VMEM((ns, HALO + tm, WIDTH), F32)] * 3
        + [pltpu.VMEM((7, ns, HALO + tm - 8, WIDTH), F32)] + [pltpu.VMEM((ns, HALO + tm - 8, WIDTH), F32)] * 3,
        input_output_aliases=aliases,
        compiler_params=pltpu.CompilerParams(dimension_semantics=("arbitrary", "arbitrary"),
                                             vmem_limit_bytes=VMEM_LIMIT),
        name="pre_prompt" if prompt else "pre_sample",
    )(*operands)


def _top_blocks_bias(gate, index, n_index, axis):
    bias = jnp.full(gate.shape, NEG_INF, F32)
    for _ in range(MOBA_TOPK):
        top = jnp.max(gate, axis=axis, keepdims=True)
        cand = jnp.logical_and(gate == top, top > NEG_INF)
        first = jnp.min(jnp.where(cand, index, n_index), axis=axis, keepdims=True)
        pick = index == first
        bias = jnp.where(pick, 0.0, bias)
        gate = jnp.where(pick, NEG_INF, gate)
    return bias


def _attn_body(qt_ref, kb_ref, vt_ref, km_ref, mixed_ref, x_ref, wout_ref, fng_ref, out_ref,
               qm_s, bias_s, acc_s, s_a, s_b, p_a, p_b, al_a, al_b, *, nblk, tq, grp, final):
    s_bufs, p_bufs, al_bufs = (s_a, s_b), (p_a, p_b), (al_a, al_b)
    w4 = N_HEADS * tq
    i = pl.program_id(1)
    qt = qt_ref[0, 0]
    row_head = lax.broadcasted_iota(jnp.int32, (WIDTH, tq), 0) // HEAD_DIM
    km = km_ref[0]
    lane_head = lax.broadcasted_iota(jnp.int32, (nblk, WIDTH), 1) // HEAD_DIM
    blk = lax.broadcasted_iota(jnp.int32, (nblk, tq), 0)
    km_heads = jnp.concatenate([jnp.where(lane_head == h, km, 0.0) for h in range(N_HEADS)], axis=0)
    gates = jnp.dot(km_heads, qt, precision=lax.Precision.HIGHEST, preferred_element_type=F32)
    for h in range(N_HEADS):
        cols = slice(h * tq, (h + 1) * tq)
        qm_s[:, cols] = jnp.where(row_head == h, qt * LOG2_E, 0.0).astype(BF16)
        gate = jnp.where(blk < i, gates[h * nblk:(h + 1) * nblk, :], NEG_INF)
        bias_s[:, cols] = _top_blocks_bias(gate, blk, nblk, 0)

    n_groups = nblk // grp
    sub = [slice(c * MOBA_BLOCK, (c + 1) * MOBA_BLOCK) for c in range(grp)]

    def qk(group, slot):
        kb = kb_ref[0, pl.ds(group * grp, grp)].reshape(grp * MOBA_BLOCK, WIDTH)
        s_bufs[slot][:, 0:w4] = jnp.dot(kb, qm_s[:, 0:w4], preferred_element_type=F32)

    def pv(group, slot):
        for h in range(N_HEADS):
            rows = slice(h * HEAD_DIM, (h + 1) * HEAD_DIM)
            cols = slice(h * tq, (h + 1) * tq)
            vt = jnp.concatenate([vt_ref[0, group * grp + c, rows, :] for c in range(grp)], axis=1)
            o = jnp.dot(vt, p_bufs[slot][:, cols], preferred_element_type=F32)
            acc_s[rows, :] = al_bufs[slot][:, cols] * acc_s[rows, :] + o

    qk(0, 1)
    key = lax.broadcasted_iota(jnp.int32, (MOBA_BLOCK, w4), 0)
    qry = lax.broadcasted_iota(jnp.int32, (MOBA_BLOCK, w4), 1) % tq
    s = jnp.dot(kb_ref[0, i], qm_s[:, 0:w4], preferred_element_type=F32)
    s = jnp.where(key <= qry, s, NEG_INF)
    m = jnp.max(s, axis=0, keepdims=True)
    p = jnp.exp2(s - m)
    l = jnp.sum(p, axis=0, keepdims=True)
    pb = p.astype(BF16)
    for h in range(N_HEADS):
        rows = slice(h * HEAD_DIM, (h + 1) * HEAD_DIM)
        acc_s[rows, :] = jnp.dot(vt_ref[0, i, rows, :], pb[:, h * tq:(h + 1) * tq], preferred_element_type=F32)
    p_bufs[0][:, 0:w4] = jnp.zeros((grp * MOBA_BLOCK, w4), BF16)
    al_bufs[0][...] = jnp.ones((1, w4), F32)

    def half(n, m, l, cur, nxt):
        picked = [bias_s[pl.ds((n - 1) * grp + c, 1), :] for c in range(grp)]
        m_new = m
        for c in range(grp):
            m_new = jnp.maximum(m_new, jnp.max(s_bufs[cur][sub[c], 0:w4], axis=0, keepdims=True) + picked[c])
        alpha = jnp.exp2(m - m_new)
        l = alpha * l
        for c in range(grp):
            p = jnp.exp2(s_bufs[cur][sub[c], 0:w4] - (m_new - picked[c]))
            l = l + jnp.sum(p, axis=0, keepdims=True)
            p_bufs[cur][sub[c], 0:w4] = p.astype(BF16)
        al_bufs[cur][...] = alpha
        pv(jnp.maximum(n - 2, 0), nxt)
        qk(jnp.minimum(n, n_groups - 1), nxt)
        return m_new, l

    def visit(n, carry):
        return lax.cond(n % 2 == 1, lambda: half(n, carry[0], carry[1], 1, 0),
                        lambda: half(n, carry[0], carry[1], 0, 1))

    n_visits = (i + grp - 1) // grp
    m, l = lax.fori_loop(1, n_visits + 1, visit, (m, l))
    odd = n_visits % 2 == 1

    @pl.when(odd)
    def _():
        pv(n_visits - 1, 1)

    @pl.when(jnp.logical_not(odd))
    def _():
        pv(jnp.maximum(n_visits - 1, 0), 0)

    inv = 1.0 / l
    for h in range(N_HEADS):
        rows = slice(h * HEAD_DIM, (h + 1) * HEAD_DIM)
        acc_s[rows, :] = acc_s[rows, :] * inv[:, h * tq:(h + 1) * tq]
    y_b = acc_s[...].T
    o = _out_proj(x_ref[0], mixed_ref[0], y_b, wout_ref)
    if final:
        o = _rmsnorm(o, fng_ref[...])
    out_ref[0] = o


def _attn_call(qt, kb, vt, km, mixed, x, w_out, layer, fng, *, final):
    bsz, nblk, _, tq = qt.shape
    d_model = x.shape[-1]
    assert tq == MOBA_BLOCK
    grp = next(g for g in (ATTN_GROUP, 2, 1) if nblk % g == 0)
    body = functools.partial(_attn_body, nblk=nblk, tq=tq, grp=grp, final=final)
    whole = lambda r, c: pl.BlockSpec((1, nblk, r, c), lambda b, i: (b, 0, 0, 0))
    return pl.pallas_call(
        body,
        grid=(bsz, nblk),
        in_specs=[pl.BlockSpec((1, 1, WIDTH, tq), lambda b, i: (b, i, 0, 0)),
                  whole(MOBA_BLOCK, WIDTH), whole(WIDTH, MOBA_BLOCK),
                  pl.BlockSpec((1, nblk, WIDTH), lambda b, i: (b, 0, 0)),
                  pl.BlockSpec((1, tq, 4 * WIDTH), lambda b, i: (b, i, 0)),
                  pl.BlockSpec((1, tq, d_model), lambda b, i: (b, i, 0)),
                  pl.BlockSpec((None, 4 * WIDTH, d_model), lambda b, i: (layer, 0, 0)),
                  pl.BlockSpec((1, d_model), lambda b, i: (0, 0))],
        out_specs=pl.BlockSpec((1, tq, d_model), lambda b, i: (b, i, 0)),
        out_shape=jax.ShapeDtypeStruct(x.shape, F32),
        scratch_shapes=[pltpu.VMEM((WIDTH, N_HEADS * tq + LANES), BF16),
                        pltpu.VMEM((nblk, N_HEADS * tq), F32),
                        pltpu.VMEM((WIDTH, tq), F32)]
        + [pltpu.VMEM((grp * MOBA_BLOCK, N_HEADS * tq + LANES), F32)] * 2
        + [pltpu.VMEM((grp * MOBA_BLOCK, N_HEADS * tq + LANES), BF16)] * 2
        + [pltpu.VMEM((1, N_HEADS * tq), F32)] * 2,
        compiler_params=pltpu.CompilerParams(dimension_semantics=("arbitrary", "arbitrary"),
                                             vmem_limit_bytes=VMEM_LIMIT),
        name="attn_prompt",
    )(qt, kb, vt, km.reshape(bsz, nblk, WIDTH), mixed, x, w_out, fng)


def _dec_body(pt_ref, *refs, layer, n_seq, pps, ns, t_valid):
    (ck_ref, cv_ref, q_ref, kn_ref, vn_ref, yb_ref,
     qm_s, mp_s, lp_s, gp_s, op_s, pbd_s, kbuf, vbuf, sems) = refs
    step = pl.program_id(1)
    rows = N_HEADS * DEC_ROWS
    nb = pps // 2
    lane_head = lax.broadcasted_iota(jnp.int32, (DEC_ROWS, WIDTH), 1) // HEAD_DIM
    lane = lax.broadcasted_iota(jnp.int32, (rows, LANES), 1)

    g = pl.program_id(0) * ns + step
    n_steps = pl.num_programs(0) * ns

    def page_copies(gg, page_of):
        slot = gg % DEC_SLOTS
        for c in range(pps):
            page = page_of(c)
            yield pltpu.make_async_copy(ck_ref.at[layer, page], kbuf.at[slot, c], sems.at[0, slot])
            yield pltpu.make_async_copy(cv_ref.at[layer, page], vbuf.at[slot, c], sems.at[1, slot])

    def request(gg):
        seq_id, seq_step = gg // ns, gg % ns
        for copy in page_copies(gg, lambda c: pt_ref[seq_id, seq_step * pps + c]):
            copy.start()

    @pl.when(g == 0)
    def _():
        pbd_s[...] = jnp.zeros(pbd_s.shape, BF16)
        for ahead in range(min(DEC_SLOTS - 1, n_seq * ns)):
            request(ahead)

    @pl.when(g + (DEC_SLOTS - 1) < n_steps)
    def _():
        request(g + (DEC_SLOTS - 1))

    for copy in page_copies(g, lambda c: 0):
        copy.wait()
    slot = g % DEC_SLOTS

    @pl.when(step == 0)
    def _():
        q8 = q_ref[0]
        for h in range(N_HEADS):
            qm_s[h * DEC_ROWS:(h + 1) * DEC_ROWS, :] = jnp.where(lane_head == h, q8, 0.0)
        mp_s[...] = jnp.full((rows, LANES), NEG_INF, F32)
        gp_s[...] = jnp.full((rows, LANES), NEG_INF, F32)
        lp_s[...] = jnp.zeros((rows, LANES), F32)

    qmb = qm_s[...].astype(BF16)
    contract_last = (((1,), (1,)), ((), ()))
    kt = jnp.concatenate([kbuf[slot, c] for c in range(pps)], axis=1).astype(BF16)
    vt = jnp.concatenate([vbuf[slot, c] for c in range(pps)], axis=1).astype(BF16)
    s_all = jnp.dot(qmb, kt, preferred_element_type=F32)
    for c in range(nb):
        s = s_all[:, c * MOBA_BLOCK:(c + 1) * MOBA_BLOCK]
        gate = jnp.sum(s, axis=1, keepdims=True) * (1.0 / MOBA_BLOCK)
        m = jnp.max(s, axis=1, keepdims=True)
        p = jnp.exp(s - m)
        l = jnp.sum(p, axis=1, keepdims=True)
        j = step * nb + c
        mp_s[...] = jnp.where(lane == j, m, mp_s[...])
        lp_s[...] = jnp.where(lane == j, l, lp_s[...])
        gp_s[...] = jnp.where(lane == j, gate, gp_s[...])
        pbd_s[c * rows:(c + 1) * rows, c * MOBA_BLOCK:(c + 1) * MOBA_BLOCK] = p.astype(BF16)
    op_s[pl.ds(pl.multiple_of(step * (nb * rows), nb * rows), nb * rows), :] = lax.dot_general(
        pbd_s[...], vt, contract_last, preferred_element_type=F32)

    @pl.when(step == ns - 1)
    def _():
        nblk = ns * nb
        sel = _top_blocks_bias(gp_s[...], lane, LANES, 1) == 0.0
        zpad = jnp.zeros((LANES - DEC_ROWS, WIDTH), F32)
        kn = jnp.concatenate([kn_ref[0], zpad], axis=0).astype(BF16)
        vn = jnp.concatenate([vn_ref[0], zpad], axis=0).astype(BF16)
        s = lax.dot_general(qmb, kn, contract_last, preferred_element_type=F32)
        qpos = jnp.minimum(lax.broadcasted_iota(jnp.int32, (rows, LANES), 0) % DEC_ROWS, t_valid - 1)
        s = jnp.where(lane <= qpos, s, NEG_INF)
        m_own = jnp.max(s, axis=1, keepdims=True)
        p = jnp.exp(s - m_own)
        l_own = jnp.sum(p, axis=1, keepdims=True)
        o_own = jnp.dot(p.astype(BF16), vn, preferred_element_type=F32)
        mp = mp_s[...]
        m_fin = jnp.maximum(m_own, jnp.max(jnp.where(sel, mp, NEG_INF), axis=1, keepdims=True))
        w = jnp.where(sel, jnp.exp(mp - m_fin), 0.0)
        a_own = jnp.exp(m_own - m_fin)
        l_fin = a_own * l_own + jnp.sum(w * lp_s[...], axis=1, keepdims=True)
        o_fin = a_own * o_own
        for j in range(nblk):
            o_fin = o_fin + w[:, j:j + 1] * op_s[j * rows:(j + 1) * rows, :]
        y = o_fin / l_fin
        y_b = None
        for h in range(N_HEADS):
            part = jnp.where(lane_head == h, y[h * DEC_ROWS:(h + 1) * DEC_ROWS, :], 0.0)
            y_b = part if y_b is None else y_b + part
        yb_ref[0] = y_b


def _dec_call(page_table, cache_k, cache_v, layer, q, kn, vn, *, t_valid):
    bsz, n_pages = page_table.shape
    depth, n_pool, page, _, _ = cache_k.shape
    assert (n_pages * page) % MOBA_BLOCK == 0 and MOBA_BLOCK == 2 * page
    nblk = n_pages * page // MOBA_BLOCK
    assert MOBA_TOPK <= nblk <= LANES
    pps = next(c for c in (16, 8, 4, 2) if n_pages % c == 0)
    ns = n_pages // pps
    ck = cache_k.transpose(0, 1, 3, 4, 2).reshape(depth, n_pool, WIDTH, page)
    cv = cache_v.transpose(0, 1, 3, 4, 2).reshape(depth, n_pool, WIDTH, page)

    in_hbm = pl.BlockSpec(memory_space=pl.ANY)
    per_b = pl.BlockSpec((1, DEC_ROWS, WIDTH), lambda b, s, pt: (b, 0, 0))
    rows = N_HEADS * DEC_ROWS
    body = functools.partial(_dec_body, layer=layer, n_seq=bsz, pps=pps, ns=ns, t_valid=t_valid)
    grid_spec = pltpu.PrefetchScalarGridSpec(
        num_scalar_prefetch=1,
        grid=(bsz, ns),
        in_specs=[in_hbm, in_hbm, per_b, per_b, per_b],
        out_specs=per_b,
        scratch_shapes=[pltpu.VMEM((rows, WIDTH), F32), pltpu.VMEM((rows, LANES), F32),
                        pltpu.VMEM((rows, LANES), F32), pltpu.VMEM((rows, LANES), F32),
                        pltpu.VMEM((nblk * rows, WIDTH), F32),
                        pltpu.VMEM((pps // 2 * rows, pps * page), BF16),
                        pltpu.VMEM((DEC_SLOTS, pps, WIDTH, page), F32),
                        pltpu.VMEM((DEC_SLOTS, pps, WIDTH, page), F32),
                        pltpu.SemaphoreType.DMA((2, DEC_SLOTS))],
    )
    return pl.pallas_call(
        body,
        grid_spec=grid_spec,
        out_shape=jax.ShapeDtypeStruct((bsz, DEC_ROWS, WIDTH), F32),
        compiler_params=pltpu.CompilerParams(dimension_semantics=("arbitrary", "arbitrary"),
                                             vmem_limit_bytes=VMEM_LIMIT),
        name="attn_sample",
    )(page_table, ck, cv, q, kn, vn)


def _proj_body(yb_ref, mixed_ref, x_ref, wout_ref, fng_ref, out_ref, *, final):
    o = _out_proj(x_ref[...], mixed_ref[...], yb_ref[...], wout_ref)
    if final:
        o = _rmsnorm(o, fng_ref[...])
    out_ref[...] = o


def _proj_call(y_b, mixed, x, w_out, layer, fng, *, final):
    bsz, t_len, d_model = x.shape
    n = bsz * t_len
    whole = lambda r, c: pl.BlockSpec((r, c), lambda i: (0, 0))
    out = pl.pallas_call(
        functools.partial(_proj_body, final=final),
        grid=(1,),
        in_specs=[whole(n, WIDTH), whole(n, 4 * WIDTH), whole(n, d_model),
                  pl.BlockSpec((None, 4 * WIDTH, d_model), lambda i: (layer, 0, 0)),
                  whole(1, d_model)],
        out_specs=whole(n, d_model),
        out_shape=jax.ShapeDtypeStruct((n, d_model), F32),
        compiler_params=pltpu.CompilerParams(dimension_semantics=("arbitrary",), vmem_limit_bytes=VMEM_LIMIT),
        name="proj_sample",
    )(y_b.reshape(n, WIDTH), mixed.reshape(n, 4 * WIDTH), x.reshape(n, d_model), w_out, fng)
    return out.reshape(bsz, t_len, d_model)


def _rope_tables(pos0, n):
    freqs = ROPE_THETA ** (-jnp.arange(ROT_HALF, dtype=F32) * 2.0 / (2 * ROT_HALF))
    ang = (pos0 + jnp.arange(n, dtype=jnp.int32)).astype(F32)[:, None] * freqs[None, :]
    cos, sin = jnp.cos(ang), jnp.sin(ang)
    one = jnp.ones((n, HEAD_DIM - 2 * ROT_HALF), F32)
    zero8 = jnp.zeros((n, ROT_HALF), F32)
    zero_rest = jnp.zeros((n, HEAD_DIM - 2 * ROT_HALF), F32)
    rc = jnp.concatenate([cos, cos, one], axis=1)
    rsa = jnp.concatenate([-sin, zero8, zero_rest], axis=1)
    rsb = jnp.concatenate([zero8, sin, zero_rest], axis=1)
    return tuple(jnp.tile(a, (1, N_HEADS)) for a in (rc, rsa, rsb))


def _pad_state(state):
    return jnp.pad(state, ((0, 0), (0, 0), (HALO - state.shape[2], 0), (0, 0)))


def kernel(x_prompt, x_sample, cache_k, cache_v, page_table, state_conv_a, state_conv_c, state_pool_d,
           norm_g, w_in, conv_a_w, conv_c_w, conv_c_b, ln_c_g, ln_c_b, w_pw_c, w_pool, pool_scale, w_out,
           final_norm_g):
    depth = w_in.shape[0]
    bsz, seq, d_model = x_prompt.shape
    dec_b, dec_t, _ = x_sample.shape
    past_len = page_table.shape[1] * cache_k.shape[2]
    assert seq % MOBA_BLOCK == 0 and dec_t <= DEC_ROWS and WIDTH == N_HEADS * HEAD_DIM

    pre_tile = next(t for t in (PRE_TILE, MOBA_BLOCK) if seq % t == 0)
    tabs_p = _rope_tables(0, seq)
    tabs_s = _rope_tables(past_len, DEC_ROWS)
    zero_state = jnp.zeros((1, bsz, HALO, WIDTH), F32)
    hist_a, hist_c, hist_d = _pad_state(state_conv_a), _pad_state(state_conv_c), _pad_state(state_pool_d)
    fng = final_norm_g.reshape(1, d_model)
    yp = x_prompt
    ys = jnp.pad(x_sample, ((0, 0), (0, DEC_ROWS - dec_t), (0, 0)))
    outs = [[] for _ in range(8)]
    kstack = jnp.zeros((depth, bsz, WIDTH, seq), F32)
    vstack = jnp.zeros((depth, bsz, WIDTH, seq), F32)
    n_groups = w_pool.shape[1]
    eye = jnp.eye(n_groups, dtype=F32)
    pool_bd = (w_pool[:, :, :, None, :] * eye[None, :, None, :, None]).reshape(depth, WIDTH, WIDTH)
    row = lambda a: a.reshape(depth, 1, a.shape[-1])
    lw = {"norm_g": row(norm_g), "w_in": w_in.astype(BF16), "conv_a_w": conv_a_w, "conv_c_w": conv_c_w,
          "conv_c_b": row(conv_c_b), "ln_c_g": row(ln_c_g), "ln_c_b": row(ln_c_b),
          "w_pw_c": w_pw_c.astype(BF16), "w_pool": pool_bd.astype(BF16), "pool_scale": row(pool_scale)}
    wo = w_out.astype(BF16)
    for l in range(depth):
        final = l == depth - 1

        mixed, kstack, vstack, qt, kb, vt, km, ap, cp, dp = _pre_call(
            yp, lw, l, tabs_p, zero_state, zero_state, zero_state, 0, (kstack, vstack),
            ns=1, tm=pre_tile, last_rows=pre_tile, pos0=0, prompt=True)
        yp = _attn_call(qt, kb, vt, km, mixed, yp, wo, l, fng, final=final)

        mixed_s, ks, vs, qs, as_, cs, ds = _pre_call(
            ys, lw, l, tabs_s, hist_a, hist_c, hist_d, l,
            ns=dec_b, tm=DEC_ROWS, last_rows=dec_t, pos0=past_len, prompt=False)
        yb_s = _dec_call(page_table, cache_k, cache_v, l, qs, ks, vs, t_valid=dec_t)
        ys = _proj_call(yb_s, mixed_s, ys, wo, l, fng, final=final)

        for lst, val in zip(outs, (ks[:, :dec_t].reshape(dec_b, dec_t, N_HEADS, HEAD_DIM),
                                   vs[:, :dec_t].reshape(dec_b, dec_t, N_HEADS, HEAD_DIM),
                                   ap, as_, cp, cs, dp, ds)):
            lst.append(val)
    to_rows = lambda st: st.reshape(depth, bsz, N_HEADS, HEAD_DIM, seq).transpose(0, 1, 4, 2, 3)
    return (yp, ys[:, :dec_t], to_rows(kstack), to_rows(vstack)) + tuple(jnp.stack(o) for o in outs)
```

```python
import functools

import jax
import jax.numpy as jnp
from jax import lax
from jax.experimental import pallas as pl
from jax.experimental.pallas import tpu as pltpu

F32 = jnp.float32
BF16 = jnp.bfloat16

WIDTH = 256
N_HEADS = 4
HEAD_DIM = 64
ROT_HALF = 8
ROPE_THETA = 500000.0
MOBA_BLOCK = 256
MOBA_TOPK = 3
CONV_A_WIDTH = 3
CONV_C_WIDTH = 31
POOL_WINDOWS = (2, 4, 8, 16)
POOL_STATE = 15
RMS_EPS = 1e-6
LN_EPS = 1e-5
HALO = 32
DEC_ROWS = 8
ATTN_GROUP = 4
DEC_SLOTS = 4
PRE_TILE = 1024
NEG_INF = float("-inf")
LOG2_E = 1.4426950408889634
LANES = 128
V7X_VMEM_BYTES = 64 * 1024 * 1024
VMEM_LIMIT = V7X_VMEM_BYTES - 8 * 1024 * 1024


def _silu(z):
    return z * jax.nn.sigmoid(z)


def _rmsnorm(x, g):
    ms = jnp.mean(x * x, axis=-1, keepdims=True)
    return x * lax.rsqrt(ms + RMS_EPS) * g


def _out_proj(x, mixed, y_b, wout_ref):
    yb = (y_b * mixed[:, WIDTH:2 * WIDTH]).astype(BF16)
    d = x.shape[-1]
    o = x + jnp.dot(mixed[:, 0:WIDTH].astype(BF16), wout_ref[0:WIDTH, 0:d], preferred_element_type=F32)
    o = o + jnp.dot(yb, wout_ref[WIDTH:2 * WIDTH, 0:d], preferred_element_type=F32)
    o = o + jnp.dot(mixed[:, 2 * WIDTH:4 * WIDTH].astype(BF16), wout_ref[2 * WIDTH:4 * WIDTH, 0:d],
                    preferred_element_type=F32)
    return o


def _pre_body(x_ref, ng_ref, win_ref, caw_ref, ccw_ref, ccb_ref, lng_ref, lnb_ref, wpw_ref, wpool_ref,
              psc_ref, rc_ref, rsa_ref, rsb_ref, prea_ref, prec_ref, pred_ref, *refs,
              ns, tm, last_rows, pos0, prompt):
    if prompt:
        (_, _, mixed_ref, kst_ref, vst_ref, qt_ref, kb_ref, vt_ref, km_ref,
         na_ref, nc_ref, nd_ref, bufa, bufc, bufd, shc, x2, x4, x8) = refs
    else:
        (mixed_ref, k_ref, v_ref, q_ref, na_ref, nc_ref, nd_ref, bufa, bufc, bufd, shc, x2, x4, x8) = refs
    t = pl.program_id(1)
    nt = pl.num_programs(1)
    span = tm + HALO - 8
    n = ns * tm
    seq = lambda z: z.reshape(ns, tm, WIDTH)
    flat = lambda z: z.reshape(n, WIDTH)

    @pl.when(t == 0)
    def _():
        bufa[:, 0:HALO, :] = prea_ref[...]
        bufc[:, 0:HALO, :] = prec_ref[...]
        bufd[:, 0:HALO, :] = pred_ref[...]

    x = x_ref[...].reshape(n, x_ref.shape[-1])
    h = _rmsnorm(x, ng_ref[...]).astype(BF16)

    def proj(c0, cn):
        return jnp.dot(h, win_ref[:, c0 * WIDTH:(c0 + cn) * WIDTH], preferred_element_type=F32)

    def put(col, val):
        mixed_ref[:, :, col * WIDTH:(col + 1) * WIDTH] = seq(val)

    pa = proj(0, 4)
    bufa[:, HALO:HALO + tm, :] = seq(pa[:, 2 * WIDTH:3 * WIDTH] * pa[:, 0:WIDTH])
    conv = None
    for j in range(CONV_A_WIDTH):
        off = HALO - (CONV_A_WIDTH - 1) + j
        term = caw_ref[j:j + 1, :] * bufa[:, off:off + tm, :]
        conv = term if conv is None else conv + term
    put(0, pa[:, WIDTH:2 * WIDTH] * flat(conv) * _silu(pa[:, 3 * WIDTH:4 * WIDTH]))

    pb = proj(4, 4)
    tab = lambda r: flat(jnp.broadcast_to(r[...][None], (ns, tm, WIDTH)))
    rc, rsa, rsb = tab(rc_ref), tab(rsa_ref), tab(rsb_ref)

    def rope(z):
        return z * rc + pltpu.roll(z, WIDTH - ROT_HALF, 1) * rsa + pltpu.roll(z, ROT_HALF, 1) * rsb

    qs = rope(pb[:, 0:WIDTH]) * (HEAD_DIM ** -0.5)
    kr = rope(pb[:, WIDTH:2 * WIDTH])
    v = pb[:, 2 * WIDTH:3 * WIDTH]
    put(1, _silu(pb[:, 3 * WIDTH:4 * WIDTH]))
    if prompt:
        kst_ref[0] = kr.T
        v_t = v.T
        vst_ref[0] = v_t
        for r in range(tm // MOBA_BLOCK):
            blk = slice(r * MOBA_BLOCK, (r + 1) * MOBA_BLOCK)
            qt_ref[0, r] = qs[blk, :].T
            kb_ref[0, r] = kr[blk, :].astype(BF16)
            vt_ref[0, r] = v_t[:, blk].astype(BF16)
            km_ref[0, r] = jnp.mean(kr[blk, :], axis=0, keepdims=True)
    else:
        k_ref[...] = seq(kr)
        v_ref[...] = seq(v)
        q_ref[...] = seq(qs)

    pc = proj(8, 3)
    bufc[:, HALO:HALO + tm, :] = seq(pc[:, 0:WIDTH] * jax.nn.sigmoid(pc[:, WIDTH:2 * WIDTH]))
    for r in range(1, 8):
        shc[r - 1] = bufc[:, r:r + span, :]
    acc = None
    for j in range(CONV_C_WIDTH):
        a, r = divmod(HALO - (CONV_C_WIDTH - 1) + j, 8)
        rows = bufc[:, 8 * a:8 * a + tm, :] if r == 0 else shc[r - 1, :, 8 * a:8 * a + tm, :]
        term = ccw_ref[j:j + 1, :] * rows
        acc = term if acc is None else acc + term
    acc = flat(acc) + ccb_ref[...]
    mu = jnp.mean(acc, axis=-1, keepdims=True)
    cen = acc - mu
    var = jnp.mean(cen * cen, axis=-1, keepdims=True)
    ln = cen * lax.rsqrt(var + LN_EPS) * lng_ref[...] + lnb_ref[...]
    yc = jnp.dot(_silu(ln).astype(BF16), wpw_ref[...], preferred_element_type=F32)
    put(2, yc * _silu(pc[:, 2 * WIDTH:3 * WIDTH]))

    pd = proj(11, 2)
    dx = pd[:, 0:WIDTH]
    bufd[:, HALO:HALO + tm, :] = seq(dx)
    assert POOL_WINDOWS == (2, 4, 8, 16) and HALO == 32
    x2[...] = bufd[:, 8:8 + span, :] + bufd[:, 7:7 + span, :]
    x4[:, 0:span - 8, :] = x2[:, 8:span, :] + x2[:, 6:span - 2, :]
    x8[:, 0:span - 16, :] = x4[:, 8:span - 8, :] + x4[:, 4:span - 12, :]
    sums = {2: x2[:, 24:24 + tm, :], 4: x4[:, 16:16 + tm, :], 8: x8[:, 8:8 + tm, :],
            16: x8[:, 8:8 + tm, :] + x8[:, 0:tm, :]}
    group = lax.broadcasted_iota(jnp.int32, (ns, tm, WIDTH), 2) // (WIDTH // len(POOL_WINDOWS))
    pos = pos0 + t * tm + lax.broadcasted_iota(jnp.int32, (ns, tm, WIDTH), 1)
    wsum = sums[POOL_WINDOWS[-1]]
    win = jnp.full((ns, tm, WIDTH), POOL_WINDOWS[-1], jnp.int32)
    for g in range(len(POOL_WINDOWS) - 2, -1, -1):
        wsum = jnp.where(group == g, sums[POOL_WINDOWS[g]], wsum)
        win = jnp.where(group == g, POOL_WINDOWS[g], win)
    cnt = jnp.minimum(pos + 1, win).astype(F32)
    diff = flat(wsum / cnt) - dx
    yd = jnp.dot(diff.astype(BF16), wpool_ref[...], preferred_element_type=F32) * psc_ref[...]
    put(3, yd * _silu(pd[:, WIDTH:2 * WIDTH]))

    @pl.when(t == nt - 1)
    def _():
        end = HALO + last_rows
        na_ref[...] = bufa[:, end - (CONV_A_WIDTH - 1):end, :]
        nc_ref[...] = bufc[:, end - (CONV_C_WIDTH - 1):end, :]
        nd_ref[...] = bufd[:, end - POOL_STATE:end, :]

    if tm >= HALO:
        @pl.when(t < nt - 1)
        def _():
            bufa[:, 0:HALO, :] = bufa[:, tm:tm + HALO, :]
            bufc[:, 0:HALO, :] = bufc[:, tm:tm + HALO, :]
            bufd[:, 0:HALO, :] = bufd[:, tm:tm + HALO, :]


def _pre_call(x, lw, layer, rope_tabs, pre_a, pre_c, pre_d, state_layer, kv_stacks=None, *,
              ns, tm, last_rows, pos0, prompt):
    bsz, t_len, d_model = x.shape
    nt = t_len // tm
    assert nt * tm == t_len and (nt == 1 or tm >= HALO) and bsz % ns == 0 and tm % 8 == 0
    rc, rsa, rsb = rope_tabs
    d_in = lw["w_in"].shape[-1]

    def vec(n):
        return pl.BlockSpec((None, 1, n), lambda b, t: (layer, 0, 0))

    def mat(r, c):
        return pl.BlockSpec((None, r, c), lambda b, t: (layer, 0, 0))

    def per_b(shape):
        return pl.BlockSpec((ns,) + shape, lambda b, t: (b,) + (0,) * len(shape))

    hist = pl.BlockSpec((None, ns, HALO, WIDTH), lambda b, t: (state_layer, b, 0, 0))
    tok = lambda w: pl.BlockSpec((ns, tm, w), lambda b, t: (b, t, 0))
    tab = pl.BlockSpec((tm, WIDTH), lambda b, t: (t, 0))
    in_specs = [tok(d_model), vec(d_model), mat(d_model, d_in),
                mat(CONV_A_WIDTH, WIDTH), mat(CONV_C_WIDTH, WIDTH), vec(WIDTH),
                vec(WIDTH), vec(WIDTH), mat(WIDTH, WIDTH), mat(WIDTH, WIDTH),
                vec(WIDTH), tab, tab, tab, hist, hist, hist]
    out_shape = [jax.ShapeDtypeStruct((bsz, t_len, 4 * WIDTH), F32),
                 jax.ShapeDtypeStruct((bsz, t_len, WIDTH), F32),
                 jax.ShapeDtypeStruct((bsz, t_len, WIDTH), F32)]
    out_specs = [tok(4 * WIDTH), tok(WIDTH), tok(WIDTH)]
    operands = [x, lw["norm_g"], lw["w_in"], lw["conv_a_w"], lw["conv_c_w"], lw["conv_c_b"], lw["ln_c_g"],
                lw["ln_c_b"], lw["w_pw_c"], lw["w_pool"], lw["pool_scale"], rc, rsa, rsb, pre_a, pre_c, pre_d]
    aliases = {}
    if prompt:
        stack_spec = pl.BlockSpec((None, 1, WIDTH, tm), lambda b, t: (layer, b, 0, t))
        for pos, stack in enumerate(kv_stacks):
            aliases[len(operands)] = 1 + pos
            operands.append(stack)
            in_specs.append(pl.BlockSpec(memory_space=pl.ANY))
            out_shape[1 + pos] = jax.ShapeDtypeStruct(stack.shape, F32)
            out_specs[1 + pos] = stack_spec
        assert tm % MOBA_BLOCK == 0 and ns == 1
        per_tile = tm // MOBA_BLOCK
        nblk = t_len // MOBA_BLOCK
        blk = lambda r, c: pl.BlockSpec((1, per_tile, r, c), lambda b, t: (b, t, 0, 0))
        out_shape += [jax.ShapeDtypeStruct((bsz, nblk, WIDTH, MOBA_BLOCK), F32),
                      jax.ShapeDtypeStruct((bsz, nblk, MOBA_BLOCK, WIDTH), BF16),
                      jax.ShapeDtypeStruct((bsz, nblk, WIDTH, MOBA_BLOCK), BF16),
                      jax.ShapeDtypeStruct((bsz, nblk, 1, WIDTH), F32)]
        out_specs += [blk(WIDTH, MOBA_BLOCK), blk(MOBA_BLOCK, WIDTH), blk(WIDTH, MOBA_BLOCK), blk(1, WIDTH)]
    else:
        out_shape += [jax.ShapeDtypeStruct((bsz, t_len, WIDTH), F32)]
        out_specs += [tok(WIDTH)]
    out_shape += [jax.ShapeDtypeStruct((bsz, CONV_A_WIDTH - 1, WIDTH), F32),
                  jax.ShapeDtypeStruct((bsz, CONV_C_WIDTH - 1, WIDTH), F32),
                  jax.ShapeDtypeStruct((bsz, POOL_STATE, WIDTH), F32)]
    out_specs += [per_b((CONV_A_WIDTH - 1, WIDTH)), per_b((CONV_C_WIDTH - 1, WIDTH)),
                  per_b((POOL_STATE, WIDTH))]
    body = functools.partial(_pre_body, ns=ns, tm=tm, last_rows=last_rows, pos0=pos0, prompt=prompt)
    return pl.pallas_call(
        body,
        grid=(bsz // ns, nt),
        in_specs=in_specs,
        out_specs=out_specs,
        out_shape=out_shape,
        scratch_shapes=[pltpu.VMEM((ns, HALO + tm, WIDTH), F32)] * 3
        + [pltpu.VMEM((7, ns, HALO + tm - 8, WIDTH), F32)] + [pltpu.VMEM((ns, HALO + tm - 8, WIDTH), F32)] * 3,
        input_output_aliases=aliases,
        compiler_params=pltpu.CompilerParams(dimension_semantics=("arbitrary", "arbitrary"),
                                             vmem_limit_bytes=VMEM_LIMIT),
        name="pre_prompt" if prompt else "pre_sample",
    )(*operands)


def _top_blocks_bias(gate, index, n_index, axis):
    bias = jnp.full(gate.shape, NEG_INF, F32)
    for _ in range(MOBA_TOPK):
        top = jnp.max(gate, axis=axis, keepdims=True)
        cand = jnp.logical_and(gate == top, top > NEG_INF)
        first = jnp.min(jnp.where(cand, index, n_index), axis=axis, keepdims=True)
        pick = index == first
        bias = jnp.where(pick, 0.0, bias)
        gate = jnp.where(pick, NEG_INF, gate)
    return bias


def _attn_body(qt_ref, kb_ref, vt_ref, km_ref, mixed_ref, x_ref, wout_ref, fng_ref, out_ref,
               qm_s, bias_s, acc_s, s_a, s_b, p_a, p_b, al_a, al_b, *, nblk, tq, grp, final):
    s_bufs, p_bufs, al_bufs = (s_a, s_b), (p_a, p_b), (al_a, al_b)
    w4 = N_HEADS * tq
    i = pl.program_id(1)
    qt = qt_ref[0, 0]
    row_head = lax.broadcasted_iota(jnp.int32, (WIDTH, tq), 0) // HEAD_DIM
    km = km_ref[0]
    lane_head = lax.broadcasted_iota(jnp.int32, (nblk, WIDTH), 1) // HEAD_DIM
    blk = lax.broadcasted_iota(jnp.int32, (nblk, tq), 0)
    km_heads = jnp.concatenate([jnp.where(lane_head == h, km, 0.0) for h in range(N_HEADS)], axis=0)
    gates = jnp.dot(km_heads, qt, precision=lax.Precision.HIGHEST, preferred_element_type=F32)
    for h in range(N_HEADS):
        cols = slice(h * tq, (h + 1) * tq)
        qm_s[:, cols] = jnp.where(row_head == h, qt * LOG2_E, 0.0).astype(BF16)
        gate = jnp.where(blk < i, gates[h * nblk:(h + 1) * nblk, :], NEG_INF)
        bias_s[:, cols] = _top_blocks_bias(gate, blk, nblk, 0)

    n_groups = nblk // grp
    sub = [slice(c * MOBA_BLOCK, (c + 1) * MOBA_BLOCK) for c in range(grp)]

    def qk(group, slot):
        kb = kb_ref[0, pl.ds(group * grp, grp)].reshape(grp * MOBA_BLOCK, WIDTH)
        s_bufs[slot][:, 0:w4] = jnp.dot(kb, qm_s[:, 0:w4], preferred_element_type=F32)

    def pv(group, slot):
        for h in range(N_HEADS):
            rows = slice(h * HEAD_DIM, (h + 1) * HEAD_DIM)
            cols = slice(h * tq, (h + 1) * tq)
            vt = jnp.concatenate([vt_ref[0, group * grp + c, rows, :] for c in range(grp)], axis=1)
            o = jnp.dot(vt, p_bufs[slot][:, cols], preferred_element_type=F32)
            acc_s[rows, 0:tq] = al_bufs[slot][:, cols] * acc_s[rows, 0:tq] + o

    qk(0, 1)
    key = lax.broadcasted_iota(jnp.int32, (MOBA_BLOCK, w4), 0)
    qry = lax.broadcasted_iota(jnp.int32, (MOBA_BLOCK, w4), 1) % tq
    s = jnp.dot(kb_ref[0, i], qm_s[:, 0:w4], preferred_element_type=F32)
    s = jnp.where(key <= qry, s, NEG_INF)
    m = jnp.max(s, axis=0, keepdims=True)
    p = jnp.exp2(s - m)
    l = jnp.sum(p, axis=0, keepdims=True)
    pb = p.astype(BF16)
    for h in range(N_HEADS):
        rows = slice(h * HEAD_DIM, (h + 1) * HEAD_DIM)
        acc_s[rows, 0:tq] = jnp.dot(vt_ref[0, i, rows, :], pb[:, h * tq:(h + 1) * tq], preferred_element_type=F32)
    p_bufs[0][:, 0:w4] = jnp.zeros((grp * MOBA_BLOCK, w4), BF16)
    al_bufs[0][...] = jnp.ones((1, w4), F32)

    def half(n, m, l, cur, nxt):
        picked = [bias_s[pl.ds((n - 1) * grp + c, 1), :] for c in range(grp)]
        m_new = m
        for c in range(grp):
            m_new = jnp.maximum(m_new, jnp.max(s_bufs[cur][sub[c], 0:w4], axis=0, keepdims=True) + picked[c])
        alpha = jnp.exp2(m - m_new)
        l = alpha * l
        for c in range(grp):
            p = jnp.exp2(s_bufs[cur][sub[c], 0:w4] - (m_new - picked[c]))
            l = l + jnp.sum(p, axis=0, keepdims=True)
            p_bufs[cur][sub[c], 0:w4] = p.astype(BF16)
        al_bufs[cur][...] = alpha
        pv(jnp.maximum(n - 2, 0), nxt)
        qk(jnp.minimum(n, n_groups - 1), nxt)
        return m_new, l

    def visit(n, carry):
        return lax.cond(n % 2 == 1, lambda: half(n, carry[0], carry[1], 1, 0),
                        lambda: half(n, carry[0], carry[1], 0, 1))

    n_visits = (i + grp - 1) // grp
    m, l = lax.fori_loop(1, n_visits + 1, visit, (m, l))
    odd = n_visits % 2 == 1

    @pl.when(odd)
    def _():
        pv(n_visits - 1, 1)

    @pl.when(jnp.logical_not(odd))
    def _():
        pv(jnp.maximum(n_visits - 1, 0), 0)

    inv = 1.0 / l
    for h in range(N_HEADS):
        rows = slice(h * HEAD_DIM, (h + 1) * HEAD_DIM)
        acc_s[rows, 0:tq] = acc_s[rows, 0:tq] * inv[:, h * tq:(h + 1) * tq]
    y_b = acc_s[:, 0:tq].T
    o = _out_proj(x_ref[0], mixed_ref[0], y_b, wout_ref)
    if final:
        o = _rmsnorm(o, fng_ref[...])
    out_ref[0] = o


def _attn_call(qt, kb, vt, km, mixed, x, w_out, layer, fng, *, final):
    bsz, nblk, _, tq = qt.shape
    d_model = x.shape[-1]
    assert tq == MOBA_BLOCK
    grp = next(g for g in (ATTN_GROUP, 2, 1) if nblk % g == 0)
    body = functools.partial(_attn_body, nblk=nblk, tq=tq, grp=grp, final=final)
    whole = lambda r, c: pl.BlockSpec((1, nblk, r, c), lambda b, i: (b, 0, 0, 0))
    return pl.pallas_call(
        body,
        grid=(bsz, nblk),
        in_specs=[pl.BlockSpec((1, 1, WIDTH, tq), lambda b, i: (b, i, 0, 0)),
                  whole(MOBA_BLOCK, WIDTH), whole(WIDTH, MOBA_BLOCK),
                  pl.BlockSpec((1, nblk, WIDTH), lambda b, i: (b, 0, 0)),
                  pl.BlockSpec((1, tq, 4 * WIDTH), lambda b, i: (b, i, 0)),
                  pl.BlockSpec((1, tq, d_model), lambda b, i: (b, i, 0)),
                  pl.BlockSpec((None,) + w_out.shape[1:], lambda b, i: (layer, 0, 0)),
                  pl.BlockSpec((1, d_model), lambda b, i: (0, 0))],
        out_specs=pl.BlockSpec((1, tq, d_model), lambda b, i: (b, i, 0)),
        out_shape=jax.ShapeDtypeStruct(x.shape, F32),
        scratch_shapes=[pltpu.VMEM((WIDTH, N_HEADS * tq + LANES), BF16),
                        pltpu.VMEM((nblk, N_HEADS * tq), F32),
                        pltpu.VMEM((WIDTH, tq + LANES), F32)]
        + [pltpu.VMEM((grp * MOBA_BLOCK, N_HEADS * tq + LANES), F32)] * 2
        + [pltpu.VMEM((grp * MOBA_BLOCK, N_HEADS * tq + LANES), BF16)] * 2
        + [pltpu.VMEM((1, N_HEADS * tq), F32)] * 2,
        compiler_params=pltpu.CompilerParams(dimension_semantics=("arbitrary", "arbitrary"),
                                             vmem_limit_bytes=VMEM_LIMIT),
        name="attn_prompt",
    )(qt, kb, vt, km.reshape(bsz, nblk, WIDTH), mixed, x, w_out, fng)


def _dec_body(pt_ref, *refs, layer, n_seq, pps, ns, t_valid):
    (ck_ref, cv_ref, q_ref, kn_ref, vn_ref, yb_ref,
     qm_s, mp_s, lp_s, gp_s, op_s, pbd_s, kbuf, vbuf, sems) = refs
    step = pl.program_id(1)
    rows = N_HEADS * DEC_ROWS
    nb = pps // 2
    lane_head = lax.broadcasted_iota(jnp.int32, (DEC_ROWS, WIDTH), 1) // HEAD_DIM
    lane = lax.broadcasted_iota(jnp.int32, (rows, LANES), 1)

    g = pl.program_id(0) * ns + step
    n_steps = pl.num_programs(0) * ns

    def page_copies(gg, page_of):
        slot = gg % DEC_SLOTS
        for c in range(pps):
            page = page_of(c)
            yield pltpu.make_async_copy(ck_ref.at[layer, page], kbuf.at[slot, c], sems.at[0, slot])
            yield pltpu.make_async_copy(cv_ref.at[layer, page], vbuf.at[slot, c], sems.at[1, slot])

    def request(gg):
        seq_id, seq_step = gg // ns, gg % ns
        for copy in page_copies(gg, lambda c: pt_ref[seq_id, seq_step * pps + c]):
            copy.start()

    @pl.when(g == 0)
    def _():
        pbd_s[...] = jnp.zeros(pbd_s.shape, BF16)
        for ahead in range(min(DEC_SLOTS - 1, n_seq * ns)):
            request(ahead)

    @pl.when(g + (DEC_SLOTS - 1) < n_steps)
    def _():
        request(g + (DEC_SLOTS - 1))

    for copy in page_copies(g, lambda c: 0):
        copy.wait()
    slot = g % DEC_SLOTS

    @pl.when(step == 0)
    def _():
        q8 = q_ref[0]
        for h in range(N_HEADS):
            qm_s[h * DEC_ROWS:(h + 1) * DEC_ROWS, :] = jnp.where(lane_head == h, q8, 0.0)
        mp_s[...] = jnp.full((rows, LANES), NEG_INF, F32)
        gp_s[...] = jnp.full((rows, LANES), NEG_INF, F32)
        lp_s[...] = jnp.zeros((rows, LANES), F32)

    qmb = qm_s[...].astype(BF16)
    contract_last = (((1,), (1,)), ((), ()))
    kt = jnp.concatenate([kbuf[slot, c] for c in range(pps)], axis=1).astype(BF16)
    vt = jnp.concatenate([vbuf[slot, c] for c in range(pps)], axis=1).astype(BF16)
    s_all = jnp.dot(qmb, kt, preferred_element_type=F32)
    for c in range(nb):
        s = s_all[:, c * MOBA_BLOCK:(c + 1) * MOBA_BLOCK]
        gate = jnp.sum(s, axis=1, keepdims=True) * (1.0 / MOBA_BLOCK)
        m = jnp.max(s, axis=1, keepdims=True)
        p = jnp.exp(s - m)
        l = jnp.sum(p, axis=1, keepdims=True)
        j = step * nb + c
        mp_s[...] = jnp.where(lane == j, m, mp_s[...])
        lp_s[...] = jnp.where(lane == j, l, lp_s[...])
        gp_s[...] = jnp.where(lane == j, gate, gp_s[...])
        pbd_s[c * rows:(c + 1) * rows, c * MOBA_BLOCK:(c + 1) * MOBA_BLOCK] = p.astype(BF16)
    op_s[pl.ds(pl.multiple_of(step * (nb * rows), nb * rows), nb * rows), :] = lax.dot_general(
        pbd_s[:, 0:nb * MOBA_BLOCK], vt, contract_last, preferred_element_type=F32)

    @pl.when(step == ns - 1)
    def _():
        nblk = ns * nb
        sel = _top_blocks_bias(gp_s[...], lane, LANES, 1) == 0.0
        zpad = jnp.zeros((LANES - DEC_ROWS, WIDTH), F32)
        kn = jnp.concatenate([kn_ref[0], zpad], axis=0).astype(BF16)
        vn = jnp.concatenate([vn_ref[0], zpad], axis=0).astype(BF16)
        s = lax.dot_general(qmb, kn, contract_last, preferred_element_type=F32)
        qpos = jnp.minimum(lax.broadcasted_iota(jnp.int32, (rows, LANES), 0) % DEC_ROWS, t_valid - 1)
        s = jnp.where(lane <= qpos, s, NEG_INF)
        m_own = jnp.max(s, axis=1, keepdims=True)
        p = jnp.exp(s - m_own)
        l_own = jnp.sum(p, axis=1, keepdims=True)
        o_own = jnp.dot(p.astype(BF16), vn, preferred_element_type=F32)
        mp = mp_s[...]
        m_fin = jnp.maximum(m_own, jnp.max(jnp.where(sel, mp, NEG_INF), axis=1, keepdims=True))
        w = jnp.where(sel, jnp.exp(mp - m_fin), 0.0)
        a_own = jnp.exp(m_own - m_fin)
        l_fin = a_own * l_own + jnp.sum(w * lp_s[...], axis=1, keepdims=True)
        o_fin = a_own * o_own
        for j in range(nblk):
            o_fin = o_fin + w[:, j:j + 1] * op_s[j * rows:(j + 1) * rows, :]
        y = o_fin / l_fin
        y_b = None
        for h in range(N_HEADS):
            part = jnp.where(lane_head == h, y[h * DEC_ROWS:(h + 1) * DEC_ROWS, :], 0.0)
            y_b = part if y_b is None else y_b + part
        yb_ref[0] = y_b


def _dec_call(page_table, cache_k, cache_v, layer, q, kn, vn, *, t_valid):
    bsz, n_pages = page_table.shape
    depth, n_pool, page, _, _ = cache_k.shape
    assert (n_pages * page) % MOBA_BLOCK == 0 and MOBA_BLOCK == 2 * page
    nblk = n_pages * page // MOBA_BLOCK
    assert MOBA_TOPK <= nblk <= LANES
    pps = next(c for c in (16, 8, 4, 2) if n_pages % c == 0)
    ns = n_pages // pps
    ck = cache_k.transpose(0, 1, 3, 4, 2).reshape(depth, n_pool, WIDTH, page)
    cv = cache_v.transpose(0, 1, 3, 4, 2).reshape(depth, n_pool, WIDTH, page)

    in_hbm = pl.BlockSpec(memory_space=pl.ANY)
    per_b = pl.BlockSpec((1, DEC_ROWS, WIDTH), lambda b, s, pt: (b, 0, 0))
    rows = N_HEADS * DEC_ROWS
    body = functools.partial(_dec_body, layer=layer, n_seq=bsz, pps=pps, ns=ns, t_valid=t_valid)
    grid_spec = pltpu.PrefetchScalarGridSpec(
        num_scalar_prefetch=1,
        grid=(bsz, ns),
        in_specs=[in_hbm, in_hbm, per_b, per_b, per_b],
        out_specs=per_b,
        scratch_shapes=[pltpu.VMEM((rows, WIDTH), F32), pltpu.VMEM((rows, LANES), F32),
                        pltpu.VMEM((rows, LANES), F32), pltpu.VMEM((rows, LANES), F32),
                        pltpu.VMEM((nblk * rows, WIDTH), F32),
                        pltpu.VMEM((pps // 2 * rows, pps * page + LANES), BF16),
                        pltpu.VMEM((DEC_SLOTS, pps, WIDTH, page), F32),
                        pltpu.VMEM((DEC_SLOTS, pps, WIDTH, page), F32),
                        pltpu.SemaphoreType.DMA((2, DEC_SLOTS))],
    )
    return pl.pallas_call(
        body,
        grid_spec=grid_spec,
        out_shape=jax.ShapeDtypeStruct((bsz, DEC_ROWS, WIDTH), F32),
        compiler_params=pltpu.CompilerParams(dimension_semantics=("arbitrary", "arbitrary"),
                                             vmem_limit_bytes=VMEM_LIMIT),
        name="attn_sample",
    )(page_table, ck, cv, q, kn, vn)


def _proj_body(yb_ref, mixed_ref, x_ref, wout_ref, fng_ref, out_ref, *, final):
    o = _out_proj(x_ref[...], mixed_ref[...], yb_ref[...], wout_ref)
    if final:
        o = _rmsnorm(o, fng_ref[...])
    out_ref[...] = o


def _proj_call(y_b, mixed, x, w_out, layer, fng, *, final):
    bsz, t_len, d_model = x.shape
    n = bsz * t_len
    whole = lambda r, c: pl.BlockSpec((r, c), lambda i: (0, 0))
    out = pl.pallas_call(
        functools.partial(_proj_body, final=final),
        grid=(1,),
        in_specs=[whole(n, WIDTH), whole(n, 4 * WIDTH), whole(n, d_model),
                  pl.BlockSpec((None,) + w_out.shape[1:], lambda i: (layer, 0, 0)),
                  whole(1, d_model)],
        out_specs=whole(n, d_model),
        out_shape=jax.ShapeDtypeStruct((n, d_model), F32),
        compiler_params=pltpu.CompilerParams(dimension_semantics=("arbitrary",), vmem_limit_bytes=VMEM_LIMIT),
        name="proj_sample",
    )(y_b.reshape(n, WIDTH), mixed.reshape(n, 4 * WIDTH), x.reshape(n, d_model), w_out, fng)
    return out.reshape(bsz, t_len, d_model)


def _rope_tables(pos0, n):
    freqs = ROPE_THETA ** (-jnp.arange(ROT_HALF, dtype=F32) * 2.0 / (2 * ROT_HALF))
    ang = (pos0 + jnp.arange(n, dtype=jnp.int32)).astype(F32)[:, None] * freqs[None, :]
    cos, sin = jnp.cos(ang), jnp.sin(ang)
    one = jnp.ones((n, HEAD_DIM - 2 * ROT_HALF), F32)
    zero8 = jnp.zeros((n, ROT_HALF), F32)
    zero_rest = jnp.zeros((n, HEAD_DIM - 2 * ROT_HALF), F32)
    rc = jnp.concatenate([cos, cos, one], axis=1)
    rsa = jnp.concatenate([-sin, zero8, zero_rest], axis=1)
    rsb = jnp.concatenate([zero8, sin, zero_rest], axis=1)
    return tuple(jnp.tile(a, (1, N_HEADS)) for a in (rc, rsa, rsb))


def _pad_state(state):
    return jnp.pad(state, ((0, 0), (0, 0), (HALO - state.shape[2], 0), (0, 0)))


def kernel(x_prompt, x_sample, cache_k, cache_v, page_table, state_conv_a, state_conv_c, state_pool_d,
           norm_g, w_in, conv_a_w, conv_c_w, conv_c_b, ln_c_g, ln_c_b, w_pw_c, w_pool, pool_scale, w_out,
           final_norm_g):
    depth = w_in.shape[0]
    bsz, seq, d_model = x_prompt.shape
    dec_b, dec_t, _ = x_sample.shape
    past_len = page_table.shape[1] * cache_k.shape[2]
    assert seq % MOBA_BLOCK == 0 and dec_t <= DEC_ROWS and WIDTH == N_HEADS * HEAD_DIM

    pre_tile = next(t for t in (PRE_TILE, MOBA_BLOCK) if seq % t == 0)
    tabs_p = _rope_tables(0, seq)
    tabs_s = _rope_tables(past_len, DEC_ROWS)
    zero_state = jnp.zeros((1, bsz, HALO, WIDTH), F32)
    hist_a, hist_c, hist_d = _pad_state(state_conv_a), _pad_state(state_conv_c), _pad_state(state_pool_d)
    fng = final_norm_g.reshape(1, d_model)
    yp = x_prompt
    ys = jnp.pad(x_sample, ((0, 0), (0, DEC_ROWS - dec_t), (0, 0)))
    outs = [[] for _ in range(8)]
    kstack = jnp.zeros((depth, bsz, WIDTH, seq), F32)
    vstack = jnp.zeros((depth, bsz, WIDTH, seq), F32)
    n_groups = w_pool.shape[1]
    eye = jnp.eye(n_groups, dtype=F32)
    pool_bd = (w_pool[:, :, :, None, :] * eye[None, :, None, :, None]).reshape(depth, WIDTH, WIDTH)
    row = lambda a: a.reshape(depth, 1, a.shape[-1])
    spare = lambda w: jnp.pad(w.astype(BF16), ((0, 0), (0, 0), (0, LANES)))
    lw = {"norm_g": row(norm_g), "w_in": spare(w_in), "conv_a_w": conv_a_w, "conv_c_w": conv_c_w,
          "conv_c_b": row(conv_c_b), "ln_c_g": row(ln_c_g), "ln_c_b": row(ln_c_b),
          "w_pw_c": w_pw_c.astype(BF16), "w_pool": pool_bd.astype(BF16), "pool_scale": row(pool_scale)}
    wo = spare(w_out)
    for l in range(depth):
        final = l == depth - 1

        mixed, kstack, vstack, qt, kb, vt, km, ap, cp, dp = _pre_call(
            yp, lw, l, tabs_p, zero_state, zero_state, zero_state, 0, (kstack, vstack),
            ns=1, tm=pre_tile, last_rows=pre_tile, pos0=0, prompt=True)
        yp = _attn_call(qt, kb, vt, km, mixed, yp, wo, l, fng, final=final)

        mixed_s, ks, vs, qs, as_, cs, ds = _pre_call(
            ys, lw, l, tabs_s, hist_a, hist_c, hist_d, l,
            ns=dec_b, tm=DEC_ROWS, last_rows=dec_t, pos0=past_len, prompt=False)
        yb_s = _dec_call(page_table, cache_k, cache_v, l, qs, ks, vs, t_valid=dec_t)
        ys = _proj_call(yb_s, mixed_s, ys, wo, l, fng, final=final)

        for lst, val in zip(outs, (ks[:, :dec_t].reshape(dec_b, dec_t, N_HEADS, HEAD_DIM),
                                   vs[:, :dec_t].reshape(dec_b, dec_t, N_HEADS, HEAD_DIM),
                                   ap, as_, cp, cs, dp, ds)):
            lst.append(val)
    to_rows = lambda st: st.reshape(depth, bsz, N_HEADS, HEAD_DIM, seq).transpose(0, 1, 4, 2, 3)
    return (yp, ys[:, :dec_t], to_rows(kstack), to_rows(vstack)) + tuple(jnp.stack(o) for o in outs)
```

```python
import functools

import jax
import jax.numpy as jnp
from jax import lax
from jax.experimental import pallas as pl
from jax.experimental.pallas import tpu as pltpu

F32 = jnp.float32
BF16 = jnp.bfloat16

WIDTH = 256
N_HEADS = 4
HEAD_DIM = 64
ROT_HALF = 8
ROPE_THETA = 500000.0
MOBA_BLOCK = 256
MOBA_TOPK = 3
CONV_A_WIDTH = 3
CONV_C_WIDTH = 31
POOL_WINDOWS = (2, 4, 8, 16)
POOL_STATE = 15
RMS_EPS = 1e-6
LN_EPS = 1e-5
HALO = 32
DEC_ROWS = 8
ATTN_GROUP = 4
DEC_SLOTS = 5
PRE_TILE = 1024
NEG_INF = float("-inf")
LOG2_E = 1.4426950408889634
LANES = 128
V7X_VMEM_BYTES = 64 * 1024 * 1024
VMEM_LIMIT = V7X_VMEM_BYTES - 8 * 1024 * 1024


def _silu(z):
    return z * jax.nn.sigmoid(z)


def _rmsnorm(x, g):
    ms = jnp.mean(x * x, axis=-1, keepdims=True)
    return x * lax.rsqrt(ms + RMS_EPS) * g


def _out_proj(x, mixed, y_b, wout_ref):
    yb = (y_b * mixed[:, WIDTH:2 * WIDTH]).astype(BF16)
    o = x + jnp.dot(mixed[:, 0:WIDTH].astype(BF16), wout_ref[0:WIDTH, :], preferred_element_type=F32)
    o = o + jnp.dot(yb, wout_ref[WIDTH:2 * WIDTH, :], preferred_element_type=F32)
    o = o + jnp.dot(mixed[:, 2 * WIDTH:4 * WIDTH].astype(BF16), wout_ref[2 * WIDTH:4 * WIDTH, :],
                    preferred_element_type=F32)
    return o


def _pre_body(x_ref, ng_ref, win_ref, caw_ref, ccw_ref, ccb_ref, lng_ref, lnb_ref, wpw_ref, wpool_ref,
              psc_ref, rc_ref, rsa_ref, rsb_ref, prea_ref, prec_ref, pred_ref, *refs,
              ns, tm, last_rows, pos0, prompt):
    if prompt:
        (_, _, mixed_ref, kst_ref, vst_ref, qt_ref, kb_ref, vt_ref, km_ref,
         na_ref, nc_ref, nd_ref, bufa, bufc, bufd, shc, x2, x4, x8) = refs
    else:
        (mixed_ref, k_ref, v_ref, q_ref, na_ref, nc_ref, nd_ref, bufa, bufc, bufd, shc, x2, x4, x8) = refs
    t = pl.program_id(1)
    nt = pl.num_programs(1)
    span = tm + HALO - 8
    n = ns * tm
    seq = lambda z: z.reshape(ns, tm, WIDTH)
    flat = lambda z: z.reshape(n, WIDTH)

    @pl.when(t == 0)
    def _():
        bufa[:, 0:HALO, :] = prea_ref[...]
        bufc[:, 0:HALO, :] = prec_ref[...]
        bufd[:, 0:HALO, :] = pred_ref[...]

    x = x_ref[...].reshape(n, x_ref.shape[-1])
    h = _rmsnorm(x, ng_ref[...]).astype(BF16)

    def proj(c0, cn):
        return jnp.dot(h, win_ref[:, c0 * WIDTH:(c0 + cn) * WIDTH], preferred_element_type=F32)

    def put(col, val):
        mixed_ref[:, :, col * WIDTH:(col + 1) * WIDTH] = seq(val)

    pa = proj(0, 4)
    bufa[:, HALO:HALO + tm, :] = seq(pa[:, 2 * WIDTH:3 * WIDTH] * pa[:, 0:WIDTH])
    conv = None
    for j in range(CONV_A_WIDTH):
        off = HALO - (CONV_A_WIDTH - 1) + j
        term = caw_ref[j:j + 1, :] * bufa[:, off:off + tm, :]
        conv = term if conv is None else conv + term
    put(0, pa[:, WIDTH:2 * WIDTH] * flat(conv) * _silu(pa[:, 3 * WIDTH:4 * WIDTH]))

    pb = proj(4, 4)
    tab = lambda r: flat(jnp.broadcast_to(r[...][None], (ns, tm, WIDTH)))
    rc, rsa, rsb = tab(rc_ref), tab(rsa_ref), tab(rsb_ref)

    def rope(z):
        return z * rc + pltpu.roll(z, WIDTH - ROT_HALF, 1) * rsa + pltpu.roll(z, ROT_HALF, 1) * rsb

    qs = rope(pb[:, 0:WIDTH]) * (HEAD_DIM ** -0.5)
    kr = rope(pb[:, WIDTH:2 * WIDTH])
    v = pb[:, 2 * WIDTH:3 * WIDTH]
    put(1, _silu(pb[:, 3 * WIDTH:4 * WIDTH]))
    if prompt:
        kst_ref[0] = kr.T
        v_t = v.T
        vst_ref[0] = v_t
        for r in range(tm // MOBA_BLOCK):
            blk = slice(r * MOBA_BLOCK, (r + 1) * MOBA_BLOCK)
            qt_ref[0, r] = qs[blk, :].T
            kb_ref[0, r] = jnp.concatenate([kr[blk, :].astype(BF16), jnp.zeros((MOBA_BLOCK, LANES), BF16)], axis=1)
            vt_ref[0, r] = v_t[:, blk].astype(BF16)
            km_ref[0, r] = jnp.mean(kr[blk, :], axis=0, keepdims=True)
    else:
        k_ref[...] = seq(kr)
        v_ref[...] = seq(v)
        q_ref[...] = seq(qs)

    pc = proj(8, 3)
    bufc[:, HALO:HALO + tm, :] = seq(pc[:, 0:WIDTH] * jax.nn.sigmoid(pc[:, WIDTH:2 * WIDTH]))
    for r in range(1, 8):
        shc[r - 1] = bufc[:, r:r + span, :]
    acc = None
    for j in range(CONV_C_WIDTH):
        a, r = divmod(HALO - (CONV_C_WIDTH - 1) + j, 8)
        rows = bufc[:, 8 * a:8 * a + tm, :] if r == 0 else shc[r - 1, :, 8 * a:8 * a + tm, :]
        term = ccw_ref[j:j + 1, :] * rows
        acc = term if acc is None else acc + term
    acc = flat(acc) + ccb_ref[...]
    mu = jnp.mean(acc, axis=-1, keepdims=True)
    cen = acc - mu
    var = jnp.mean(cen * cen, axis=-1, keepdims=True)
    ln = cen * lax.rsqrt(var + LN_EPS) * lng_ref[...] + lnb_ref[...]
    yc = jnp.dot(_silu(ln).astype(BF16), wpw_ref[...], preferred_element_type=F32)
    put(2, yc * _silu(pc[:, 2 * WIDTH:3 * WIDTH]))

    pd = proj(11, 2)
    dx = pd[:, 0:WIDTH]
    bufd[:, HALO:HALO + tm, :] = seq(dx)
    assert POOL_WINDOWS == (2, 4, 8, 16) and HALO == 32
    x2[...] = bufd[:, 8:8 + span, :] + bufd[:, 7:7 + span, :]
    x4[:, 0:span - 8, :] = x2[:, 8:span, :] + x2[:, 6:span - 2, :]
    x8[:, 0:span - 16, :] = x4[:, 8:span - 8, :] + x4[:, 4:span - 12, :]
    sums = {2: x2[:, 24:24 + tm, :], 4: x4[:, 16:16 + tm, :], 8: x8[:, 8:8 + tm, :],
            16: x8[:, 8:8 + tm, :] + x8[:, 0:tm, :]}
    group = lax.broadcasted_iota(jnp.int32, (ns, tm, WIDTH), 2) // (WIDTH // len(POOL_WINDOWS))
    pos = pos0 + t * tm + lax.broadcasted_iota(jnp.int32, (ns, tm, WIDTH), 1)
    wsum = sums[POOL_WINDOWS[-1]]
    win = jnp.full((ns, tm, WIDTH), POOL_WINDOWS[-1], jnp.int32)
    for g in range(len(POOL_WINDOWS) - 2, -1, -1):
        wsum = jnp.where(group == g, sums[POOL_WINDOWS[g]], wsum)
        win = jnp.where(group == g, POOL_WINDOWS[g], win)
    cnt = jnp.minimum(pos + 1, win).astype(F32)
    diff = flat(wsum / cnt) - dx
    yd = jnp.dot(diff.astype(BF16), wpool_ref[...], preferred_element_type=F32) * psc_ref[...]
    put(3, yd * _silu(pd[:, WIDTH:2 * WIDTH]))

    @pl.when(t == nt - 1)
    def _():
        end = HALO + last_rows
        na_ref[...] = bufa[:, end - (CONV_A_WIDTH - 1):end, :]
        nc_ref[...] = bufc[:, end - (CONV_C_WIDTH - 1):end, :]
        nd_ref[...] = bufd[:, end - POOL_STATE:end, :]

    if tm >= HALO:
        @pl.when(t < nt - 1)
        def _():
            bufa[:, 0:HALO, :] = bufa[:, tm:tm + HALO, :]
            bufc[:, 0:HALO, :] = bufc[:, tm:tm + HALO, :]
            bufd[:, 0:HALO, :] = bufd[:, tm:tm + HALO, :]


def _pre_call(x, lw, layer, rope_tabs, pre_a, pre_c, pre_d, state_layer, kv_stacks=None, *,
              ns, tm, last_rows, pos0, prompt):
    bsz, t_len, d_model = x.shape
    nt = t_len // tm
    assert nt * tm == t_len and (nt == 1 or tm >= HALO) and bsz % ns == 0 and tm % 8 == 0
    rc, rsa, rsb = rope_tabs
    d_in = lw["w_in"].shape[-1]

    def vec(n):
        return pl.BlockSpec((None, 1, n), lambda b, t: (layer, 0, 0))

    def mat(r, c):
        return pl.BlockSpec((None, r, c), lambda b, t: (layer, 0, 0))

    def per_b(shape):
        return pl.BlockSpec((ns,) + shape, lambda b, t: (b,) + (0,) * len(shape))

    hist = pl.BlockSpec((None, ns, HALO, WIDTH), lambda b, t: (state_layer, b, 0, 0))
    tok = lambda w: pl.BlockSpec((ns, tm, w), lambda b, t: (b, t, 0))
    tab = pl.BlockSpec((tm, WIDTH), lambda b, t: (t, 0))
    in_specs = [tok(d_model), vec(d_model), mat(d_model, d_in),
                mat(CONV_A_WIDTH, WIDTH), mat(CONV_C_WIDTH, WIDTH), vec(WIDTH),
                vec(WIDTH), vec(WIDTH), mat(WIDTH, WIDTH), mat(WIDTH, WIDTH),
                vec(WIDTH), tab, tab, tab, hist, hist, hist]
    out_shape = [jax.ShapeDtypeStruct((bsz, t_len, 4 * WIDTH), F32),
                 jax.ShapeDtypeStruct((bsz, t_len, WIDTH), F32),
                 jax.ShapeDtypeStruct((bsz, t_len, WIDTH), F32)]
    out_specs = [tok(4 * WIDTH), tok(WIDTH), tok(WIDTH)]
    operands = [x, lw["norm_g"], lw["w_in"], lw["conv_a_w"], lw["conv_c_w"], lw["conv_c_b"], lw["ln_c_g"],
                lw["ln_c_b"], lw["w_pw_c"], lw["w_pool"], lw["pool_scale"], rc, rsa, rsb, pre_a, pre_c, pre_d]
    aliases = {}
    if prompt:
        stack_spec = pl.BlockSpec((None, 1, WIDTH, tm), lambda b, t: (layer, b, 0, t))
        for pos, stack in enumerate(kv_stacks):
            aliases[len(operands)] = 1 + pos
            operands.append(stack)
            in_specs.append(pl.BlockSpec(memory_space=pl.ANY))
            out_shape[1 + pos] = jax.ShapeDtypeStruct(stack.shape, F32)
            out_specs[1 + pos] = stack_spec
        assert tm % MOBA_BLOCK == 0 and ns == 1
        per_tile = tm // MOBA_BLOCK
        nblk = t_len // MOBA_BLOCK
        blk = lambda r, c: pl.BlockSpec((1, per_tile, r, c), lambda b, t: (b, t, 0, 0))
        out_shape += [jax.ShapeDtypeStruct((bsz, nblk, WIDTH, MOBA_BLOCK), F32),
                      jax.ShapeDtypeStruct((bsz, nblk, MOBA_BLOCK, WIDTH + LANES), BF16),
                      jax.ShapeDtypeStruct((bsz, nblk, WIDTH, MOBA_BLOCK), BF16),
                      jax.ShapeDtypeStruct((bsz, nblk, 1, WIDTH), F32)]
        out_specs += [blk(WIDTH, MOBA_BLOCK), blk(MOBA_BLOCK, WIDTH + LANES), blk(WIDTH, MOBA_BLOCK), blk(1, WIDTH)]
    else:
        out_shape += [jax.ShapeDtypeStruct((bsz, t_len, WIDTH), F32)]
        out_specs += [tok(WIDTH)]
    out_shape += [jax.ShapeDtypeStruct((bsz, CONV_A_WIDTH - 1, WIDTH), F32),
                  jax.ShapeDtypeStruct((bsz, CONV_C_WIDTH - 1, WIDTH), F32),
                  jax.ShapeDtypeStruct((bsz, POOL_STATE, WIDTH), F32)]
    out_specs += [per_b((CONV_A_WIDTH - 1, WIDTH)), per_b((CONV_C_WIDTH - 1, WIDTH)),
                  per_b((POOL_STATE, WIDTH))]
    body = functools.partial(_pre_body, ns=ns, tm=tm, last_rows=last_rows, pos0=pos0, prompt=prompt)
    return pl.pallas_call(
        body,
        grid=(bsz // ns, nt),
        in_specs=in_specs,
        out_specs=out_specs,
        out_shape=out_shape,
        scratch_shapes=[pltpu.VMEM((ns, HALO + tm, WIDTH), F32)] * 3
        + [pltpu.VMEM((7, ns, HALO + tm - 8, WIDTH), F32)] + [pltpu.VMEM((ns, HALO + tm - 8, WIDTH), F32)] * 3,
        input_output_aliases=aliases,
        compiler_params=pltpu.CompilerParams(dimension_semantics=("arbitrary", "arbitrary"),
                                             vmem_limit_bytes=VMEM_LIMIT),
        name="pre_prompt" if prompt else "pre_sample",
    )(*operands)


def _top_blocks_bias(gate, index, n_index, axis):
    bias = jnp.full(gate.shape, NEG_INF, F32)
    for _ in range(MOBA_TOPK):
        top = jnp.max(gate, axis=axis, keepdims=True)
        cand = jnp.logical_and(gate == top, top > NEG_INF)
        first = jnp.min(jnp.where(cand, index, n_index), axis=axis, keepdims=True)
        pick = index == first
        bias = jnp.where(pick, 0.0, bias)
        gate = jnp.where(pick, NEG_INF, gate)
    return bias


def _attn_body(qt_ref, kb_ref, vt_ref, km_ref, mixed_ref, x_ref, wout_ref, fng_ref, out_ref,
               qm_s, bias_s, acc_s, s_a, s_b, p_a, p_b, al_a, al_b, *, nblk, tq, grp, final):
    s_bufs, p_bufs, al_bufs = (s_a, s_b), (p_a, p_b), (al_a, al_b)
    w4 = N_HEADS * tq
    i = pl.program_id(1)
    qt = qt_ref[0, 0]
    row_head = lax.broadcasted_iota(jnp.int32, (WIDTH, tq), 0) // HEAD_DIM
    km = km_ref[0]
    lane_head = lax.broadcasted_iota(jnp.int32, (nblk, WIDTH), 1) // HEAD_DIM
    blk = lax.broadcasted_iota(jnp.int32, (nblk, tq), 0)
    km_heads = jnp.concatenate([jnp.where(lane_head == h, km, 0.0) for h in range(N_HEADS)], axis=0)
    gates = jnp.dot(km_heads, qt, precision=lax.Precision.HIGHEST, preferred_element_type=F32)
    for h in range(N_HEADS):
        cols = slice(h * tq, (h + 1) * tq)
        qm_s[:, cols] = jnp.where(row_head == h, qt * LOG2_E, 0.0).astype(BF16)
        gate = jnp.where(blk < i, gates[h * nblk:(h + 1) * nblk, :], NEG_INF)
        bias_s[:, cols] = _top_blocks_bias(gate, blk, nblk, 0)

    n_groups = nblk // grp
    sub = [slice(c * MOBA_BLOCK, (c + 1) * MOBA_BLOCK) for c in range(grp)]

    def qk(group, slot):
        kb = kb_ref[0, pl.ds(group * grp, grp), :, 0:WIDTH].reshape(grp * MOBA_BLOCK, WIDTH)
        s_bufs[slot][:, 0:w4] = jnp.dot(kb, qm_s[:, 0:w4], preferred_element_type=F32)

    def pv(group, slot):
        for h in range(N_HEADS):
            rows = slice(h * HEAD_DIM, (h + 1) * HEAD_DIM)
            cols = slice(h * tq, (h + 1) * tq)
            vt = jnp.concatenate([vt_ref[0, group * grp + c, rows, :] for c in range(grp)], axis=1)
            o = jnp.dot(vt, p_bufs[slot][:, cols], preferred_element_type=F32)
            acc_s[rows, :] = al_bufs[slot][:, cols] * acc_s[rows, :] + o

    qk(0, 1)
    key = lax.broadcasted_iota(jnp.int32, (MOBA_BLOCK, w4), 0)
    qry = lax.broadcasted_iota(jnp.int32, (MOBA_BLOCK, w4), 1) % tq
    s = jnp.dot(kb_ref[0, i, :, 0:WIDTH], qm_s[:, 0:w4], preferred_element_type=F32)
    s = jnp.where(key <= qry, s, NEG_INF)
    m = jnp.max(s, axis=0, keepdims=True)
    p = jnp.exp2(s - m)
    l = jnp.sum(p, axis=0, keepdims=True)
    pb = p.astype(BF16)
    for h in range(N_HEADS):
        rows = slice(h * HEAD_DIM, (h + 1) * HEAD_DIM)
        acc_s[rows, :] = jnp.dot(vt_ref[0, i, rows, :], pb[:, h * tq:(h + 1) * tq], preferred_element_type=F32)
    p_bufs[0][:, 0:w4] = jnp.zeros((grp * MOBA_BLOCK, w4), BF16)
    al_bufs[0][...] = jnp.ones((1, w4), F32)

    def half(n, m, l, cur, nxt):
        picked = [bias_s[pl.ds((n - 1) * grp + c, 1), :] for c in range(grp)]
        m_new = m
        for c in range(grp):
            m_new = jnp.maximum(m_new, jnp.max(s_bufs[cur][sub[c], 0:w4], axis=0, keepdims=True) + picked[c])
        alpha = jnp.exp2(m - m_new)
        l = alpha * l
        for c in range(grp):
            p = jnp.exp2(s_bufs[cur][sub[c], 0:w4] - (m_new - picked[c]))
            l = l + jnp.sum(p, axis=0, keepdims=True)
            p_bufs[cur][sub[c], 0:w4] = p.astype(BF16)
        al_bufs[cur][...] = alpha
        pv(jnp.maximum(n - 2, 0), nxt)
        qk(jnp.minimum(n, n_groups - 1), nxt)
        return m_new, l

    def visit(n, carry):
        return lax.cond(n % 2 == 1, lambda: half(n, carry[0], carry[1], 1, 0),
                        lambda: half(n, carry[0], carry[1], 0, 1))

    n_visits = (i + grp - 1) // grp
    m, l = lax.fori_loop(1, n_visits + 1, visit, (m, l))
    odd = n_visits % 2 == 1

    @pl.when(odd)
    def _():
        pv(n_visits - 1, 1)

    @pl.when(jnp.logical_not(odd))
    def _():
        pv(jnp.maximum(n_visits - 1, 0), 0)

    inv = 1.0 / l
    for h in range(N_HEADS):
        rows = slice(h * HEAD_DIM, (h + 1) * HEAD_DIM)
        acc_s[rows, :] = acc_s[rows, :] * inv[:, h * tq:(h + 1) * tq]
    y_b = acc_s[...].T
    o = _out_proj(x_ref[0], mixed_ref[0], y_b, wout_ref)
    if final:
        o = _rmsnorm(o, fng_ref[...])
    out_ref[0] = o


def _attn_call(qt, kb, vt, km, mixed, x, w_out, layer, fng, *, final):
    bsz, nblk, _, tq = qt.shape
    d_model = x.shape[-1]
    assert tq == MOBA_BLOCK
    grp = next(g for g in (ATTN_GROUP, 2, 1) if nblk % g == 0)
    body = functools.partial(_attn_body, nblk=nblk, tq=tq, grp=grp, final=final)
    whole = lambda r, c: pl.BlockSpec((1, nblk, r, c), lambda b, i: (b, 0, 0, 0))
    return pl.pallas_call(
        body,
        grid=(bsz, nblk),
        in_specs=[pl.BlockSpec((1, 1, WIDTH, tq), lambda b, i: (b, i, 0, 0)),
                  whole(MOBA_BLOCK, WIDTH + LANES), whole(WIDTH, MOBA_BLOCK),
                  pl.BlockSpec((1, nblk, WIDTH), lambda b, i: (b, 0, 0)),
                  pl.BlockSpec((1, tq, 4 * WIDTH), lambda b, i: (b, i, 0)),
                  pl.BlockSpec((1, tq, d_model), lambda b, i: (b, i, 0)),
                  pl.BlockSpec((None, 4 * WIDTH, d_model), lambda b, i: (layer, 0, 0)),
                  pl.BlockSpec((1, d_model), lambda b, i: (0, 0))],
        out_specs=pl.BlockSpec((1, tq, d_model), lambda b, i: (b, i, 0)),
        out_shape=jax.ShapeDtypeStruct(x.shape, F32),
        scratch_shapes=[pltpu.VMEM((WIDTH, N_HEADS * tq + LANES), BF16),
                        pltpu.VMEM((nblk, N_HEADS * tq), F32),
                        pltpu.VMEM((WIDTH, tq), F32)]
        + [pltpu.VMEM((grp * MOBA_BLOCK, N_HEADS * tq + LANES), F32)] * 2
        + [pltpu.VMEM((grp * MOBA_BLOCK, N_HEADS * tq + LANES), BF16)] * 2
        + [pltpu.VMEM((1, N_HEADS * tq), F32)] * 2,
        compiler_params=pltpu.CompilerParams(dimension_semantics=("arbitrary", "arbitrary"),
                                             vmem_limit_bytes=VMEM_LIMIT),
        name="attn_prompt",
    )(qt, kb, vt, km.reshape(bsz, nblk, WIDTH), mixed, x, w_out, fng)


def _dec_body(pt_ref, *refs, layer, n_seq, pps, ns, t_valid):
    (ck_ref, cv_ref, q_ref, kn_ref, vn_ref, yb_ref,
     qm_s, mp_s, lp_s, gp_s, op_s, pbd_s, kbuf, vbuf, sems) = refs
    step = pl.program_id(1)
    rows = N_HEADS * DEC_ROWS
    nb = pps // 2
    lane_head = lax.broadcasted_iota(jnp.int32, (DEC_ROWS, WIDTH), 1) // HEAD_DIM
    lane = lax.broadcasted_iota(jnp.int32, (rows, LANES), 1)

    g = pl.program_id(0) * ns + step
    n_steps = pl.num_programs(0) * ns

    def page_copies(gg, page_of):
        slot = gg % DEC_SLOTS
        for c in range(pps):
            page = page_of(c)
            yield pltpu.make_async_copy(ck_ref.at[layer, page], kbuf.at[slot, c], sems.at[0, slot])
            yield pltpu.make_async_copy(cv_ref.at[layer, page], vbuf.at[slot, c], sems.at[1, slot])

    def request(gg):
        seq_id, seq_step = gg // ns, gg % ns
        for copy in page_copies(gg, lambda c: pt_ref[seq_id, seq_step * pps + c]):
            copy.start()

    @pl.when(g == 0)
    def _():
        pbd_s[...] = jnp.zeros(pbd_s.shape, BF16)
        for ahead in range(min(DEC_SLOTS - 1, n_seq * ns)):
            request(ahead)

    @pl.when(g + (DEC_SLOTS - 1) < n_steps)
    def _():
        request(g + (DEC_SLOTS - 1))

    for copy in page_copies(g, lambda c: 0):
        copy.wait()
    slot = g % DEC_SLOTS

    @pl.when(step == 0)
    def _():
        q8 = q_ref[0]
        for h in range(N_HEADS):
            qm_s[h * DEC_ROWS:(h + 1) * DEC_ROWS, :] = jnp.where(lane_head == h, q8, 0.0)
        mp_s[...] = jnp.full((rows, LANES), NEG_INF, F32)
        gp_s[...] = jnp.full((rows, LANES), NEG_INF, F32)
        lp_s[...] = jnp.zeros((rows, LANES), F32)

    qmb = qm_s[...].astype(BF16)
    contract_last = (((1,), (1,)), ((), ()))
    kt = jnp.concatenate([kbuf[slot, c] for c in range(pps)], axis=1).astype(BF16)
    vt = jnp.concatenate([vbuf[slot, c] for c in range(pps)], axis=1).astype(BF16)
    s_all = jnp.dot(qmb, kt, preferred_element_type=F32)
    for c in range(nb):
        s = s_all[:, c * MOBA_BLOCK:(c + 1) * MOBA_BLOCK]
        gate = jnp.sum(s, axis=1, keepdims=True) * (1.0 / MOBA_BLOCK)
        m = jnp.max(s, axis=1, keepdims=True)
        p = jnp.exp(s - m)
        l = jnp.sum(p, axis=1, keepdims=True)
        j = step * nb + c
        mp_s[...] = jnp.where(lane == j, m, mp_s[...])
        lp_s[...] = jnp.where(lane == j, l, lp_s[...])
        gp_s[...] = jnp.where(lane == j, gate, gp_s[...])
        pbd_s[c * rows:(c + 1) * rows, c * MOBA_BLOCK:(c + 1) * MOBA_BLOCK] = p.astype(BF16)
    op_s[pl.ds(pl.multiple_of(step * (nb * rows), nb * rows), nb * rows), :] = lax.dot_general(
        pbd_s[...], vt, contract_last, preferred_element_type=F32)

    @pl.when(step == ns - 1)
    def _():
        nblk = ns * nb
        sel = _top_blocks_bias(gp_s[...], lane, LANES, 1) == 0.0
        zpad = jnp.zeros((LANES - DEC_ROWS, WIDTH), F32)
        kn = jnp.concatenate([kn_ref[0], zpad], axis=0).astype(BF16)
        vn = jnp.concatenate([vn_ref[0], zpad], axis=0).astype(BF16)
        s = lax.dot_general(qmb, kn, contract_last, preferred_element_type=F32)
        qpos = jnp.minimum(lax.broadcasted_iota(jnp.int32, (rows, LANES), 0) % DEC_ROWS, t_valid - 1)
        s = jnp.where(lane <= qpos, s, NEG_INF)
        m_own = jnp.max(s, axis=1, keepdims=True)
        p = jnp.exp(s - m_own)
        l_own = jnp.sum(p, axis=1, keepdims=True)
        o_own = jnp.dot(p.astype(BF16), vn, preferred_element_type=F32)
        mp = mp_s[...]
        m_fin = jnp.maximum(m_own, jnp.max(jnp.where(sel, mp, NEG_INF), axis=1, keepdims=True))
        w = jnp.where(sel, jnp.exp(mp - m_fin), 0.0)
        a_own = jnp.exp(m_own - m_fin)
        l_fin = a_own * l_own + jnp.sum(w * lp_s[...], axis=1, keepdims=True)
        o_fin = a_own * o_own
        for j in range(nblk):
            o_fin = o_fin + w[:, j:j + 1] * op_s[j * rows:(j + 1) * rows, :]
        y = o_fin / l_fin
        y_b = None
        for h in range(N_HEADS):
            part = jnp.where(lane_head == h, y[h * DEC_ROWS:(h + 1) * DEC_ROWS, :], 0.0)
            y_b = part if y_b is None else y_b + part
        yb_ref[0] = y_b


def _dec_call(page_table, cache_k, cache_v, layer, q, kn, vn, *, t_valid):
    bsz, n_pages = page_table.shape
    depth, n_pool, page, _, _ = cache_k.shape
    assert (n_pages * page) % MOBA_BLOCK == 0 and MOBA_BLOCK == 2 * page
    nblk = n_pages * page // MOBA_BLOCK
    assert MOBA_TOPK <= nblk <= LANES
    pps = next(c for c in (16, 8, 4, 2) if n_pages % c == 0)
    ns = n_pages // pps
    ck = cache_k.transpose(0, 1, 3, 4, 2).reshape(depth, n_pool, WIDTH, page)
    cv = cache_v.transpose(0, 1, 3, 4, 2).reshape(depth, n_pool, WIDTH, page)

    in_hbm = pl.BlockSpec(memory_space=pl.ANY)
    per_b = pl.BlockSpec((1, DEC_ROWS, WIDTH), lambda b, s, pt: (b, 0, 0))
    rows = N_HEADS * DEC_ROWS
    body = functools.partial(_dec_body, layer=layer, n_seq=bsz, pps=pps, ns=ns, t_valid=t_valid)
    grid_spec = pltpu.PrefetchScalarGridSpec(
        num_scalar_prefetch=1,
        grid=(bsz, ns),
        in_specs=[in_hbm, in_hbm, per_b, per_b, per_b],
        out_specs=per_b,
        scratch_shapes=[pltpu.VMEM((rows, WIDTH), F32), pltpu.VMEM((rows, LANES), F32),
                        pltpu.VMEM((rows, LANES), F32), pltpu.VMEM((rows, LANES), F32),
                        pltpu.VMEM((nblk * rows, WIDTH), F32),
                        pltpu.VMEM((pps // 2 * rows, pps * page), BF16),
                        pltpu.VMEM((DEC_SLOTS, pps, WIDTH, page), F32),
                        pltpu.VMEM((DEC_SLOTS, pps, WIDTH, page), F32),
                        pltpu.SemaphoreType.DMA((2, DEC_SLOTS))],
    )
    return pl.pallas_call(
        body,
        grid_spec=grid_spec,
        out_shape=jax.ShapeDtypeStruct((bsz, DEC_ROWS, WIDTH), F32),
        compiler_params=pltpu.CompilerParams(dimension_semantics=("arbitrary", "arbitrary"),
                                             vmem_limit_bytes=VMEM_LIMIT),
        name="attn_sample",
    )(page_table, ck, cv, q, kn, vn)


def _proj_body(yb_ref, mixed_ref, x_ref, wout_ref, fng_ref, out_ref, *, final):
    o = _out_proj(x_ref[...], mixed_ref[...], yb_ref[...], wout_ref)
    if final:
        o = _rmsnorm(o, fng_ref[...])
    out_ref[...] = o


def _proj_call(y_b, mixed, x, w_out, layer, fng, *, final):
    bsz, t_len, d_model = x.shape
    n = bsz * t_len
    whole = lambda r, c: pl.BlockSpec((r, c), lambda i: (0, 0))
    out = pl.pallas_call(
        functools.partial(_proj_body, final=final),
        grid=(1,),
        in_specs=[whole(n, WIDTH), whole(n, 4 * WIDTH), whole(n, d_model),
                  pl.BlockSpec((None, 4 * WIDTH, d_model), lambda i: (layer, 0, 0)),
                  whole(1, d_model)],
        out_specs=whole(n, d_model),
        out_shape=jax.ShapeDtypeStruct((n, d_model), F32),
        compiler_params=pltpu.CompilerParams(dimension_semantics=("arbitrary",), vmem_limit_bytes=VMEM_LIMIT),
        name="proj_sample",
    )(y_b.reshape(n, WIDTH), mixed.reshape(n, 4 * WIDTH), x.reshape(n, d_model), w_out, fng)
    return out.reshape(bsz, t_len, d_model)


def _rope_tables(pos0, n):
    freqs = ROPE_THETA ** (-jnp.arange(ROT_HALF, dtype=F32) * 2.0 / (2 * ROT_HALF))
    ang = (pos0 + jnp.arange(n, dtype=jnp.int32)).astype(F32)[:, None] * freqs[None, :]
    cos, sin = jnp.cos(ang), jnp.sin(ang)
    one = jnp.ones((n, HEAD_DIM - 2 * ROT_HALF), F32)
    zero8 = jnp.zeros((n, ROT_HALF), F32)
    zero_rest = jnp.zeros((n, HEAD_DIM - 2 * ROT_HALF), F32)
    rc = jnp.concatenate([cos, cos, one], axis=1)
    rsa = jnp.concatenate([-sin, zero8, zero_rest], axis=1)
    rsb = jnp.concatenate([zero8, sin, zero_rest], axis=1)
    return tuple(jnp.tile(a, (1, N_HEADS)) for a in (rc, rsa, rsb))


def _pad_state(state):
    return jnp.pad(state, ((0, 0), (0, 0), (HALO - state.shape[2], 0), (0, 0)))


def kernel(x_prompt, x_sample, cache_k, cache_v, page_table, state_conv_a, state_conv_c, state_pool_d,
           norm_g, w_in, conv_a_w, conv_c_w, conv_c_b, ln_c_g, ln_c_b, w_pw_c, w_pool, pool_scale, w_out,
           final_norm_g):
    depth = w_in.shape[0]
    bsz, seq, d_model = x_prompt.shape
    dec_b, dec_t, _ = x_sample.shape
    past_len = page_table.shape[1] * cache_k.shape[2]
    assert seq % MOBA_BLOCK == 0 and dec_t <= DEC_ROWS and WIDTH == N_HEADS * HEAD_DIM

    pre_tile = next(t for t in (PRE_TILE, MOBA_BLOCK) if seq % t == 0)
    tabs_p = _rope_tables(0, seq)
    tabs_s = _rope_tables(past_len, DEC_ROWS)
    zero_state = jnp.zeros((1, bsz, HALO, WIDTH), F32)
    hist_a, hist_c, hist_d = _pad_state(state_conv_a), _pad_state(state_conv_c), _pad_state(state_pool_d)
    fng = final_norm_g.reshape(1, d_model)
    yp = x_prompt
    ys = jnp.pad(x_sample, ((0, 0), (0, DEC_ROWS - dec_t), (0, 0)))
    outs = [[] for _ in range(8)]
    kstack = jnp.zeros((depth, bsz, WIDTH, seq), F32)
    vstack = jnp.zeros((depth, bsz, WIDTH, seq), F32)
    n_groups = w_pool.shape[1]
    eye = jnp.eye(n_groups, dtype=F32)
    pool_bd = (w_pool[:, :, :, None, :] * eye[None, :, None, :, None]).reshape(depth, WIDTH, WIDTH)
    row = lambda a: a.reshape(depth, 1, a.shape[-1])
    lw = {"norm_g": row(norm_g), "w_in": w_in.astype(BF16), "conv_a_w": conv_a_w, "conv_c_w": conv_c_w,
          "conv_c_b": row(conv_c_b), "ln_c_g": row(ln_c_g), "ln_c_b": row(ln_c_b),
          "w_pw_c": w_pw_c.astype(BF16), "w_pool": pool_bd.astype(BF16), "pool_scale": row(pool_scale)}
    wo = w_out.astype(BF16)
    for l in range(depth):
        final = l == depth - 1

        mixed, kstack, vstack, qt, kb, vt, km, ap, cp, dp = _pre_call(
            yp, lw, l, tabs_p, zero_state, zero_state, zero_state, 0, (kstack, vstack),
            ns=1, tm=pre_tile, last_rows=pre_tile, pos0=0, prompt=True)
        yp = _attn_call(qt, kb, vt, km, mixed, yp, wo, l, fng, final=final)

        mixed_s, ks, vs, qs, as_, cs, ds = _pre_call(
            ys, lw, l, tabs_s, hist_a, hist_c, hist_d, l,
            ns=dec_b, tm=DEC_ROWS, last_rows=dec_t, pos0=past_len, prompt=False)
        yb_s = _dec_call(page_table, cache_k, cache_v, l, qs, ks, vs, t_valid=dec_t)
        ys = _proj_call(yb_s, mixed_s, ys, wo, l, fng, final=final)

        for lst, val in zip(outs, (ks[:, :dec_t].reshape(dec_b, dec_t, N_HEADS, HEAD_DIM),
                                   vs[:, :dec_t].reshape(dec_b, dec_t, N_HEADS, HEAD_DIM),
                                   ap, as_, cp, cs, dp, ds)):
            lst.append(val)
    to_rows = lambda st: st.reshape(depth, bsz, N_HEADS, HEAD_DIM, seq).transpose(0, 1, 4, 2, 3)
    return (yp, ys[:, :dec_t], to_rows(kstack), to_rows(vstack)) + tuple(jnp.stack(o) for o in outs)
```

```python
import functools

import jax
import jax.numpy as jnp
from jax import lax
from jax.experimental import pallas as pl
from jax.experimental.pallas import tpu as pltpu

F32 = jnp.float32
BF16 = jnp.bfloat16

WIDTH = 256
N_HEADS = 4
HEAD_DIM = 64
ROT_HALF = 8
ROPE_THETA = 500000.0
MOBA_BLOCK = 256
MOBA_TOPK = 3
CONV_A_WIDTH = 3
CONV_C_WIDTH = 31
POOL_WINDOWS = (2, 4, 8, 16)
POOL_STATE = 15
RMS_EPS = 1e-6
LN_EPS = 1e-5
HALO = 32
DEC_ROWS = 8
ATTN_GROUP = 4
DEC_SLOTS = 5
PRE_TILE = 1024
NEG_INF = float("-inf")
LOG2_E = 1.4426950408889634
LANES = 128
V7X_VMEM_BYTES = 64 * 1024 * 1024
VMEM_LIMIT = V7X_VMEM_BYTES - 8 * 1024 * 1024


def _silu(z):
    return z * jax.nn.sigmoid(z)


def _rmsnorm(x, g):
    ms = jnp.mean(x * x, axis=-1, keepdims=True)
    return x * lax.rsqrt(ms + RMS_EPS) * g


def _out_proj(x, mixed, y_b, wout_ref):
    yb = (y_b * mixed[:, WIDTH:2 * WIDTH]).astype(BF16)
    o = x + jnp.dot(mixed[:, 0:WIDTH].astype(BF16), wout_ref[0:WIDTH, :], preferred_element_type=F32)
    o = o + jnp.dot(yb, wout_ref[WIDTH:2 * WIDTH, :], preferred_element_type=F32)
    o = o + jnp.dot(mixed[:, 2 * WIDTH:4 * WIDTH].astype(BF16), wout_ref[2 * WIDTH:4 * WIDTH, :],
                    preferred_element_type=F32)
    return o


def _pre_body(x_ref, ng_ref, win_ref, caw_ref, ccw_ref, ccb_ref, lng_ref, lnb_ref, wpw_ref, wpool_ref,
              psc_ref, rc_ref, rsa_ref, rsb_ref, prea_ref, prec_ref, pred_ref, *refs,
              ns, tm, last_rows, pos0, prompt):
    if prompt:
        (_, _, mixed_ref, kst_ref, vst_ref, qt_ref, kb_ref, vt_ref, km_ref,
         na_ref, nc_ref, nd_ref, bufa, bufc, bufd, shc, x2, x4, x8) = refs
    else:
        (mixed_ref, k_ref, v_ref, q_ref, na_ref, nc_ref, nd_ref, bufa, bufc, bufd, shc, x2, x4, x8) = refs
    t = pl.program_id(1)
    nt = pl.num_programs(1)
    span = tm + HALO - 8
    n = ns * tm
    seq = lambda z: z.reshape(ns, tm, WIDTH)
    flat = lambda z: z.reshape(n, WIDTH)

    @pl.when(t == 0)
    def _():
        bufa[:, 0:HALO, :] = prea_ref[...]
        bufc[:, 0:HALO, :] = prec_ref[...]
        bufd[:, 0:HALO, :] = pred_ref[...]

    x = x_ref[...].reshape(n, x_ref.shape[-1])
    h = _rmsnorm(x, ng_ref[...]).astype(BF16)

    def proj(c0, cn):
        return jnp.dot(h, win_ref[:, c0 * WIDTH:(c0 + cn) * WIDTH], preferred_element_type=F32)

    def put(col, val):
        mixed_ref[:, :, col * WIDTH:(col + 1) * WIDTH] = seq(val)

    pa = proj(0, 4)
    bufa[:, HALO:HALO + tm, :] = seq(pa[:, 2 * WIDTH:3 * WIDTH] * pa[:, 0:WIDTH])
    conv = None
    for j in range(CONV_A_WIDTH):
        off = HALO - (CONV_A_WIDTH - 1) + j
        term = caw_ref[j:j + 1, :] * bufa[:, off:off + tm, :]
        conv = term if conv is None else conv + term
    put(0, pa[:, WIDTH:2 * WIDTH] * flat(conv) * _silu(pa[:, 3 * WIDTH:4 * WIDTH]))

    pb = proj(4, 4)
    tab = lambda r: flat(jnp.broadcast_to(r[...][None], (ns, tm, WIDTH)))
    rc, rsa, rsb = tab(rc_ref), tab(rsa_ref), tab(rsb_ref)

    def rope(z):
        return z * rc + pltpu.roll(z, WIDTH - ROT_HALF, 1) * rsa + pltpu.roll(z, ROT_HALF, 1) * rsb

    qs = rope(pb[:, 0:WIDTH]) * (HEAD_DIM ** -0.5)
    kr = rope(pb[:, WIDTH:2 * WIDTH])
    v = pb[:, 2 * WIDTH:3 * WIDTH]
    put(1, _silu(pb[:, 3 * WIDTH:4 * WIDTH]))
    if prompt:
        kst_ref[0] = kr.T
        v_t = v.T
        vst_ref[0] = v_t
        for r in range(tm // MOBA_BLOCK):
            blk = slice(r * MOBA_BLOCK, (r + 1) * MOBA_BLOCK)
            qt_ref[0, r] = qs[blk, :].T
            kb_ref[0, r] = jnp.concatenate([kr[blk, :].astype(BF16), jnp.zeros((MOBA_BLOCK, LANES), BF16)], axis=1)
            vt_ref[0, r] = jnp.concatenate([v_t[:, blk].astype(BF16), jnp.zeros((WIDTH, LANES), BF16)], axis=1)
            km_ref[0, r] = jnp.mean(kr[blk, :], axis=0, keepdims=True)
    else:
        k_ref[...] = seq(kr)
        v_ref[...] = seq(v)
        q_ref[...] = seq(qs)

    pc = proj(8, 3)
    bufc[:, HALO:HALO + tm, :] = seq(pc[:, 0:WIDTH] * jax.nn.sigmoid(pc[:, WIDTH:2 * WIDTH]))
    for r in range(1, 8):
        shc[r - 1] = bufc[:, r:r + span, :]
    acc = None
    for j in range(CONV_C_WIDTH):
        a, r = divmod(HALO - (CONV_C_WIDTH - 1) + j, 8)
        rows = bufc[:, 8 * a:8 * a + tm, :] if r == 0 else shc[r - 1, :, 8 * a:8 * a + tm, :]
        term = ccw_ref[j:j + 1, :] * rows
        acc = term if acc is None else acc + term
    acc = flat(acc) + ccb_ref[...]
    mu = jnp.mean(acc, axis=-1, keepdims=True)
    cen = acc - mu
    var = jnp.mean(cen * cen, axis=-1, keepdims=True)
    ln = cen * lax.rsqrt(var + LN_EPS) * lng_ref[...] + lnb_ref[...]
    yc = jnp.dot(_silu(ln).astype(BF16), wpw_ref[...], preferred_element_type=F32)
    put(2, yc * _silu(pc[:, 2 * WIDTH:3 * WIDTH]))

    pd = proj(11, 2)
    dx = pd[:, 0:WIDTH]
    bufd[:, HALO:HALO + tm, :] = seq(dx)
    assert POOL_WINDOWS == (2, 4, 8, 16) and HALO == 32
    x2[...] = bufd[:, 8:8 + span, :] + bufd[:, 7:7 + span, :]
    x4[:, 0:span - 8, :] = x2[:, 8:span, :] + x2[:, 6:span - 2, :]
    x8[:, 0:span - 16, :] = x4[:, 8:span - 8, :] + x4[:, 4:span - 12, :]
    sums = {2: x2[:, 24:24 + tm, :], 4: x4[:, 16:16 + tm, :], 8: x8[:, 8:8 + tm, :],
            16: x8[:, 8:8 + tm, :] + x8[:, 0:tm, :]}
    group = lax.broadcasted_iota(jnp.int32, (ns, tm, WIDTH), 2) // (WIDTH // len(POOL_WINDOWS))
    pos = pos0 + t * tm + lax.broadcasted_iota(jnp.int32, (ns, tm, WIDTH), 1)
    wsum = sums[POOL_WINDOWS[-1]]
    win = jnp.full((ns, tm, WIDTH), POOL_WINDOWS[-1], jnp.int32)
    for g in range(len(POOL_WINDOWS) - 2, -1, -1):
        wsum = jnp.where(group == g, sums[POOL_WINDOWS[g]], wsum)
        win = jnp.where(group == g, POOL_WINDOWS[g], win)
    cnt = jnp.minimum(pos + 1, win).astype(F32)
    diff = flat(wsum / cnt) - dx
    yd = jnp.dot(diff.astype(BF16), wpool_ref[...], preferred_element_type=F32) * psc_ref[...]
    put(3, yd * _silu(pd[:, WIDTH:2 * WIDTH]))

    @pl.when(t == nt - 1)
    def _():
        end = HALO + last_rows
        na_ref[...] = bufa[:, end - (CONV_A_WIDTH - 1):end, :]
        nc_ref[...] = bufc[:, end - (CONV_C_WIDTH - 1):end, :]
        nd_ref[...] = bufd[:, end - POOL_STATE:end, :]

    if tm >= HALO:
        @pl.when(t < nt - 1)
        def _():
            bufa[:, 0:HALO, :] = bufa[:, tm:tm + HALO, :]
            bufc[:, 0:HALO, :] = bufc[:, tm:tm + HALO, :]
            bufd[:, 0:HALO, :] = bufd[:, tm:tm + HALO, :]


def _pre_call(x, lw, layer, rope_tabs, pre_a, pre_c, pre_d, state_layer, kv_stacks=None, *,
              ns, tm, last_rows, pos0, prompt):
    bsz, t_len, d_model = x.shape
    nt = t_len // tm
    assert nt * tm == t_len and (nt == 1 or tm >= HALO) and bsz % ns == 0 and tm % 8 == 0
    rc, rsa, rsb = rope_tabs
    d_in = lw["w_in"].shape[-1]

    def vec(n):
        return pl.BlockSpec((None, 1, n), lambda b, t: (layer, 0, 0))

    def mat(r, c):
        return pl.BlockSpec((None, r, c), lambda b, t: (layer, 0, 0))

    def per_b(shape):
        return pl.BlockSpec((ns,) + shape, lambda b, t: (b,) + (0,) * len(shape))

    hist = pl.BlockSpec((None, ns, HALO, WIDTH), lambda b, t: (state_layer, b, 0, 0))
    tok = lambda w: pl.BlockSpec((ns, tm, w), lambda b, t: (b, t, 0))
    tab = pl.BlockSpec((tm, WIDTH), lambda b, t: (t, 0))
    in_specs = [tok(d_model), vec(d_model), mat(d_model, d_in),
                mat(CONV_A_WIDTH, WIDTH), mat(CONV_C_WIDTH, WIDTH), vec(WIDTH),
                vec(WIDTH), vec(WIDTH), mat(WIDTH, WIDTH), mat(WIDTH, WIDTH),
                vec(WIDTH), tab, tab, tab, hist, hist, hist]
    out_shape = [jax.ShapeDtypeStruct((bsz, t_len, 4 * WIDTH), F32),
                 jax.ShapeDtypeStruct((bsz, t_len, WIDTH), F32),
                 jax.ShapeDtypeStruct((bsz, t_len, WIDTH), F32)]
    out_specs = [tok(4 * WIDTH), tok(WIDTH), tok(WIDTH)]
    operands = [x, lw["norm_g"], lw["w_in"], lw["conv_a_w"], lw["conv_c_w"], lw["conv_c_b"], lw["ln_c_g"],
                lw["ln_c_b"], lw["w_pw_c"], lw["w_pool"], lw["pool_scale"], rc, rsa, rsb, pre_a, pre_c, pre_d]
    aliases = {}
    if prompt:
        stack_spec = pl.BlockSpec((None, 1, WIDTH, tm), lambda b, t: (layer, b, 0, t))
        for pos, stack in enumerate(kv_stacks):
            aliases[len(operands)] = 1 + pos
            operands.append(stack)
            in_specs.append(pl.BlockSpec(memory_space=pl.ANY))
            out_shape[1 + pos] = jax.ShapeDtypeStruct(stack.shape, F32)
            out_specs[1 + pos] = stack_spec
        assert tm % MOBA_BLOCK == 0 and ns == 1
        per_tile = tm // MOBA_BLOCK
        nblk = t_len // MOBA_BLOCK
        blk = lambda r, c: pl.BlockSpec((1, per_tile, r, c), lambda b, t: (b, t, 0, 0))
        out_shape += [jax.ShapeDtypeStruct((bsz, nblk, WIDTH, MOBA_BLOCK), F32),
                      jax.ShapeDtypeStruct((bsz, nblk, MOBA_BLOCK, WIDTH + LANES), BF16),
                      jax.ShapeDtypeStruct((bsz, nblk, WIDTH, MOBA_BLOCK + LANES), BF16),
                      jax.ShapeDtypeStruct((bsz, nblk, 1, WIDTH), F32)]
        out_specs += [blk(WIDTH, MOBA_BLOCK), blk(MOBA_BLOCK, WIDTH + LANES), blk(WIDTH, MOBA_BLOCK + LANES),
                      blk(1, WIDTH)]
    else:
        out_shape += [jax.ShapeDtypeStruct((bsz, t_len, WIDTH), F32)]
        out_specs += [tok(WIDTH)]
    out_shape += [jax.ShapeDtypeStruct((bsz, CONV_A_WIDTH - 1, WIDTH), F32),
                  jax.ShapeDtypeStruct((bsz, CONV_C_WIDTH - 1, WIDTH), F32),
                  jax.ShapeDtypeStruct((bsz, POOL_STATE, WIDTH), F32)]
    out_specs += [per_b((CONV_A_WIDTH - 1, WIDTH)), per_b((CONV_C_WIDTH - 1, WIDTH)),
                  per_b((POOL_STATE, WIDTH))]
    body = functools.partial(_pre_body, ns=ns, tm=tm, last_rows=last_rows, pos0=pos0, prompt=prompt)
    return pl.pallas_call(
        body,
        grid=(bsz // ns, nt),
        in_specs=in_specs,
        out_specs=out_specs,
        out_shape=out_shape,
        scratch_shapes=[pltpu.VMEM((ns, HALO + tm, WIDTH), F32)] * 3
        + [pltpu.VMEM((7, ns, HALO + tm - 8, WIDTH), F32)] + [pltpu.VMEM((ns, HALO + tm - 8, WIDTH), F32)] * 3,
        input_output_aliases=aliases,
        compiler_params=pltpu.CompilerParams(dimension_semantics=("arbitrary", "arbitrary"),
                                             vmem_limit_bytes=VMEM_LIMIT),
        name="pre_prompt" if prompt else "pre_sample",
    )(*operands)


def _top_blocks_bias(gate, index, n_index, axis):
    bias = jnp.full(gate.shape, NEG_INF, F32)
    for _ in range(MOBA_TOPK):
        top = jnp.max(gate, axis=axis, keepdims=True)
        cand = jnp.logical_and(gate == top, top > NEG_INF)
        first = jnp.min(jnp.where(cand, index, n_index), axis=axis, keepdims=True)
        pick = index == first
        bias = jnp.where(pick, 0.0, bias)
        gate = jnp.where(pick, NEG_INF, gate)
    return bias


def _attn_body(qt_ref, kb_ref, vt_ref, km_ref, mixed_ref, x_ref, wout_ref, fng_ref, out_ref,
               qm_s, bias_s, acc_s, s_a, s_b, p_a, p_b, al_a, al_b, *, nblk, tq, grp, final):
    s_bufs, p_bufs, al_bufs = (s_a, s_b), (p_a, p_b), (al_a, al_b)
    w4 = N_HEADS * tq
    i = pl.program_id(1)
    qt = qt_ref[0, 0]
    row_head = lax.broadcasted_iota(jnp.int32, (WIDTH, tq), 0) // HEAD_DIM
    km = km_ref[0]
    lane_head = lax.broadcasted_iota(jnp.int32, (nblk, WIDTH), 1) // HEAD_DIM
    blk = lax.broadcasted_iota(jnp.int32, (nblk, tq), 0)
    km_heads = jnp.concatenate([jnp.where(lane_head == h, km, 0.0) for h in range(N_HEADS)], axis=0)
    gates = jnp.dot(km_heads, qt, precision=lax.Precision.HIGHEST, preferred_element_type=F32)
    for h in range(N_HEADS):
        cols = slice(h * tq, (h + 1) * tq)
        qm_s[:, cols] = jnp.where(row_head == h, qt * LOG2_E, 0.0).astype(BF16)
        gate = jnp.where(blk < i, gates[h * nblk:(h + 1) * nblk, :], NEG_INF)
        bias_s[:, cols] = _top_blocks_bias(gate, blk, nblk, 0)

    n_groups = nblk // grp
    sub = [slice(c * MOBA_BLOCK, (c + 1) * MOBA_BLOCK) for c in range(grp)]

    def qk(group, slot):
        kb = kb_ref[0, pl.ds(group * grp, grp), :, 0:WIDTH].reshape(grp * MOBA_BLOCK, WIDTH)
        s_bufs[slot][:, 0:w4] = jnp.dot(kb, qm_s[:, 0:w4], preferred_element_type=F32)

    def pv(group, slot):
        for h in range(N_HEADS):
            rows = slice(h * HEAD_DIM, (h + 1) * HEAD_DIM)
            cols = slice(h * tq, (h + 1) * tq)
            vt = jnp.concatenate([vt_ref[0, group * grp + c, rows, 0:MOBA_BLOCK] for c in range(grp)], axis=1)
            o = jnp.dot(vt, p_bufs[slot][:, cols], preferred_element_type=F32)
            acc_s[rows, :] = al_bufs[slot][:, cols] * acc_s[rows, :] + o

    qk(0, 1)
    key = lax.broadcasted_iota(jnp.int32, (MOBA_BLOCK, w4), 0)
    qry = lax.broadcasted_iota(jnp.int32, (MOBA_BLOCK, w4), 1) % tq
    s = jnp.dot(kb_ref[0, i, :, 0:WIDTH], qm_s[:, 0:w4], preferred_element_type=F32)
    s = jnp.where(key <= qry, s, NEG_INF)
    m = jnp.max(s, axis=0, keepdims=True)
    p = jnp.exp2(s - m)
    l = jnp.sum(p, axis=0, keepdims=True)
    pb = p.astype(BF16)
    for h in range(N_HEADS):
        rows = slice(h * HEAD_DIM, (h + 1) * HEAD_DIM)
        acc_s[rows, :] = jnp.dot(vt_ref[0, i, rows, 0:MOBA_BLOCK], pb[:, h * tq:(h + 1) * tq],
                                 preferred_element_type=F32)
    p_bufs[0][:, 0:w4] = jnp.zeros((grp * MOBA_BLOCK, w4), BF16)
    al_bufs[0][...] = jnp.ones((1, w4), F32)

    def half(n, m, l, cur, nxt):
        picked = [bias_s[pl.ds((n - 1) * grp + c, 1), :] for c in range(grp)]
        m_new = m
        for c in range(grp):
            m_new = jnp.maximum(m_new, jnp.max(s_bufs[cur][sub[c], 0:w4], axis=0, keepdims=True) + picked[c])
        alpha = jnp.exp2(m - m_new)
        l = alpha * l
        for c in range(grp):
            p = jnp.exp2(s_bufs[cur][sub[c], 0:w4] - (m_new - picked[c]))
            l = l + jnp.sum(p, axis=0, keepdims=True)
            p_bufs[cur][sub[c], 0:w4] = p.astype(BF16)
        al_bufs[cur][...] = alpha
        pv(jnp.maximum(n - 2, 0), nxt)
        qk(jnp.minimum(n, n_groups - 1), nxt)
        return m_new, l

    def visit(n, carry):
        return lax.cond(n % 2 == 1, lambda: half(n, carry[0], carry[1], 1, 0),
                        lambda: half(n, carry[0], carry[1], 0, 1))

    n_visits = (i + grp - 1) // grp
    m, l = lax.fori_loop(1, n_visits + 1, visit, (m, l))
    odd = n_visits % 2 == 1

    @pl.when(odd)
    def _():
        pv(n_visits - 1, 1)

    @pl.when(jnp.logical_not(odd))
    def _():
        pv(jnp.maximum(n_visits - 1, 0), 0)

    inv = 1.0 / l
    for h in range(N_HEADS):
        rows = slice(h * HEAD_DIM, (h + 1) * HEAD_DIM)
        acc_s[rows, :] = acc_s[rows, :] * inv[:, h * tq:(h + 1) * tq]
    y_b = acc_s[...].T
    o = _out_proj(x_ref[0], mixed_ref[0], y_b, wout_ref)
    if final:
        o = _rmsnorm(o, fng_ref[...])
    out_ref[0] = o


def _attn_call(qt, kb, vt, km, mixed, x, w_out, layer, fng, *, final):
    bsz, nblk, _, tq = qt.shape
    d_model = x.shape[-1]
    assert tq == MOBA_BLOCK
    grp = next(g for g in (ATTN_GROUP, 2, 1) if nblk % g == 0)
    body = functools.partial(_attn_body, nblk=nblk, tq=tq, grp=grp, final=final)
    whole = lambda r, c: pl.BlockSpec((1, nblk, r, c), lambda b, i: (b, 0, 0, 0))
    return pl.pallas_call(
        body,
        grid=(bsz, nblk),
        in_specs=[pl.BlockSpec((1, 1, WIDTH, tq), lambda b, i: (b, i, 0, 0)),
                  whole(MOBA_BLOCK, WIDTH + LANES), whole(WIDTH, MOBA_BLOCK + LANES),
                  pl.BlockSpec((1, nblk, WIDTH), lambda b, i: (b, 0, 0)),
                  pl.BlockSpec((1, tq, 4 * WIDTH), lambda b, i: (b, i, 0)),
                  pl.BlockSpec((1, tq, d_model), lambda b, i: (b, i, 0)),
                  pl.BlockSpec((None, 4 * WIDTH, d_model), lambda b, i: (layer, 0, 0)),
                  pl.BlockSpec((1, d_model), lambda b, i: (0, 0))],
        out_specs=pl.BlockSpec((1, tq, d_model), lambda b, i: (b, i, 0)),
        out_shape=jax.ShapeDtypeStruct(x.shape, F32),
        scratch_shapes=[pltpu.VMEM((WIDTH, N_HEADS * tq + LANES), BF16),
                        pltpu.VMEM((nblk, N_HEADS * tq), F32),
                        pltpu.VMEM((WIDTH, tq), F32)]
        + [pltpu.VMEM((grp * MOBA_BLOCK, N_HEADS * tq + LANES), F32)] * 2
        + [pltpu.VMEM((grp * MOBA_BLOCK, N_HEADS * tq + LANES), BF16)] * 2
        + [pltpu.VMEM((1, N_HEADS * tq), F32)] * 2,
        compiler_params=pltpu.CompilerParams(dimension_semantics=("arbitrary", "arbitrary"),
                                             vmem_limit_bytes=VMEM_LIMIT),
        name="attn_prompt",
    )(qt, kb, vt, km.reshape(bsz, nblk, WIDTH), mixed, x, w_out, fng)


def _dec_body(pt_ref, *refs, layer, n_seq, pps, ns, t_valid):
    (ck_ref, cv_ref, q_ref, kn_ref, vn_ref, yb_ref,
     qm_s, mp_s, lp_s, gp_s, op_s, pbd_s, kbuf, vbuf, sems) = refs
    step = pl.program_id(1)
    rows = N_HEADS * DEC_ROWS
    nb = pps // 2
    lane_head = lax.broadcasted_iota(jnp.int32, (DEC_ROWS, WIDTH), 1) // HEAD_DIM
    lane = lax.broadcasted_iota(jnp.int32, (rows, LANES), 1)

    g = pl.program_id(0) * ns + step
    n_steps = pl.num_programs(0) * ns

    def page_copies(gg, page_of):
        slot = gg % DEC_SLOTS
        for c in range(pps):
            page = page_of(c)
            yield pltpu.make_async_copy(ck_ref.at[layer, page], kbuf.at[slot, c], sems.at[0, slot])
            yield pltpu.make_async_copy(cv_ref.at[layer, page], vbuf.at[slot, c], sems.at[1, slot])

    def request(gg):
        seq_id, seq_step = gg // ns, gg % ns
        for copy in page_copies(gg, lambda c: pt_ref[seq_id, seq_step * pps + c]):
            copy.start()

    @pl.when(g == 0)
    def _():
        pbd_s[...] = jnp.zeros(pbd_s.shape, BF16)
        for ahead in range(min(DEC_SLOTS - 1, n_seq * ns)):
            request(ahead)

    @pl.when(g + (DEC_SLOTS - 1) < n_steps)
    def _():
        request(g + (DEC_SLOTS - 1))

    for copy in page_copies(g, lambda c: 0):
        copy.wait()
    slot = g % DEC_SLOTS

    @pl.when(step == 0)
    def _():
        q8 = q_ref[0]
        for h in range(N_HEADS):
            qm_s[h * DEC_ROWS:(h + 1) * DEC_ROWS, :] = jnp.where(lane_head == h, q8, 0.0)
        mp_s[...] = jnp.full((rows, LANES), NEG_INF, F32)
        gp_s[...] = jnp.full((rows, LANES), NEG_INF, F32)
        lp_s[...] = jnp.zeros((rows, LANES), F32)

    qmb = qm_s[...].astype(BF16)
    contract_last = (((1,), (1,)), ((), ()))
    kt = jnp.concatenate([kbuf[slot, c] for c in range(pps)], axis=1).astype(BF16)
    vt = jnp.concatenate([vbuf[slot, c] for c in range(pps)], axis=1).astype(BF16)
    s_all = jnp.dot(qmb, kt, preferred_element_type=F32)
    for c in range(nb):
        s = s_all[:, c * MOBA_BLOCK:(c + 1) * MOBA_BLOCK]
        gate = jnp.sum(s, axis=1, keepdims=True) * (1.0 / MOBA_BLOCK)
        m = jnp.max(s, axis=1, keepdims=True)
        p = jnp.exp(s - m)
        l = jnp.sum(p, axis=1, keepdims=True)
        j = step * nb + c
        mp_s[...] = jnp.where(lane == j, m, mp_s[...])
        lp_s[...] = jnp.where(lane == j, l, lp_s[...])
        gp_s[...] = jnp.where(lane == j, gate, gp_s[...])
        pbd_s[c * rows:(c + 1) * rows, c * MOBA_BLOCK:(c + 1) * MOBA_BLOCK] = p.astype(BF16)
    op_s[pl.ds(pl.multiple_of(step * (nb * rows), nb * rows), nb * rows), :] = lax.dot_general(
        pbd_s[...], vt, contract_last, preferred_element_type=F32)

    @pl.when(step == ns - 1)
    def _():
        nblk = ns * nb
        sel = _top_blocks_bias(gp_s[...], lane, LANES, 1) == 0.0
        zpad = jnp.zeros((LANES - DEC_ROWS, WIDTH), F32)
        kn = jnp.concatenate([kn_ref[0], zpad], axis=0).astype(BF16)
        vn = jnp.concatenate([vn_ref[0], zpad], axis=0).astype(BF16)
        s = lax.dot_general(qmb, kn, contract_last, preferred_element_type=F32)
        qpos = jnp.minimum(lax.broadcasted_iota(jnp.int32, (rows, LANES), 0) % DEC_ROWS, t_valid - 1)
        s = jnp.where(lane <= qpos, s, NEG_INF)
        m_own = jnp.max(s, axis=1, keepdims=True)
        p = jnp.exp(s - m_own)
        l_own = jnp.sum(p, axis=1, keepdims=True)
        o_own = jnp.dot(p.astype(BF16), vn, preferred_element_type=F32)
        mp = mp_s[...]
        m_fin = jnp.maximum(m_own, jnp.max(jnp.where(sel, mp, NEG_INF), axis=1, keepdims=True))
        w = jnp.where(sel, jnp.exp(mp - m_fin), 0.0)
        a_own = jnp.exp(m_own - m_fin)
        l_fin = a_own * l_own + jnp.sum(w * lp_s[...], axis=1, keepdims=True)
        o_fin = a_own * o_own
        for j in range(nblk):
            o_fin = o_fin + w[:, j:j + 1] * op_s[j * rows:(j + 1) * rows, :]
        y = o_fin / l_fin
        y_b = None
        for h in range(N_HEADS):
            part = jnp.where(lane_head == h, y[h * DEC_ROWS:(h + 1) * DEC_ROWS, :], 0.0)
            y_b = part if y_b is None else y_b + part
        yb_ref[0] = y_b


def _dec_call(page_table, cache_k, cache_v, layer, q, kn, vn, *, t_valid):
    bsz, n_pages = page_table.shape
    depth, n_pool, page, _, _ = cache_k.shape
    assert (n_pages * page) % MOBA_BLOCK == 0 and MOBA_BLOCK == 2 * page
    nblk = n_pages * page // MOBA_BLOCK
    assert MOBA_TOPK <= nblk <= LANES
    pps = next(c for c in (16, 8, 4, 2) if n_pages % c == 0)
    ns = n_pages // pps
    ck = cache_k.transpose(0, 1, 3, 4, 2).reshape(depth, n_pool, WIDTH, page)
    cv = cache_v.transpose(0, 1, 3, 4, 2).reshape(depth, n_pool, WIDTH, page)

    in_hbm = pl.BlockSpec(memory_space=pl.ANY)
    per_b = pl.BlockSpec((1, DEC_ROWS, WIDTH), lambda b, s, pt: (b, 0, 0))
    rows = N_HEADS * DEC_ROWS
    body = functools.partial(_dec_body, layer=layer, n_seq=bsz, pps=pps, ns=ns, t_valid=t_valid)
    grid_spec = pltpu.PrefetchScalarGridSpec(
        num_scalar_prefetch=1,
        grid=(bsz, ns),
        in_specs=[in_hbm, in_hbm, per_b, per_b, per_b],
        out_specs=per_b,
        scratch_shapes=[pltpu.VMEM((rows, WIDTH), F32), pltpu.VMEM((rows, LANES), F32),
                        pltpu.VMEM((rows, LANES), F32), pltpu.VMEM((rows, LANES), F32),
                        pltpu.VMEM((nblk * rows, WIDTH), F32),
                        pltpu.VMEM((pps // 2 * rows, pps * page), BF16),
                        pltpu.VMEM((DEC_SLOTS, pps, WIDTH, page), F32),
                        pltpu.VMEM((DEC_SLOTS, pps, WIDTH, page), F32),
                        pltpu.SemaphoreType.DMA((2, DEC_SLOTS))],
    )
    return pl.pallas_call(
        body,
        grid_spec=grid_spec,
        out_shape=jax.ShapeDtypeStruct((bsz, DEC_ROWS, WIDTH), F32),
        compiler_params=pltpu.CompilerParams(dimension_semantics=("arbitrary", "arbitrary"),
                                             vmem_limit_bytes=VMEM_LIMIT),
        name="attn_sample",
    )(page_table, ck, cv, q, kn, vn)


def _proj_body(yb_ref, mixed_ref, x_ref, wout_ref, fng_ref, out_ref, *, final):
    o = _out_proj(x_ref[...], mixed_ref[...], yb_ref[...], wout_ref)
    if final:
        o = _rmsnorm(o, fng_ref[...])
    out_ref[...] = o


def _proj_call(y_b, mixed, x, w_out, layer, fng, *, final):
    bsz, t_len, d_model = x.shape
    n = bsz * t_len
    whole = lambda r, c: pl.BlockSpec((r, c), lambda i: (0, 0))
    out = pl.pallas_call(
        functools.partial(_proj_body, final=final),
        grid=(1,),
        in_specs=[whole(n, WIDTH), whole(n, 4 * WIDTH), whole(n, d_model),
                  pl.BlockSpec((None, 4 * WIDTH, d_model), lambda i: (layer, 0, 0)),
                  whole(1, d_model)],
        out_specs=whole(n, d_model),
        out_shape=jax.ShapeDtypeStruct((n, d_model), F32),
        compiler_params=pltpu.CompilerParams(dimension_semantics=("arbitrary",), vmem_limit_bytes=VMEM_LIMIT),
        name="proj_sample",
    )(y_b.reshape(n, WIDTH), mixed.reshape(n, 4 * WIDTH), x.reshape(n, d_model), w_out, fng)
    return out.reshape(bsz, t_len, d_model)


def _rope_tables(pos0, n):
    freqs = ROPE_THETA ** (-jnp.arange(ROT_HALF, dtype=F32) * 2.0 / (2 * ROT_HALF))
    ang = (pos0 + jnp.arange(n, dtype=jnp.int32)).astype(F32)[:, None] * freqs[None, :]
    cos, sin = jnp.cos(ang), jnp.sin(ang)
    one = jnp.ones((n, HEAD_DIM - 2 * ROT_HALF), F32)
    zero8 = jnp.zeros((n, ROT_HALF), F32)
    zero_rest = jnp.zeros((n, HEAD_DIM - 2 * ROT_HALF), F32)
    rc = jnp.concatenate([cos, cos, one], axis=1)
    rsa = jnp.concatenate([-sin, zero8, zero_rest], axis=1)
    rsb = jnp.concatenate([zero8, sin, zero_rest], axis=1)
    return tuple(jnp.tile(a, (1, N_HEADS)) for a in (rc, rsa, rsb))


def _pad_state(state):
    return jnp.pad(state, ((0, 0), (0, 0), (HALO - state.shape[2], 0), (0, 0)))


def kernel(x_prompt, x_sample, cache_k, cache_v, page_table, state_conv_a, state_conv_c, state_pool_d,
           norm_g, w_in, conv_a_w, conv_c_w, conv_c_b, ln_c_g, ln_c_b, w_pw_c, w_pool, pool_scale, w_out,
           final_norm_g):
    depth = w_in.shape[0]
    bsz, seq, d_model = x_prompt.shape
    dec_b, dec_t, _ = x_sample.shape
    past_len = page_table.shape[1] * cache_k.shape[2]
    assert seq % MOBA_BLOCK == 0 and dec_t <= DEC_ROWS and WIDTH == N_HEADS * HEAD_DIM

    pre_tile = next(t for t in (PRE_TILE, MOBA_BLOCK) if seq % t == 0)
    tabs_p = _rope_tables(0, seq)
    tabs_s = _rope_tables(past_len, DEC_ROWS)
    zero_state = jnp.zeros((1, bsz, HALO, WIDTH), F32)
    hist_a, hist_c, hist_d = _pad_state(state_conv_a), _pad_state(state_conv_c), _pad_state(state_pool_d)
    fng = final_norm_g.reshape(1, d_model)
    yp = x_prompt
    ys = jnp.pad(x_sample, ((0, 0), (0, DEC_ROWS - dec_t), (0, 0)))
    outs = [[] for _ in range(8)]
    kstack = jnp.zeros((depth, bsz, WIDTH, seq), F32)
    vstack = jnp.zeros((depth, bsz, WIDTH, seq), F32)
    n_groups = w_pool.shape[1]
    eye = jnp.eye(n_groups, dtype=F32)
    pool_bd = (w_pool[:, :, :, None, :] * eye[None, :, None, :, None]).reshape(depth, WIDTH, WIDTH)
    row = lambda a: a.reshape(depth, 1, a.shape[-1])
    lw = {"norm_g": row(norm_g), "w_in": w_in.astype(BF16), "conv_a_w": conv_a_w, "conv_c_w": conv_c_w,
          "conv_c_b": row(conv_c_b), "ln_c_g": row(ln_c_g), "ln_c_b": row(ln_c_b),
          "w_pw_c": w_pw_c.astype(BF16), "w_pool": pool_bd.astype(BF16), "pool_scale": row(pool_scale)}
    wo = w_out.astype(BF16)
    for l in range(depth):
        final = l == depth - 1

        mixed, kstack, vstack, qt, kb, vt, km, ap, cp, dp = _pre_call(
            yp, lw, l, tabs_p, zero_state, zero_state, zero_state, 0, (kstack, vstack),
            ns=1, tm=pre_tile, last_rows=pre_tile, pos0=0, prompt=True)
        yp = _attn_call(qt, kb, vt, km, mixed, yp, wo, l, fng, final=final)

        mixed_s, ks, vs, qs, as_, cs, ds = _pre_call(
            ys, lw, l, tabs_s, hist_a, hist_c, hist_d, l,
            ns=dec_b, tm=DEC_ROWS, last_rows=dec_t, pos0=past_len, prompt=False)
        yb_s = _dec_call(page_table, cache_k, cache_v, l, qs, ks, vs, t_valid=dec_t)
        ys = _proj_call(yb_s, mixed_s, ys, wo, l, fng, final=final)

        for lst, val in zip(outs, (ks[:, :dec_t].reshape(dec_b, dec_t, N_HEADS, HEAD_DIM),
                                   vs[:, :dec_t].reshape(dec_b, dec_t, N_HEADS, HEAD_DIM),
                                   ap, as_, cp, cs, dp, ds)):
            lst.append(val)
    to_rows = lambda st: st.reshape(depth, bsz, N_HEADS, HEAD_DIM, seq).transpose(0, 1, 4, 2, 3)
    return (yp, ys[:, :dec_t], to_rows(kstack), to_rows(vstack)) + tuple(jnp.stack(o) for o in outs)
```

```python
import functools

import jax
import jax.numpy as jnp
from jax import lax
from jax.experimental import pallas as pl
from jax.experimental.pallas import tpu as pltpu

F32 = jnp.float32
BF16 = jnp.bfloat16

WIDTH = 256
N_HEADS = 4
HEAD_DIM = 64
ROT_HALF = 8
ROPE_THETA = 500000.0
MOBA_BLOCK = 256
MOBA_TOPK = 3
CONV_A_WIDTH = 3
CONV_C_WIDTH = 31
POOL_WINDOWS = (2, 4, 8, 16)
POOL_STATE = 15
RMS_EPS = 1e-6
LN_EPS = 1e-5
HALO = 32
DEC_ROWS = 8
SOFTMAX_PARTS = 4
ATTN_GROUP = 4
DEC_SLOTS = 5
PRE_TILE = 1024
NEG_INF = float("-inf")
LOG2_E = 1.4426950408889634
LANES = 128
V7X_VMEM_BYTES = 64 * 1024 * 1024
VMEM_LIMIT = V7X_VMEM_BYTES - 8 * 1024 * 1024


def _silu(z):
    return z * jax.nn.sigmoid(z)


def _rmsnorm(x, g):
    ms = jnp.mean(x * x, axis=-1, keepdims=True)
    return x * lax.rsqrt(ms + RMS_EPS) * g


def _out_proj(x, mixed, y_b, wout_ref):
    yb = (y_b * mixed[:, WIDTH:2 * WIDTH]).astype(BF16)
    o = x + jnp.dot(mixed[:, 0:WIDTH].astype(BF16), wout_ref[0:WIDTH, :], preferred_element_type=F32)
    o = o + jnp.dot(yb, wout_ref[WIDTH:2 * WIDTH, :], preferred_element_type=F32)
    o = o + jnp.dot(mixed[:, 2 * WIDTH:4 * WIDTH].astype(BF16), wout_ref[2 * WIDTH:4 * WIDTH, :],
                    preferred_element_type=F32)
    return o


def _pre_body(x_ref, ng_ref, win_ref, caw_ref, ccw_ref, ccb_ref, lng_ref, lnb_ref, wpw_ref, wpool_ref,
              psc_ref, rc_ref, rsa_ref, rsb_ref, prea_ref, prec_ref, pred_ref, *refs,
              ns, tm, last_rows, pos0, prompt):
    if prompt:
        (_, _, mixed_ref, kst_ref, vst_ref, qt_ref, kb_ref, vt_ref, km_ref,
         na_ref, nc_ref, nd_ref, bufa, bufc, bufd, shc, x2, x4, x8) = refs
    else:
        (mixed_ref, k_ref, v_ref, q_ref, na_ref, nc_ref, nd_ref, bufa, bufc, bufd, shc, x2, x4, x8) = refs
    t = pl.program_id(1)
    nt = pl.num_programs(1)
    span = tm + HALO - 8
    n = ns * tm
    seq = lambda z: z.reshape(ns, tm, WIDTH)
    flat = lambda z: z.reshape(n, WIDTH)

    @pl.when(t == 0)
    def _():
        bufa[:, 0:HALO, :] = prea_ref[...]
        bufc[:, 0:HALO, :] = prec_ref[...]
        bufd[:, 0:HALO, :] = pred_ref[...]

    x = x_ref[...].reshape(n, x_ref.shape[-1])
    h = _rmsnorm(x, ng_ref[...]).astype(BF16)

    def proj(c0, cn):
        return jnp.dot(h, win_ref[:, c0 * WIDTH:(c0 + cn) * WIDTH], preferred_element_type=F32)

    def put(col, val):
        mixed_ref[:, :, col * WIDTH:(col + 1) * WIDTH] = seq(val)

    pa = proj(0, 4)
    bufa[:, HALO:HALO + tm, :] = seq(pa[:, 2 * WIDTH:3 * WIDTH] * pa[:, 0:WIDTH])
    conv = None
    for j in range(CONV_A_WIDTH):
        off = HALO - (CONV_A_WIDTH - 1) + j
        term = caw_ref[j:j + 1, :] * bufa[:, off:off + tm, :]
        conv = term if conv is None else conv + term
    put(0, pa[:, WIDTH:2 * WIDTH] * flat(conv) * _silu(pa[:, 3 * WIDTH:4 * WIDTH]))

    pb = proj(4, 4)
    tab = lambda r: flat(jnp.broadcast_to(r[...][None], (ns, tm, WIDTH)))
    rc, rsa, rsb = tab(rc_ref), tab(rsa_ref), tab(rsb_ref)

    def rope(z):
        return z * rc + pltpu.roll(z, WIDTH - ROT_HALF, 1) * rsa + pltpu.roll(z, ROT_HALF, 1) * rsb

    qs = rope(pb[:, 0:WIDTH]) * (HEAD_DIM ** -0.5)
    kr = rope(pb[:, WIDTH:2 * WIDTH])
    v = pb[:, 2 * WIDTH:3 * WIDTH]
    put(1, _silu(pb[:, 3 * WIDTH:4 * WIDTH]))
    if prompt:
        kst_ref[0] = kr.T
        v_t = v.T
        vst_ref[0] = v_t
        for r in range(tm // MOBA_BLOCK):
            blk = slice(r * MOBA_BLOCK, (r + 1) * MOBA_BLOCK)
            qt_ref[0, r] = qs[blk, :].T
            kb_ref[0, r] = jnp.concatenate([kr[blk, :].astype(BF16), jnp.zeros((MOBA_BLOCK, LANES), BF16)], axis=1)
            vt_ref[0, r] = v_t[:, blk].astype(BF16)
            km_ref[0, r] = jnp.mean(kr[blk, :], axis=0, keepdims=True)
    else:
        k_ref[...] = seq(kr)
        v_ref[...] = seq(v)
        q_ref[...] = seq(qs)

    pc = proj(8, 3)
    bufc[:, HALO:HALO + tm, :] = seq(pc[:, 0:WIDTH] * jax.nn.sigmoid(pc[:, WIDTH:2 * WIDTH]))
    for r in range(1, 8):
        shc[r - 1] = bufc[:, r:r + span, :]
    acc = None
    for j in range(CONV_C_WIDTH):
        a, r = divmod(HALO - (CONV_C_WIDTH - 1) + j, 8)
        rows = bufc[:, 8 * a:8 * a + tm, :] if r == 0 else shc[r - 1, :, 8 * a:8 * a + tm, :]
        term = ccw_ref[j:j + 1, :] * rows
        acc = term if acc is None else acc + term
    acc = flat(acc) + ccb_ref[...]
    mu = jnp.mean(acc, axis=-1, keepdims=True)
    cen = acc - mu
    var = jnp.mean(cen * cen, axis=-1, keepdims=True)
    ln = cen * lax.rsqrt(var + LN_EPS) * lng_ref[...] + lnb_ref[...]
    yc = jnp.dot(_silu(ln).astype(BF16), wpw_ref[...], preferred_element_type=F32)
    put(2, yc * _silu(pc[:, 2 * WIDTH:3 * WIDTH]))

    pd = proj(11, 2)
    dx = pd[:, 0:WIDTH]
    bufd[:, HALO:HALO + tm, :] = seq(dx)
    assert POOL_WINDOWS == (2, 4, 8, 16) and HALO == 32
    x2[...] = bufd[:, 8:8 + span, :] + bufd[:, 7:7 + span, :]
    x4[:, 0:span - 8, :] = x2[:, 8:span, :] + x2[:, 6:span - 2, :]
    x8[:, 0:span - 16, :] = x4[:, 8:span - 8, :] + x4[:, 4:span - 12, :]
    sums = {2: x2[:, 24:24 + tm, :], 4: x4[:, 16:16 + tm, :], 8: x8[:, 8:8 + tm, :],
            16: x8[:, 8:8 + tm, :] + x8[:, 0:tm, :]}
    group = lax.broadcasted_iota(jnp.int32, (ns, tm, WIDTH), 2) // (WIDTH // len(POOL_WINDOWS))
    pos = pos0 + t * tm + lax.broadcasted_iota(jnp.int32, (ns, tm, WIDTH), 1)
    wsum = sums[POOL_WINDOWS[-1]]
    win = jnp.full((ns, tm, WIDTH), POOL_WINDOWS[-1], jnp.int32)
    for g in range(len(POOL_WINDOWS) - 2, -1, -1):
        wsum = jnp.where(group == g, sums[POOL_WINDOWS[g]], wsum)
        win = jnp.where(group == g, POOL_WINDOWS[g], win)
    cnt = jnp.minimum(pos + 1, win).astype(F32)
    diff = flat(wsum / cnt) - dx
    yd = jnp.dot(diff.astype(BF16), wpool_ref[...], preferred_element_type=F32) * psc_ref[...]
    put(3, yd * _silu(pd[:, WIDTH:2 * WIDTH]))

    @pl.when(t == nt - 1)
    def _():
        end = HALO + last_rows
        na_ref[...] = bufa[:, end - (CONV_A_WIDTH - 1):end, :]
        nc_ref[...] = bufc[:, end - (CONV_C_WIDTH - 1):end, :]
        nd_ref[...] = bufd[:, end - POOL_STATE:end, :]

    if tm >= HALO:
        @pl.when(t < nt - 1)
        def _():
            bufa[:, 0:HALO, :] = bufa[:, tm:tm + HALO, :]
            bufc[:, 0:HALO, :] = bufc[:, tm:tm + HALO, :]
            bufd[:, 0:HALO, :] = bufd[:, tm:tm + HALO, :]


def _pre_call(x, lw, layer, rope_tabs, pre_a, pre_c, pre_d, state_layer, kv_stacks=None, *,
              ns, tm, last_rows, pos0, prompt):
    bsz, t_len, d_model = x.shape
    nt = t_len // tm
    assert nt * tm == t_len and (nt == 1 or tm >= HALO) and bsz % ns == 0 and tm % 8 == 0
    rc, rsa, rsb = rope_tabs
    d_in = lw["w_in"].shape[-1]

    def vec(n):
        return pl.BlockSpec((None, 1, n), lambda b, t: (layer, 0, 0))

    def mat(r, c):
        return pl.BlockSpec((None, r, c), lambda b, t: (layer, 0, 0))

    def per_b(shape):
        return pl.BlockSpec((ns,) + shape, lambda b, t: (b,) + (0,) * len(shape))

    hist = pl.BlockSpec((None, ns, HALO, WIDTH), lambda b, t: (state_layer, b, 0, 0))
    tok = lambda w: pl.BlockSpec((ns, tm, w), lambda b, t: (b, t, 0))
    tab = pl.BlockSpec((tm, WIDTH), lambda b, t: (t, 0))
    in_specs = [tok(d_model), vec(d_model), mat(d_model, d_in),
                mat(CONV_A_WIDTH, WIDTH), mat(CONV_C_WIDTH, WIDTH), vec(WIDTH),
                vec(WIDTH), vec(WIDTH), mat(WIDTH, WIDTH), mat(WIDTH, WIDTH),
                vec(WIDTH), tab, tab, tab, hist, hist, hist]
    out_shape = [jax.ShapeDtypeStruct((bsz, t_len, 4 * WIDTH), F32),
                 jax.ShapeDtypeStruct((bsz, t_len, WIDTH), F32),
                 jax.ShapeDtypeStruct((bsz, t_len, WIDTH), F32)]
    out_specs = [tok(4 * WIDTH), tok(WIDTH), tok(WIDTH)]
    operands = [x, lw["norm_g"], lw["w_in"], lw["conv_a_w"], lw["conv_c_w"], lw["conv_c_b"], lw["ln_c_g"],
                lw["ln_c_b"], lw["w_pw_c"], lw["w_pool"], lw["pool_scale"], rc, rsa, rsb, pre_a, pre_c, pre_d]
    aliases = {}
    if prompt:
        stack_spec = pl.BlockSpec((None, 1, WIDTH, tm), lambda b, t: (layer, b, 0, t))
        for pos, stack in enumerate(kv_stacks):
            aliases[len(operands)] = 1 + pos
            operands.append(stack)
            in_specs.append(pl.BlockSpec(memory_space=pl.ANY))
            out_shape[1 + pos] = jax.ShapeDtypeStruct(stack.shape, F32)
            out_specs[1 + pos] = stack_spec
        assert tm % MOBA_BLOCK == 0 and ns == 1
        per_tile = tm // MOBA_BLOCK
        nblk = t_len // MOBA_BLOCK
        blk = lambda r, c: pl.BlockSpec((1, per_tile, r, c), lambda b, t: (b, t, 0, 0))
        out_shape += [jax.ShapeDtypeStruct((bsz, nblk, WIDTH, MOBA_BLOCK), F32),
                      jax.ShapeDtypeStruct((bsz, nblk, MOBA_BLOCK, WIDTH + LANES), BF16),
                      jax.ShapeDtypeStruct((bsz, nblk, WIDTH, MOBA_BLOCK), BF16),
                      jax.ShapeDtypeStruct((bsz, nblk, 1, WIDTH), F32)]
        out_specs += [blk(WIDTH, MOBA_BLOCK), blk(MOBA_BLOCK, WIDTH + LANES), blk(WIDTH, MOBA_BLOCK), blk(1, WIDTH)]
    else:
        out_shape += [jax.ShapeDtypeStruct((bsz, t_len, WIDTH), F32)]
        out_specs += [tok(WIDTH)]
    out_shape += [jax.ShapeDtypeStruct((bsz, CONV_A_WIDTH - 1, WIDTH), F32),
                  jax.ShapeDtypeStruct((bsz, CONV_C_WIDTH - 1, WIDTH), F32),
                  jax.ShapeDtypeStruct((bsz, POOL_STATE, WIDTH), F32)]
    out_specs += [per_b((CONV_A_WIDTH - 1, WIDTH)), per_b((CONV_C_WIDTH - 1, WIDTH)),
                  per_b((POOL_STATE, WIDTH))]
    body = functools.partial(_pre_body, ns=ns, tm=tm, last_rows=last_rows, pos0=pos0, prompt=prompt)
    return pl.pallas_call(
        body,
        grid=(bsz // ns, nt),
        in_specs=in_specs,
        out_specs=out_specs,
        out_shape=out_shape,
        scratch_shapes=[pltpu.VMEM((ns, HALO + tm, WIDTH), F32)] * 3
        + [pltpu.VMEM((7, ns, HALO + tm - 8, WIDTH), F32)] + [pltpu.VMEM((ns, HALO + tm - 8, WIDTH), F32)] * 3,
        input_output_aliases=aliases,
        compiler_params=pltpu.CompilerParams(dimension_semantics=("arbitrary", "arbitrary"),
                                             vmem_limit_bytes=VMEM_LIMIT),
        name="pre_prompt" if prompt else "pre_sample",
    )(*operands)


def _top_blocks_bias(gate, index, n_index, axis):
    bias = jnp.full(gate.shape, NEG_INF, F32)
    for _ in range(MOBA_TOPK):
        top = jnp.max(gate, axis=axis, keepdims=True)
        cand = jnp.logical_and(gate == top, top > NEG_INF)
        first = jnp.min(jnp.where(cand, index, n_index), axis=axis, keepdims=True)
        pick = index == first
        bias = jnp.where(pick, 0.0, bias)
        gate = jnp.where(pick, NEG_INF, gate)
    return bias


def _attn_body(qt_ref, kb_ref, vt_ref, km_ref, mixed_ref, x_ref, wout_ref, fng_ref, out_ref,
               qm_s, bias_s, acc_s, s_a, s_b, p_a, p_b, al_a, al_b, *, nblk, tq, grp, final):
    s_bufs, p_bufs, al_bufs = (s_a, s_b), (p_a, p_b), (al_a, al_b)
    w4 = N_HEADS * tq
    i = pl.program_id(1)
    qt = qt_ref[0, 0]
    row_head = lax.broadcasted_iota(jnp.int32, (WIDTH, tq), 0) // HEAD_DIM
    km = km_ref[0]
    lane_head = lax.broadcasted_iota(jnp.int32, (nblk, WIDTH), 1) // HEAD_DIM
    blk = lax.broadcasted_iota(jnp.int32, (nblk, tq), 0)
    km_heads = jnp.concatenate([jnp.where(lane_head == h, km, 0.0) for h in range(N_HEADS)], axis=0)
    gates = jnp.dot(km_heads, qt, precision=lax.Precision.HIGHEST, preferred_element_type=F32)
    for h in range(N_HEADS):
        cols = slice(h * tq, (h + 1) * tq)
        qm_s[:, cols] = jnp.where(row_head == h, qt * LOG2_E, 0.0).astype(BF16)
        gate = jnp.where(blk < i, gates[h * nblk:(h + 1) * nblk, :], NEG_INF)
        bias_s[:, cols] = _top_blocks_bias(gate, blk, nblk, 0)

    n_groups = nblk // grp
    sub = [slice(c * MOBA_BLOCK, (c + 1) * MOBA_BLOCK) for c in range(grp)]

    def qk(group, slot):
        kb = kb_ref[0, pl.ds(group * grp, grp), :, 0:WIDTH].reshape(grp * MOBA_BLOCK, WIDTH)
        s_bufs[slot][:, 0:w4] = jnp.dot(kb, qm_s[:, 0:w4], preferred_element_type=F32)

    def pv(group, slot):
        for h in range(N_HEADS):
            rows = slice(h * HEAD_DIM, (h + 1) * HEAD_DIM)
            cols = slice(h * tq, (h + 1) * tq)
            vt = jnp.concatenate([vt_ref[0, group * grp + c, rows, :] for c in range(grp)], axis=1)
            o = jnp.dot(vt, p_bufs[slot][:, cols], preferred_element_type=F32)
            acc_s[rows, :] = al_bufs[slot][:, cols] * acc_s[rows, :] + o

    qk(0, 1)
    key = lax.broadcasted_iota(jnp.int32, (MOBA_BLOCK, w4), 0)
    qry = lax.broadcasted_iota(jnp.int32, (MOBA_BLOCK, w4), 1) % tq
    s = jnp.dot(kb_ref[0, i, :, 0:WIDTH], qm_s[:, 0:w4], preferred_element_type=F32)
    s = jnp.where(key <= qry, s, NEG_INF)
    m = jnp.max(s, axis=0, keepdims=True)
    p = jnp.exp2(s - m)
    l = jnp.sum(p, axis=0, keepdims=True)
    pb = p.astype(BF16)
    for h in range(N_HEADS):
        rows = slice(h * HEAD_DIM, (h + 1) * HEAD_DIM)
        acc_s[rows, :] = jnp.dot(vt_ref[0, i, rows, :], pb[:, h * tq:(h + 1) * tq], preferred_element_type=F32)
    p_bufs[0][:, 0:w4] = jnp.zeros((grp * MOBA_BLOCK, w4), BF16)
    al_bufs[0][...] = jnp.ones((1, w4), F32)

    def half(n, m, l, cur, nxt):
        m_parts, l_parts = [], []
        for part in range(SOFTMAX_PARTS):
            cs = slice(part * w4 // SOFTMAX_PARTS, (part + 1) * w4 // SOFTMAX_PARTS)
            picked = [bias_s[pl.ds((n - 1) * grp + c, 1), cs] for c in range(grp)]
            m_old = m[:, cs]
            m_new = m_old
            for c in range(grp):
                m_new = jnp.maximum(m_new, jnp.max(s_bufs[cur][sub[c], cs], axis=0, keepdims=True) + picked[c])
            alpha = jnp.exp2(m_old - m_new)
            l_new = alpha * l[:, cs]
            for c in range(grp):
                p = jnp.exp2(s_bufs[cur][sub[c], cs] - (m_new - picked[c]))
                l_new = l_new + jnp.sum(p, axis=0, keepdims=True)
                p_bufs[cur][sub[c], cs] = p.astype(BF16)
            al_bufs[cur][:, cs] = alpha
            m_parts.append(m_new)
            l_parts.append(l_new)
        pv(jnp.maximum(n - 2, 0), nxt)
        qk(jnp.minimum(n, n_groups - 1), nxt)
        return jnp.concatenate(m_parts, axis=1), jnp.concatenate(l_parts, axis=1)

    def visit(n, carry):
        return lax.cond(n % 2 == 1, lambda: half(n, carry[0], carry[1], 1, 0),
                        lambda: half(n, carry[0], carry[1], 0, 1))

    n_visits = (i + grp - 1) // grp
    m, l = lax.fori_loop(1, n_visits + 1, visit, (m, l))
    odd = n_visits % 2 == 1

    @pl.when(odd)
    def _():
        pv(n_visits - 1, 1)

    @pl.when(jnp.logical_not(odd))
    def _():
        pv(jnp.maximum(n_visits - 1, 0), 0)

    inv = 1.0 / l
    for h in range(N_HEADS):
        rows = slice(h * HEAD_DIM, (h + 1) * HEAD_DIM)
        acc_s[rows, :] = acc_s[rows, :] * inv[:, h * tq:(h + 1) * tq]
    y_b = acc_s[...].T
    o = _out_proj(x_ref[0], mixed_ref[0], y_b, wout_ref)
    if final:
        o = _rmsnorm(o, fng_ref[...])
    out_ref[0] = o


def _attn_call(qt, kb, vt, km, mixed, x, w_out, layer, fng, *, final):
    bsz, nblk, _, tq = qt.shape
    d_model = x.shape[-1]
    assert tq == MOBA_BLOCK
    grp = next(g for g in (ATTN_GROUP, 2, 1) if nblk % g == 0)
    body = functools.partial(_attn_body, nblk=nblk, tq=tq, grp=grp, final=final)
    whole = lambda r, c: pl.BlockSpec((1, nblk, r, c), lambda b, i: (b, 0, 0, 0))
    return pl.pallas_call(
        body,
        grid=(bsz, nblk),
        in_specs=[pl.BlockSpec((1, 1, WIDTH, tq), lambda b, i: (b, i, 0, 0)),
                  whole(MOBA_BLOCK, WIDTH + LANES), whole(WIDTH, MOBA_BLOCK),
                  pl.BlockSpec((1, nblk, WIDTH), lambda b, i: (b, 0, 0)),
                  pl.BlockSpec((1, tq, 4 * WIDTH), lambda b, i: (b, i, 0)),
                  pl.BlockSpec((1, tq, d_model), lambda b, i: (b, i, 0)),
                  pl.BlockSpec((None, 4 * WIDTH, d_model), lambda b, i: (layer, 0, 0)),
                  pl.BlockSpec((1, d_model), lambda b, i: (0, 0))],
        out_specs=pl.BlockSpec((1, tq, d_model), lambda b, i: (b, i, 0)),
        out_shape=jax.ShapeDtypeStruct(x.shape, F32),
        scratch_shapes=[pltpu.VMEM((WIDTH, N_HEADS * tq + LANES), BF16),
                        pltpu.VMEM((nblk, N_HEADS * tq), F32),
                        pltpu.VMEM((WIDTH, tq), F32)]
        + [pltpu.VMEM((grp * MOBA_BLOCK, N_HEADS * tq + LANES), F32)] * 2
        + [pltpu.VMEM((grp * MOBA_BLOCK, N_HEADS * tq + LANES), BF16)] * 2
        + [pltpu.VMEM((1, N_HEADS * tq), F32)] * 2,
        compiler_params=pltpu.CompilerParams(dimension_semantics=("arbitrary", "arbitrary"),
                                             vmem_limit_bytes=VMEM_LIMIT),
        name="attn_prompt",
    )(qt, kb, vt, km.reshape(bsz, nblk, WIDTH), mixed, x, w_out, fng)


def _dec_body(pt_ref, *refs, layer, n_seq, pps, ns, t_valid):
    (ck_ref, cv_ref, q_ref, kn_ref, vn_ref, yb_ref,
     qm_s, mp_s, lp_s, gp_s, op_s, pbd_s, kbuf, vbuf, sems) = refs
    step = pl.program_id(1)
    rows = N_HEADS * DEC_ROWS
    nb = pps // 2
    lane_head = lax.broadcasted_iota(jnp.int32, (DEC_ROWS, WIDTH), 1) // HEAD_DIM
    lane = lax.broadcasted_iota(jnp.int32, (rows, LANES), 1)

    g = pl.program_id(0) * ns + step
    n_steps = pl.num_programs(0) * ns

    def page_copies(gg, page_of):
        slot = gg % DEC_SLOTS
        for c in range(pps):
            page = page_of(c)
            yield pltpu.make_async_copy(ck_ref.at[layer, page], kbuf.at[slot, c], sems.at[0, slot])
            yield pltpu.make_async_copy(cv_ref.at[layer, page], vbuf.at[slot, c], sems.at[1, slot])

    def request(gg):
        seq_id, seq_step = gg // ns, gg % ns
        for copy in page_copies(gg, lambda c: pt_ref[seq_id, seq_step * pps + c]):
            copy.start()

    @pl.when(g == 0)
    def _():
        pbd_s[...] = jnp.zeros(pbd_s.shape, BF16)
        for ahead in range(min(DEC_SLOTS - 1, n_seq * ns)):
            request(ahead)

    @pl.when(g + (DEC_SLOTS - 1) < n_steps)
    def _():
        request(g + (DEC_SLOTS - 1))

    for copy in page_copies(g, lambda c: 0):
        copy.wait()
    slot = g % DEC_SLOTS

    @pl.when(step == 0)
    def _():
        q8 = q_ref[0]
        for h in range(N_HEADS):
            qm_s[h * DEC_ROWS:(h + 1) * DEC_ROWS, :] = jnp.where(lane_head == h, q8, 0.0)
        mp_s[...] = jnp.full((rows, LANES), NEG_INF, F32)
        gp_s[...] = jnp.full((rows, LANES), NEG_INF, F32)
        lp_s[...] = jnp.zeros((rows, LANES), F32)

    qmb = qm_s[...].astype(BF16)
    contract_last = (((1,), (1,)), ((), ()))
    kt = jnp.concatenate([kbuf[slot, c] for c in range(pps)], axis=1).astype(BF16)
    vt = jnp.concatenate([vbuf[slot, c] for c in range(pps)], axis=1).astype(BF16)
    s_all = jnp.dot(qmb, kt, preferred_element_type=F32)
    for c in range(nb):
        s = s_all[:, c * MOBA_BLOCK:(c + 1) * MOBA_BLOCK]
        gate = jnp.sum(s, axis=1, keepdims=True) * (1.0 / MOBA_BLOCK)
        m = jnp.max(s, axis=1, keepdims=True)
        p = jnp.exp(s - m)
        l = jnp.sum(p, axis=1, keepdims=True)
        j = step * nb + c
        mp_s[...] = jnp.where(lane == j, m, mp_s[...])
        lp_s[...] = jnp.where(lane == j, l, lp_s[...])
        gp_s[...] = jnp.where(lane == j, gate, gp_s[...])
        pbd_s[c * rows:(c + 1) * rows, c * MOBA_BLOCK:(c + 1) * MOBA_BLOCK] = p.astype(BF16)
    op_s[pl.ds(pl.multiple_of(step * (nb * rows), nb * rows), nb * rows), :] = lax.dot_general(
        pbd_s[...], vt, contract_last, preferred_element_type=F32)

    @pl.when(step == ns - 1)
    def _():
        nblk = ns * nb
        sel = _top_blocks_bias(gp_s[...], lane, LANES, 1) == 0.0
        zpad = jnp.zeros((LANES - DEC_ROWS, WIDTH), F32)
        kn = jnp.concatenate([kn_ref[0], zpad], axis=0).astype(BF16)
        vn = jnp.concatenate([vn_ref[0], zpad], axis=0).astype(BF16)
        s = lax.dot_general(qmb, kn, contract_last, preferred_element_type=F32)
        qpos = jnp.minimum(lax.broadcasted_iota(jnp.int32, (rows, LANES), 0) % DEC_ROWS, t_valid - 1)
        s = jnp.where(lane <= qpos, s, NEG_INF)
        m_own = jnp.max(s, axis=1, keepdims=True)
        p = jnp.exp(s - m_own)
        l_own = jnp.sum(p, axis=1, keepdims=True)
        o_own = jnp.dot(p.astype(BF16), vn, preferred_element_type=F32)
        mp = mp_s[...]
        m_fin = jnp.maximum(m_own, jnp.max(jnp.where(sel, mp, NEG_INF), axis=1, keepdims=True))
        w = jnp.where(sel, jnp.exp(mp - m_fin), 0.0)
        a_own = jnp.exp(m_own - m_fin)
        l_fin = a_own * l_own + jnp.sum(w * lp_s[...], axis=1, keepdims=True)
        o_fin = a_own * o_own
        for j in range(nblk):
            o_fin = o_fin + w[:, j:j + 1] * op_s[j * rows:(j + 1) * rows, :]
        y = o_fin / l_fin
        y_b = None
        for h in range(N_HEADS):
            part = jnp.where(lane_head == h, y[h * DEC_ROWS:(h + 1) * DEC_ROWS, :], 0.0)
            y_b = part if y_b is None else y_b + part
        yb_ref[0] = y_b


def _dec_call(page_table, cache_k, cache_v, layer, q, kn, vn, *, t_valid):
    bsz, n_pages = page_table.shape
    depth, n_pool, page, _, _ = cache_k.shape
    assert (n_pages * page) % MOBA_BLOCK == 0 and MOBA_BLOCK == 2 * page
    nblk = n_pages * page // MOBA_BLOCK
    assert MOBA_TOPK <= nblk <= LANES
    pps = next(c for c in (16, 8, 4, 2) if n_pages % c == 0)
    ns = n_pages // pps
    ck = cache_k.transpose(0, 1, 3, 4, 2).reshape(depth, n_pool, WIDTH, page)
    cv = cache_v.transpose(0, 1, 3, 4, 2).reshape(depth, n_pool, WIDTH, page)

    in_hbm = pl.BlockSpec(memory_space=pl.ANY)
    per_b = pl.BlockSpec((1, DEC_ROWS, WIDTH), lambda b, s, pt: (b, 0, 0))
    rows = N_HEADS * DEC_ROWS
    body = functools.partial(_dec_body, layer=layer, n_seq=bsz, pps=pps, ns=ns, t_valid=t_valid)
    grid_spec = pltpu.PrefetchScalarGridSpec(
        num_scalar_prefetch=1,
        grid=(bsz, ns),
        in_specs=[in_hbm, in_hbm, per_b, per_b, per_b],
        out_specs=per_b,
        scratch_shapes=[pltpu.VMEM((rows, WIDTH), F32), pltpu.VMEM((rows, LANES), F32),
                        pltpu.VMEM((rows, LANES), F32), pltpu.VMEM((rows, LANES), F32),
                        pltpu.VMEM((nblk * rows, WIDTH), F32),
                        pltpu.VMEM((pps // 2 * rows, pps * page), BF16),
                        pltpu.VMEM((DEC_SLOTS, pps, WIDTH, page), F32),
                        pltpu.VMEM((DEC_SLOTS, pps, WIDTH, page), F32),
                        pltpu.SemaphoreType.DMA((2, DEC_SLOTS))],
    )
    return pl.pallas_call(
        body,
        grid_spec=grid_spec,
        out_shape=jax.ShapeDtypeStruct((bsz, DEC_ROWS, WIDTH), F32),
        compiler_params=pltpu.CompilerParams(dimension_semantics=("arbitrary", "arbitrary"),
                                             vmem_limit_bytes=VMEM_LIMIT),
        name="attn_sample",
    )(page_table, ck, cv, q, kn, vn)


def _proj_body(yb_ref, mixed_ref, x_ref, wout_ref, fng_ref, out_ref, *, final):
    o = _out_proj(x_ref[...], mixed_ref[...], yb_ref[...], wout_ref)
    if final:
        o = _rmsnorm(o, fng_ref[...])
    out_ref[...] = o


def _proj_call(y_b, mixed, x, w_out, layer, fng, *, final):
    bsz, t_len, d_model = x.shape
    n = bsz * t_len
    whole = lambda r, c: pl.BlockSpec((r, c), lambda i: (0, 0))
    out = pl.pallas_call(
        functools.partial(_proj_body, final=final),
        grid=(1,),
        in_specs=[whole(n, WIDTH), whole(n, 4 * WIDTH), whole(n, d_model),
                  pl.BlockSpec((None, 4 * WIDTH, d_model), lambda i: (layer, 0, 0)),
                  whole(1, d_model)],
        out_specs=whole(n, d_model),
        out_shape=jax.ShapeDtypeStruct((n, d_model), F32),
        compiler_params=pltpu.CompilerParams(dimension_semantics=("arbitrary",), vmem_limit_bytes=VMEM_LIMIT),
        name="proj_sample",
    )(y_b.reshape(n, WIDTH), mixed.reshape(n, 4 * WIDTH), x.reshape(n, d_model), w_out, fng)
    return out.reshape(bsz, t_len, d_model)


def _rope_tables(pos0, n):
    freqs = ROPE_THETA ** (-jnp.arange(ROT_HALF, dtype=F32) * 2.0 / (2 * ROT_HALF))
    ang = (pos0 + jnp.arange(n, dtype=jnp.int32)).astype(F32)[:, None] * freqs[None, :]
    cos, sin = jnp.cos(ang), jnp.sin(ang)
    one = jnp.ones((n, HEAD_DIM - 2 * ROT_HALF), F32)
    zero8 = jnp.zeros((n, ROT_HALF), F32)
    zero_rest = jnp.zeros((n, HEAD_DIM - 2 * ROT_HALF), F32)
    rc = jnp.concatenate([cos, cos, one], axis=1)
    rsa = jnp.concatenate([-sin, zero8, zero_rest], axis=1)
    rsb = jnp.concatenate([zero8, sin, zero_rest], axis=1)
    return tuple(jnp.tile(a, (1, N_HEADS)) for a in (rc, rsa, rsb))


def _pad_state(state):
    return jnp.pad(state, ((0, 0), (0, 0), (HALO - state.shape[2], 0), (0, 0)))


def kernel(x_prompt, x_sample, cache_k, cache_v, page_table, state_conv_a, state_conv_c, state_pool_d,
           norm_g, w_in, conv_a_w, conv_c_w, conv_c_b, ln_c_g, ln_c_b, w_pw_c, w_pool, pool_scale, w_out,
           final_norm_g):
    depth = w_in.shape[0]
    bsz, seq, d_model = x_prompt.shape
    dec_b, dec_t, _ = x_sample.shape
    past_len = page_table.shape[1] * cache_k.shape[2]
    assert seq % MOBA_BLOCK == 0 and dec_t <= DEC_ROWS and WIDTH == N_HEADS * HEAD_DIM

    pre_tile = next(t for t in (PRE_TILE, MOBA_BLOCK) if seq % t == 0)
    tabs_p = _rope_tables(0, seq)
    tabs_s = _rope_tables(past_len, DEC_ROWS)
    zero_state = jnp.zeros((1, bsz, HALO, WIDTH), F32)
    hist_a, hist_c, hist_d = _pad_state(state_conv_a), _pad_state(state_conv_c), _pad_state(state_pool_d)
    fng = final_norm_g.reshape(1, d_model)
    yp = x_prompt
    ys = jnp.pad(x_sample, ((0, 0), (0, DEC_ROWS - dec_t), (0, 0)))
    outs = [[] for _ in range(8)]
    kstack = jnp.zeros((depth, bsz, WIDTH, seq), F32)
    vstack = jnp.zeros((depth, bsz, WIDTH, seq), F32)
    n_groups = w_pool.shape[1]
    eye = jnp.eye(n_groups, dtype=F32)
    pool_bd = (w_pool[:, :, :, None, :] * eye[None, :, None, :, None]).reshape(depth, WIDTH, WIDTH)
    row = lambda a: a.reshape(depth, 1, a.shape[-1])
    lw = {"norm_g": row(norm_g), "w_in": w_in.astype(BF16), "conv_a_w": conv_a_w, "conv_c_w": conv_c_w,
          "conv_c_b": row(conv_c_b), "ln_c_g": row(ln_c_g), "ln_c_b": row(ln_c_b),
          "w_pw_c": w_pw_c.astype(BF16), "w_pool": pool_bd.astype(BF16), "pool_scale": row(pool_scale)}
    wo = w_out.astype(BF16)
    for l in range(depth):
        final = l == depth - 1

        mixed, kstack, vstack, qt, kb, vt, km, ap, cp, dp = _pre_call(
            yp, lw, l, tabs_p, zero_state, zero_state, zero_state, 0, (kstack, vstack),
            ns=1, tm=pre_tile, last_rows=pre_tile, pos0=0, prompt=True)
        yp = _attn_call(qt, kb, vt, km, mixed, yp, wo, l, fng, final=final)

        mixed_s, ks, vs, qs, as_, cs, ds = _pre_call(
            ys, lw, l, tabs_s, hist_a, hist_c, hist_d, l,
            ns=dec_b, tm=DEC_ROWS, last_rows=dec_t, pos0=past_len, prompt=False)
        yb_s = _dec_call(page_table, cache_k, cache_v, l, qs, ks, vs, t_valid=dec_t)
        ys = _proj_call(yb_s, mixed_s, ys, wo, l, fng, final=final)

        for lst, val in zip(outs, (ks[:, :dec_t].reshape(dec_b, dec_t, N_HEADS, HEAD_DIM),
                                   vs[:, :dec_t].reshape(dec_b, dec_t, N_HEADS, HEAD_DIM),
                                   ap, as_, cp, cs, dp, ds)):
            lst.append(val)
    to_rows = lambda st: st.reshape(depth, bsz, N_HEADS, HEAD_DIM, seq).transpose(0, 1, 4, 2, 3)
    return (yp, ys[:, :dec_t], to_rows(kstack), to_rows(vstack)) + tuple(jnp.stack(o) for o in outs)
```

```python
import functools

import jax
import jax.numpy as jnp
from jax import lax
from jax.experimental import pallas as pl
from jax.experimental.pallas import tpu as pltpu

F32 = jnp.float32
BF16 = jnp.bfloat16

WIDTH = 256
N_HEADS = 4
HEAD_DIM = 64
ROT_HALF = 8
ROPE_THETA = 500000.0
MOBA_BLOCK = 256
MOBA_TOPK = 3
CONV_A_WIDTH = 3
CONV_C_WIDTH = 31
POOL_WINDOWS = (2, 4, 8, 16)
POOL_STATE = 15
RMS_EPS = 1e-6
LN_EPS = 1e-5
HALO = 32
DEC_ROWS = 8
SOFTMAX_PARTS = 4
ATTN_GROUP = 4
DEC_SLOTS = 5
PRE_TILE = 1024
NEG_INF = float("-inf")
LOG2_E = 1.4426950408889634
LANES = 128
V7X_VMEM_BYTES = 64 * 1024 * 1024
VMEM_LIMIT = V7X_VMEM_BYTES - 8 * 1024 * 1024


def _silu(z):
    return z * jax.nn.sigmoid(z)


def _rmsnorm(x, g):
    ms = jnp.mean(x * x, axis=-1, keepdims=True)
    return x * lax.rsqrt(ms + RMS_EPS) * g


def _out_proj(x, mixed, y_b, wout_ref):
    yb = (y_b * mixed[:, WIDTH:2 * WIDTH]).astype(BF16)
    o = x + jnp.dot(mixed[:, 0:WIDTH].astype(BF16), wout_ref[0:WIDTH, :], preferred_element_type=F32)
    o = o + jnp.dot(yb, wout_ref[WIDTH:2 * WIDTH, :], preferred_element_type=F32)
    o = o + jnp.dot(mixed[:, 2 * WIDTH:4 * WIDTH].astype(BF16), wout_ref[2 * WIDTH:4 * WIDTH, :],
                    preferred_element_type=F32)
    return o


def _pre_body(x_ref, ng_ref, win_ref, caw_ref, ccw_ref, ccb_ref, lng_ref, lnb_ref, wpw_ref, wpool_ref,
              psc_ref, rc_ref, rsa_ref, rsb_ref, prea_ref, prec_ref, pred_ref, *refs,
              ns, tm, last_rows, pos0, prompt):
    if prompt:
        (_, _, mixed_ref, kst_ref, vst_ref, qt_ref, kb_ref, vt_ref, km_ref,
         na_ref, nc_ref, nd_ref, bufa, bufc, bufd, shc, x2, x4, x8) = refs
    else:
        (mixed_ref, k_ref, v_ref, q_ref, na_ref, nc_ref, nd_ref, bufa, bufc, bufd, shc, x2, x4, x8) = refs
    t = pl.program_id(1)
    nt = pl.num_programs(1)
    span = tm + HALO - 8
    n = ns * tm
    seq = lambda z: z.reshape(ns, tm, WIDTH)
    flat = lambda z: z.reshape(n, WIDTH)

    @pl.when(t == 0)
    def _():
        bufa[:, 0:HALO, :] = prea_ref[...]
        bufc[:, 0:HALO, :] = prec_ref[...]
        bufd[:, 0:HALO, :] = pred_ref[...]

    x = x_ref[...].reshape(n, x_ref.shape[-1])
    h = _rmsnorm(x, ng_ref[...]).astype(BF16)

    def proj(c0, cn):
        return jnp.dot(h, win_ref[:, c0 * WIDTH:(c0 + cn) * WIDTH], preferred_element_type=F32)

    def put(col, val):
        mixed_ref[:, :, col * WIDTH:(col + 1) * WIDTH] = seq(val)

    pa = proj(0, 4)
    bufa[:, HALO:HALO + tm, :] = seq(pa[:, 2 * WIDTH:3 * WIDTH] * pa[:, 0:WIDTH])
    conv = None
    for j in range(CONV_A_WIDTH):
        off = HALO - (CONV_A_WIDTH - 1) + j
        term = caw_ref[j:j + 1, :] * bufa[:, off:off + tm, :]
        conv = term if conv is None else conv + term
    put(0, pa[:, WIDTH:2 * WIDTH] * flat(conv) * _silu(pa[:, 3 * WIDTH:4 * WIDTH]))

    pb = proj(4, 4)
    tab = lambda r: flat(jnp.broadcast_to(r[...][None], (ns, tm, WIDTH)))
    rc, rsa, rsb = tab(rc_ref), tab(rsa_ref), tab(rsb_ref)

    def rope(z):
        return z * rc + pltpu.roll(z, WIDTH - ROT_HALF, 1) * rsa + pltpu.roll(z, ROT_HALF, 1) * rsb

    qs = rope(pb[:, 0:WIDTH]) * (HEAD_DIM ** -0.5)
    kr = rope(pb[:, WIDTH:2 * WIDTH])
    v = pb[:, 2 * WIDTH:3 * WIDTH]
    put(1, _silu(pb[:, 3 * WIDTH:4 * WIDTH]))
    if prompt:
        kst_ref[0] = kr.T
        v_t = v.T
        vst_ref[0] = v_t
        for r in range(tm // MOBA_BLOCK):
            blk = slice(r * MOBA_BLOCK, (r + 1) * MOBA_BLOCK)
            qt_ref[0, r] = qs[blk, :].T
            kb_ref[0, r] = jnp.concatenate([kr[blk, :].astype(BF16), jnp.zeros((MOBA_BLOCK, LANES), BF16)], axis=1)
            vt_ref[0, r] = v_t[:, blk].astype(BF16)
            km_ref[0, r] = jnp.mean(kr[blk, :], axis=0, keepdims=True)
    else:
        k_ref[...] = seq(kr)
        v_ref[...] = seq(v)
        q_ref[...] = seq(qs)

    pc = proj(8, 3)
    bufc[:, HALO:HALO + tm, :] = seq(pc[:, 0:WIDTH] * jax.nn.sigmoid(pc[:, WIDTH:2 * WIDTH]))
    for r in range(1, 8):
        shc[r - 1] = bufc[:, r:r + span, :]
    acc = None
    for j in range(CONV_C_WIDTH):
        a, r = divmod(HALO - (CONV_C_WIDTH - 1) + j, 8)
        rows = bufc[:, 8 * a:8 * a + tm, :] if r == 0 else shc[r - 1, :, 8 * a:8 * a + tm, :]
        term = ccw_ref[j:j + 1, :] * rows
        acc = term if acc is None else acc + term
    acc = flat(acc) + ccb_ref[...]
    mu = jnp.mean(acc, axis=-1, keepdims=True)
    cen = acc - mu
    var = jnp.mean(cen * cen, axis=-1, keepdims=True)
    ln = cen * lax.rsqrt(var + LN_EPS) * lng_ref[...] + lnb_ref[...]
    yc = jnp.dot(_silu(ln).astype(BF16), wpw_ref[...], preferred_element_type=F32)
    put(2, yc * _silu(pc[:, 2 * WIDTH:3 * WIDTH]))

    pd = proj(11, 2)
    dx = pd[:, 0:WIDTH]
    bufd[:, HALO:HALO + tm, :] = seq(dx)
    assert POOL_WINDOWS == (2, 4, 8, 16) and HALO == 32
    x2[...] = bufd[:, 8:8 + span, :] + bufd[:, 7:7 + span, :]
    x4[:, 0:span - 8, :] = x2[:, 8:span, :] + x2[:, 6:span - 2, :]
    x8[:, 0:span - 16, :] = x4[:, 8:span - 8, :] + x4[:, 4:span - 12, :]
    sums = {2: x2[:, 24:24 + tm, :], 4: x4[:, 16:16 + tm, :], 8: x8[:, 8:8 + tm, :],
            16: x8[:, 8:8 + tm, :] + x8[:, 0:tm, :]}
    group = lax.broadcasted_iota(jnp.int32, (ns, tm, WIDTH), 2) // (WIDTH // len(POOL_WINDOWS))
    pos = pos0 + t * tm + lax.broadcasted_iota(jnp.int32, (ns, tm, WIDTH), 1)
    wsum = sums[POOL_WINDOWS[-1]]
    win = jnp.full((ns, tm, WIDTH), POOL_WINDOWS[-1], jnp.int32)
    for g in range(len(POOL_WINDOWS) - 2, -1, -1):
        wsum = jnp.where(group == g, sums[POOL_WINDOWS[g]], wsum)
        win = jnp.where(group == g, POOL_WINDOWS[g], win)
    cnt = jnp.minimum(pos + 1, win).astype(F32)
    diff = flat(wsum / cnt) - dx
    yd = jnp.dot(diff.astype(BF16), wpool_ref[...], preferred_element_type=F32) * psc_ref[...]
    put(3, yd * _silu(pd[:, WIDTH:2 * WIDTH]))

    @pl.when(t == nt - 1)
    def _():
        end = HALO + last_rows
        na_ref[...] = bufa[:, end - (CONV_A_WIDTH - 1):end, :]
        nc_ref[...] = bufc[:, end - (CONV_C_WIDTH - 1):end, :]
        nd_ref[...] = bufd[:, end - POOL_STATE:end, :]

    if tm >= HALO:
        @pl.when(t < nt - 1)
        def _():
            bufa[:, 0:HALO, :] = bufa[:, tm:tm + HALO, :]
            bufc[:, 0:HALO, :] = bufc[:, tm:tm + HALO, :]
            bufd[:, 0:HALO, :] = bufd[:, tm:tm + HALO, :]


def _pre_call(x, lw, layer, rope_tabs, pre_a, pre_c, pre_d, state_layer, kv_stacks=None, *,
              ns, tm, last_rows, pos0, prompt):
    bsz, t_len, d_model = x.shape
    nt = t_len // tm
    assert nt * tm == t_len and (nt == 1 or tm >= HALO) and bsz % ns == 0 and tm % 8 == 0
    rc, rsa, rsb = rope_tabs
    d_in = lw["w_in"].shape[-1]

    def vec(n):
        return pl.BlockSpec((None, 1, n), lambda b, t: (layer, 0, 0))

    def mat(r, c):
        return pl.BlockSpec((None, r, c), lambda b, t: (layer, 0, 0))

    def per_b(shape):
        return pl.BlockSpec((ns,) + shape, lambda b, t: (b,) + (0,) * len(shape))

    hist = pl.BlockSpec((None, ns, HALO, WIDTH), lambda b, t: (state_layer, b, 0, 0))
    tok = lambda w: pl.BlockSpec((ns, tm, w), lambda b, t: (b, t, 0))
    tab = pl.BlockSpec((tm, WIDTH), lambda b, t: (t, 0))
    in_specs = [tok(d_model), vec(d_model), mat(d_model, d_in),
                mat(CONV_A_WIDTH, WIDTH), mat(CONV_C_WIDTH, WIDTH), vec(WIDTH),
                vec(WIDTH), vec(WIDTH), mat(WIDTH, WIDTH), mat(WIDTH, WIDTH),
                vec(WIDTH), tab, tab, tab, hist, hist, hist]
    out_shape = [jax.ShapeDtypeStruct((bsz, t_len, 4 * WIDTH), F32),
                 jax.ShapeDtypeStruct((bsz, t_len, WIDTH), F32),
                 jax.ShapeDtypeStruct((bsz, t_len, WIDTH), F32)]
    out_specs = [tok(4 * WIDTH), tok(WIDTH), tok(WIDTH)]
    operands = [x, lw["norm_g"], lw["w_in"], lw["conv_a_w"], lw["conv_c_w"], lw["conv_c_b"], lw["ln_c_g"],
                lw["ln_c_b"], lw["w_pw_c"], lw["w_pool"], lw["pool_scale"], rc, rsa, rsb, pre_a, pre_c, pre_d]
    aliases = {}
    if prompt:
        stack_spec = pl.BlockSpec((None, 1, WIDTH, tm), lambda b, t: (layer, b, 0, t))
        for pos, stack in enumerate(kv_stacks):
            aliases[len(operands)] = 1 + pos
            operands.append(stack)
            in_specs.append(pl.BlockSpec(memory_space=pl.ANY))
            out_shape[1 + pos] = jax.ShapeDtypeStruct(stack.shape, F32)
            out_specs[1 + pos] = stack_spec
        assert tm % MOBA_BLOCK == 0 and ns == 1
        per_tile = tm // MOBA_BLOCK
        nblk = t_len // MOBA_BLOCK
        blk = lambda r, c: pl.BlockSpec((1, per_tile, r, c), lambda b, t: (b, t, 0, 0))
        out_shape += [jax.ShapeDtypeStruct((bsz, nblk, WIDTH, MOBA_BLOCK), F32),
                      jax.ShapeDtypeStruct((bsz, nblk, MOBA_BLOCK, WIDTH + LANES), BF16),
                      jax.ShapeDtypeStruct((bsz, nblk, WIDTH, MOBA_BLOCK), BF16),
                      jax.ShapeDtypeStruct((bsz, nblk, 1, WIDTH), F32)]
        out_specs += [blk(WIDTH, MOBA_BLOCK), blk(MOBA_BLOCK, WIDTH + LANES), blk(WIDTH, MOBA_BLOCK), blk(1, WIDTH)]
    else:
        out_shape += [jax.ShapeDtypeStruct((bsz, t_len, WIDTH), F32)]
        out_specs += [tok(WIDTH)]
    out_shape += [jax.ShapeDtypeStruct((bsz, CONV_A_WIDTH - 1, WIDTH), F32),
                  jax.ShapeDtypeStruct((bsz, CONV_C_WIDTH - 1, WIDTH), F32),
                  jax.ShapeDtypeStruct((bsz, POOL_STATE, WIDTH), F32)]
    out_specs += [per_b((CONV_A_WIDTH - 1, WIDTH)), per_b((CONV_C_WIDTH - 1, WIDTH)),
                  per_b((POOL_STATE, WIDTH))]
    body = functools.partial(_pre_body, ns=ns, tm=tm, last_rows=last_rows, pos0=pos0, prompt=prompt)
    return pl.pallas_call(
        body,
        grid=(bsz // ns, nt),
        in_specs=in_specs,
        out_specs=out_specs,
        out_shape=out_shape,
        scratch_shapes=[pltpu.VMEM((ns, HALO + tm, WIDTH), F32)] * 3
        + [pltpu.VMEM((7, ns, HALO + tm - 8, WIDTH), F32)] + [pltpu.VMEM((ns, HALO + tm - 8, WIDTH), F32)] * 3,
        input_output_aliases=aliases,
        compiler_params=pltpu.CompilerParams(dimension_semantics=("arbitrary", "arbitrary"),
                                             vmem_limit_bytes=VMEM_LIMIT),
        name="pre_prompt" if prompt else "pre_sample",
    )(*operands)


def _top_blocks_bias(gate, index, n_index, axis):
    bias = jnp.full(gate.shape, NEG_INF, F32)
    for _ in range(MOBA_TOPK):
        top = jnp.max(gate, axis=axis, keepdims=True)
        cand = jnp.logical_and(gate == top, top > NEG_INF)
        first = jnp.min(jnp.where(cand, index, n_index), axis=axis, keepdims=True)
        pick = index == first
        bias = jnp.where(pick, 0.0, bias)
        gate = jnp.where(pick, NEG_INF, gate)
    return bias


def _attn_body(qt_ref, kb_ref, vt_ref, km_ref, mixed_ref, x_ref, wout_ref, fng_ref, out_ref,
               qm_s, bias_s, acc_s, s_a, s_b, p_a, p_b, al_a, al_b, *, nblk, tq, grp, final):
    s_bufs, p_bufs, al_bufs = (s_a, s_b), (p_a, p_b), (al_a, al_b)
    w4 = N_HEADS * tq
    i = pl.program_id(1)
    qt = qt_ref[0, 0]
    row_head = lax.broadcasted_iota(jnp.int32, (WIDTH, tq), 0) // HEAD_DIM
    km = km_ref[0]
    lane_head = lax.broadcasted_iota(jnp.int32, (nblk, WIDTH), 1) // HEAD_DIM
    blk = lax.broadcasted_iota(jnp.int32, (nblk, tq), 0)
    km_heads = jnp.concatenate([jnp.where(lane_head == h, km, 0.0) for h in range(N_HEADS)], axis=0)
    gates = jnp.dot(km_heads, qt, precision=lax.Precision.HIGHEST, preferred_element_type=F32)
    for h in range(N_HEADS):
        cols = slice(h * tq, (h + 1) * tq)
        qm_s[:, cols] = jnp.where(row_head == h, qt * LOG2_E, 0.0).astype(BF16)
        gate = jnp.where(blk < i, gates[h * nblk:(h + 1) * nblk, :], NEG_INF)
        bias_s[:, cols] = _top_blocks_bias(gate, blk, nblk, 0)

    n_groups = nblk // grp
    sub = [slice(c * MOBA_BLOCK, (c + 1) * MOBA_BLOCK) for c in range(grp)]

    def qk(group, slot):
        kb = kb_ref[0, pl.ds(group * grp, grp), :, 0:WIDTH].reshape(grp * MOBA_BLOCK, WIDTH)
        s_bufs[slot][:, 0:w4] = jnp.dot(kb, qm_s[:, 0:w4], preferred_element_type=F32)

    def pv(group, slot):
        for h in range(N_HEADS):
            rows = slice(h * HEAD_DIM, (h + 1) * HEAD_DIM)
            cols = slice(h * tq, (h + 1) * tq)
            vt = jnp.concatenate([vt_ref[0, group * grp + c, rows, :] for c in range(grp)], axis=1)
            o = jnp.dot(vt, p_bufs[slot][:, cols], preferred_element_type=F32)
            acc_s[rows, :] = al_bufs[slot][:, cols] * acc_s[rows, :] + o

    qk(0, 1)
    key = lax.broadcasted_iota(jnp.int32, (MOBA_BLOCK, w4), 0)
    qry = lax.broadcasted_iota(jnp.int32, (MOBA_BLOCK, w4), 1) % tq
    s = jnp.dot(kb_ref[0, i, :, 0:WIDTH], qm_s[:, 0:w4], preferred_element_type=F32)
    s = jnp.where(key <= qry, s, NEG_INF)
    m = jnp.max(s, axis=0, keepdims=True)
    p = jnp.exp2(s - m)
    l = jnp.sum(p, axis=0, keepdims=True)
    pb = p.astype(BF16)
    for h in range(N_HEADS):
        rows = slice(h * HEAD_DIM, (h + 1) * HEAD_DIM)
        acc_s[rows, :] = jnp.dot(vt_ref[0, i, rows, :], pb[:, h * tq:(h + 1) * tq], preferred_element_type=F32)

    def half(n, m, l, cur, nxt):
        m_parts, l_parts = [], []
        for part in range(SOFTMAX_PARTS):
            cs = slice(part * w4 // SOFTMAX_PARTS, (part + 1) * w4 // SOFTMAX_PARTS)
            picked = [bias_s[pl.ds((n - 1) * grp + c, 1), cs] for c in range(grp)]
            m_old = m[:, cs]
            m_new = m_old
            for c in range(grp):
                m_new = jnp.maximum(m_new, jnp.max(s_bufs[cur][sub[c], cs], axis=0, keepdims=True) + picked[c])
            alpha = jnp.exp2(m_old - m_new)
            l_new = alpha * l[:, cs]
            weights = []
            for c in range(grp):
                p = jnp.exp2(s_bufs[cur][sub[c], cs] - (m_new - picked[c]))
                l_new = l_new + jnp.sum(p, axis=0, keepdims=True)
                weights.append(p.astype(BF16))
            rows = slice(part * HEAD_DIM, (part + 1) * HEAD_DIM)
            vt = jnp.concatenate([vt_ref[0, (n - 1) * grp + c, rows, :] for c in range(grp)], axis=1)
            o = jnp.dot(vt, jnp.concatenate(weights, axis=0), preferred_element_type=F32)
            acc_s[rows, :] = alpha * acc_s[rows, :] + o
            m_parts.append(m_new)
            l_parts.append(l_new)
        qk(jnp.minimum(n, n_groups - 1), nxt)
        return jnp.concatenate(m_parts, axis=1), jnp.concatenate(l_parts, axis=1)

    def visit(n, carry):
        return lax.cond(n % 2 == 1, lambda: half(n, carry[0], carry[1], 1, 0),
                        lambda: half(n, carry[0], carry[1], 0, 1))

    n_visits = (i + grp - 1) // grp
    m, l = lax.fori_loop(1, n_visits + 1, visit, (m, l))
    inv = 1.0 / l
    for h in range(N_HEADS):
        rows = slice(h * HEAD_DIM, (h + 1) * HEAD_DIM)
        acc_s[rows, :] = acc_s[rows, :] * inv[:, h * tq:(h + 1) * tq]
    y_b = acc_s[...].T
    o = _out_proj(x_ref[0], mixed_ref[0], y_b, wout_ref)
    if final:
        o = _rmsnorm(o, fng_ref[...])
    out_ref[0] = o


def _attn_call(qt, kb, vt, km, mixed, x, w_out, layer, fng, *, final):
    bsz, nblk, _, tq = qt.shape
    d_model = x.shape[-1]
    assert tq == MOBA_BLOCK
    grp = next(g for g in (ATTN_GROUP, 2, 1) if nblk % g == 0)
    body = functools.partial(_attn_body, nblk=nblk, tq=tq, grp=grp, final=final)
    whole = lambda r, c: pl.BlockSpec((1, nblk, r, c), lambda b, i: (b, 0, 0, 0))
    return pl.pallas_call(
        body,
        grid=(bsz, nblk),
        in_specs=[pl.BlockSpec((1, 1, WIDTH, tq), lambda b, i: (b, i, 0, 0)),
                  whole(MOBA_BLOCK, WIDTH + LANES), whole(WIDTH, MOBA_BLOCK),
                  pl.BlockSpec((1, nblk, WIDTH), lambda b, i: (b, 0, 0)),
                  pl.BlockSpec((1, tq, 4 * WIDTH), lambda b, i: (b, i, 0)),
                  pl.BlockSpec((1, tq, d_model), lambda b, i: (b, i, 0)),
                  pl.BlockSpec((None, 4 * WIDTH, d_model), lambda b, i: (layer, 0, 0)),
                  pl.BlockSpec((1, d_model), lambda b, i: (0, 0))],
        out_specs=pl.BlockSpec((1, tq, d_model), lambda b, i: (b, i, 0)),
        out_shape=jax.ShapeDtypeStruct(x.shape, F32),
        scratch_shapes=[pltpu.VMEM((WIDTH, N_HEADS * tq + LANES), BF16),
                        pltpu.VMEM((nblk, N_HEADS * tq), F32),
                        pltpu.VMEM((WIDTH, tq), F32)]
        + [pltpu.VMEM((grp * MOBA_BLOCK, N_HEADS * tq + LANES), F32)] * 2
        + [pltpu.VMEM((grp * MOBA_BLOCK, N_HEADS * tq + LANES), BF16)] * 2
        + [pltpu.VMEM((1, N_HEADS * tq), F32)] * 2,
        compiler_params=pltpu.CompilerParams(dimension_semantics=("arbitrary", "arbitrary"),
                                             vmem_limit_bytes=VMEM_LIMIT),
        name="attn_prompt",
    )(qt, kb, vt, km.reshape(bsz, nblk, WIDTH), mixed, x, w_out, fng)


def _dec_body(pt_ref, *refs, layer, n_seq, pps, ns, t_valid):
    (ck_ref, cv_ref, q_ref, kn_ref, vn_ref, yb_ref,
     qm_s, mp_s, lp_s, gp_s, op_s, pbd_s, kbuf, vbuf, sems) = refs
    step = pl.program_id(1)
    rows = N_HEADS * DEC_ROWS
    nb = pps // 2
    lane_head = lax.broadcasted_iota(jnp.int32, (DEC_ROWS, WIDTH), 1) // HEAD_DIM
    lane = lax.broadcasted_iota(jnp.int32, (rows, LANES), 1)

    g = pl.program_id(0) * ns + step
    n_steps = pl.num_programs(0) * ns

    def page_copies(gg, page_of):
        slot = gg % DEC_SLOTS
        for c in range(pps):
            page = page_of(c)
            yield pltpu.make_async_copy(ck_ref.at[layer, page], kbuf.at[slot, c], sems.at[0, slot])
            yield pltpu.make_async_copy(cv_ref.at[layer, page], vbuf.at[slot, c], sems.at[1, slot])

    def request(gg):
        seq_id, seq_step = gg // ns, gg % ns
        for copy in page_copies(gg, lambda c: pt_ref[seq_id, seq_step * pps + c]):
            copy.start()

    @pl.when(g == 0)
    def _():
        pbd_s[...] = jnp.zeros(pbd_s.shape, BF16)
        for ahead in range(min(DEC_SLOTS - 1, n_seq * ns)):
            request(ahead)

    @pl.when(g + (DEC_SLOTS - 1) < n_steps)
    def _():
        request(g + (DEC_SLOTS - 1))

    for copy in page_copies(g, lambda c: 0):
        copy.wait()
    slot = g % DEC_SLOTS

    @pl.when(step == 0)
    def _():
        q8 = q_ref[0]
        for h in range(N_HEADS):
            qm_s[h * DEC_ROWS:(h + 1) * DEC_ROWS, :] = jnp.where(lane_head == h, q8, 0.0)
        mp_s[...] = jnp.full((rows, LANES), NEG_INF, F32)
        gp_s[...] = jnp.full((rows, LANES), NEG_INF, F32)
        lp_s[...] = jnp.zeros((rows, LANES), F32)

    qmb = qm_s[...].astype(BF16)
    contract_last = (((1,), (1,)), ((), ()))
    kt = jnp.concatenate([kbuf[slot, c] for c in range(pps)], axis=1).astype(BF16)
    vt = jnp.concatenate([vbuf[slot, c] for c in range(pps)], axis=1).astype(BF16)
    s_all = jnp.dot(qmb, kt, preferred_element_type=F32)
    for c in range(nb):
        s = s_all[:, c * MOBA_BLOCK:(c + 1) * MOBA_BLOCK]
        gate = jnp.sum(s, axis=1, keepdims=True) * (1.0 / MOBA_BLOCK)
        m = jnp.max(s, axis=1, keepdims=True)
        p = jnp.exp(s - m)
        l = jnp.sum(p, axis=1, keepdims=True)
        j = step * nb + c
        mp_s[...] = jnp.where(lane == j, m, mp_s[...])
        lp_s[...] = jnp.where(lane == j, l, lp_s[...])
        gp_s[...] = jnp.where(lane == j, gate, gp_s[...])
        pbd_s[c * rows:(c + 1) * rows, c * MOBA_BLOCK:(c + 1) * MOBA_BLOCK] = p.astype(BF16)
    op_s[pl.ds(pl.multiple_of(step * (nb * rows), nb * rows), nb * rows), :] = lax.dot_general(
        pbd_s[...], vt, contract_last, preferred_element_type=F32)

    @pl.when(step == ns - 1)
    def _():
        nblk = ns * nb
        sel = _top_blocks_bias(gp_s[...], lane, LANES, 1) == 0.0
        zpad = jnp.zeros((LANES - DEC_ROWS, WIDTH), F32)
        kn = jnp.concatenate([kn_ref[0], zpad], axis=0).astype(BF16)
        vn = jnp.concatenate([vn_ref[0], zpad], axis=0).astype(BF16)
        s = lax.dot_general(qmb, kn, contract_last, preferred_element_type=F32)
        qpos = jnp.minimum(lax.broadcasted_iota(jnp.int32, (rows, LANES), 0) % DEC_ROWS, t_valid - 1)
        s = jnp.where(lane <= qpos, s, NEG_INF)
        m_own = jnp.max(s, axis=1, keepdims=True)
        p = jnp.exp(s - m_own)
        l_own = jnp.sum(p, axis=1, keepdims=True)
        o_own = jnp.dot(p.astype(BF16), vn, preferred_element_type=F32)
        mp = mp_s[...]
        m_fin = jnp.maximum(m_own, jnp.max(jnp.where(sel, mp, NEG_INF), axis=1, keepdims=True))
        w = jnp.where(sel, jnp.exp(mp - m_fin), 0.0)
        a_own = jnp.exp(m_own - m_fin)
        l_fin = a_own * l_own + jnp.sum(w * lp_s[...], axis=1, keepdims=True)
        o_fin = a_own * o_own
        for j in range(nblk):
            o_fin = o_fin + w[:, j:j + 1] * op_s[j * rows:(j + 1) * rows, :]
        y = o_fin / l_fin
        y_b = None
        for h in range(N_HEADS):
            part = jnp.where(lane_head == h, y[h * DEC_ROWS:(h + 1) * DEC_ROWS, :], 0.0)
            y_b = part if y_b is None else y_b + part
        yb_ref[0] = y_b


def _dec_call(page_table, cache_k, cache_v, layer, q, kn, vn, *, t_valid):
    bsz, n_pages = page_table.shape
    depth, n_pool, page, _, _ = cache_k.shape
    assert (n_pages * page) % MOBA_BLOCK == 0 and MOBA_BLOCK == 2 * page
    nblk = n_pages * page // MOBA_BLOCK
    assert MOBA_TOPK <= nblk <= LANES
    pps = next(c for c in (16, 8, 4, 2) if n_pages % c == 0)
    ns = n_pages // pps
    ck = cache_k.transpose(0, 1, 3, 4, 2).reshape(depth, n_pool, WIDTH, page)
    cv = cache_v.transpose(0, 1, 3, 4, 2).reshape(depth, n_pool, WIDTH, page)

    in_hbm = pl.BlockSpec(memory_space=pl.ANY)
    per_b = pl.BlockSpec((1, DEC_ROWS, WIDTH), lambda b, s, pt: (b, 0, 0))
    rows = N_HEADS * DEC_ROWS
    body = functools.partial(_dec_body, layer=layer, n_seq=bsz, pps=pps, ns=ns, t_valid=t_valid)
    grid_spec = pltpu.PrefetchScalarGridSpec(
        num_scalar_prefetch=1,
        grid=(bsz, ns),
        in_specs=[in_hbm, in_hbm, per_b, per_b, per_b],
        out_specs=per_b,
        scratch_shapes=[pltpu.VMEM((rows, WIDTH), F32), pltpu.VMEM((rows, LANES), F32),
                        pltpu.VMEM((rows, LANES), F32), pltpu.VMEM((rows, LANES), F32),
                        pltpu.VMEM((nblk * rows, WIDTH), F32),
                        pltpu.VMEM((pps // 2 * rows, pps * page), BF16),
                        pltpu.VMEM((DEC_SLOTS, pps, WIDTH, page), F32),
                        pltpu.VMEM((DEC_SLOTS, pps, WIDTH, page), F32),
                        pltpu.SemaphoreType.DMA((2, DEC_SLOTS))],
    )
    return pl.pallas_call(
        body,
        grid_spec=grid_spec,
        out_shape=jax.ShapeDtypeStruct((bsz, DEC_ROWS, WIDTH), F32),
        compiler_params=pltpu.CompilerParams(dimension_semantics=("arbitrary", "arbitrary"),
                                             vmem_limit_bytes=VMEM_LIMIT),
        name="attn_sample",
    )(page_table, ck, cv, q, kn, vn)


def _proj_body(yb_ref, mixed_ref, x_ref, wout_ref, fng_ref, out_ref, *, final):
    o = _out_proj(x_ref[...], mixed_ref[...], yb_ref[...], wout_ref)
    if final:
        o = _rmsnorm(o, fng_ref[...])
    out_ref[...] = o


def _proj_call(y_b, mixed, x, w_out, layer, fng, *, final):
    bsz, t_len, d_model = x.shape
    n = bsz * t_len
    whole = lambda r, c: pl.BlockSpec((r, c), lambda i: (0, 0))
    out = pl.pallas_call(
        functools.partial(_proj_body, final=final),
        grid=(1,),
        in_specs=[whole(n, WIDTH), whole(n, 4 * WIDTH), whole(n, d_model),
                  pl.BlockSpec((None, 4 * WIDTH, d_model), lambda i: (layer, 0, 0)),
                  whole(1, d_model)],
        out_specs=whole(n, d_model),
        out_shape=jax.ShapeDtypeStruct((n, d_model), F32),
        compiler_params=pltpu.CompilerParams(dimension_semantics=("arbitrary",), vmem_limit_bytes=VMEM_LIMIT),
        name="proj_sample",
    )(y_b.reshape(n, WIDTH), mixed.reshape(n, 4 * WIDTH), x.reshape(n, d_model), w_out, fng)
    return out.reshape(bsz, t_len, d_model)


def _rope_tables(pos0, n):
    freqs = ROPE_THETA ** (-jnp.arange(ROT_HALF, dtype=F32) * 2.0 / (2 * ROT_HALF))
    ang = (pos0 + jnp.arange(n, dtype=jnp.int32)).astype(F32)[:, None] * freqs[None, :]
    cos, sin = jnp.cos(ang), jnp.sin(ang)
    one = jnp.ones((n, HEAD_DIM - 2 * ROT_HALF), F32)
    zero8 = jnp.zeros((n, ROT_HALF), F32)
    zero_rest = jnp.zeros((n, HEAD_DIM - 2 * ROT_HALF), F32)
    rc = jnp.concatenate([cos, cos, one], axis=1)
    rsa = jnp.concatenate([-sin, zero8, zero_rest], axis=1)
    rsb = jnp.concatenate([zero8, sin, zero_rest], axis=1)
    return tuple(jnp.tile(a, (1, N_HEADS)) for a in (rc, rsa, rsb))


def _pad_state(state):
    return jnp.pad(state, ((0, 0), (0, 0), (HALO - state.shape[2], 0), (0, 0)))


def kernel(x_prompt, x_sample, cache_k, cache_v, page_table, state_conv_a, state_conv_c, state_pool_d,
           norm_g, w_in, conv_a_w, conv_c_w, conv_c_b, ln_c_g, ln_c_b, w_pw_c, w_pool, pool_scale, w_out,
           final_norm_g):
    depth = w_in.shape[0]
    bsz, seq, d_model = x_prompt.shape
    dec_b, dec_t, _ = x_sample.shape
    past_len = page_table.shape[1] * cache_k.shape[2]
    assert seq % MOBA_BLOCK == 0 and dec_t <= DEC_ROWS and WIDTH == N_HEADS * HEAD_DIM

    pre_tile = next(t for t in (PRE_TILE, MOBA_BLOCK) if seq % t == 0)
    tabs_p = _rope_tables(0, seq)
    tabs_s = _rope_tables(past_len, DEC_ROWS)
    zero_state = jnp.zeros((1, bsz, HALO, WIDTH), F32)
    hist_a, hist_c, hist_d = _pad_state(state_conv_a), _pad_state(state_conv_c), _pad_state(state_pool_d)
    fng = final_norm_g.reshape(1, d_model)
    yp = x_prompt
    ys = jnp.pad(x_sample, ((0, 0), (0, DEC_ROWS - dec_t), (0, 0)))
    outs = [[] for _ in range(8)]
    kstack = jnp.zeros((depth, bsz, WIDTH, seq), F32)
    vstack = jnp.zeros((depth, bsz, WIDTH, seq), F32)
    n_groups = w_pool.shape[1]
    eye = jnp.eye(n_groups, dtype=F32)
    pool_bd = (w_pool[:, :, :, None, :] * eye[None, :, None, :, None]).reshape(depth, WIDTH, WIDTH)
    row = lambda a: a.reshape(depth, 1, a.shape[-1])
    lw = {"norm_g": row(norm_g), "w_in": w_in.astype(BF16), "conv_a_w": conv_a_w, "conv_c_w": conv_c_w,
          "conv_c_b": row(conv_c_b), "ln_c_g": row(ln_c_g), "ln_c_b": row(ln_c_b),
          "w_pw_c": w_pw_c.astype(BF16), "w_pool": pool_bd.astype(BF16), "pool_scale": row(pool_scale)}
    wo = w_out.astype(BF16)
    for l in range(depth):
        final = l == depth - 1

        mixed, kstack, vstack, qt, kb, vt, km, ap, cp, dp = _pre_call(
            yp, lw, l, tabs_p, zero_state, zero_state, zero_state, 0, (kstack, vstack),
            ns=1, tm=pre_tile, last_rows=pre_tile, pos0=0, prompt=True)
        yp = _attn_call(qt, kb, vt, km, mixed, yp, wo, l, fng, final=final)

        mixed_s, ks, vs, qs, as_, cs, ds = _pre_call(
            ys, lw, l, tabs_s, hist_a, hist_c, hist_d, l,
            ns=dec_b, tm=DEC_ROWS, last_rows=dec_t, pos0=past_len, prompt=False)
        yb_s = _dec_call(page_table, cache_k, cache_v, l, qs, ks, vs, t_valid=dec_t)
        ys = _proj_call(yb_s, mixed_s, ys, wo, l, fng, final=final)

        for lst, val in zip(outs, (ks[:, :dec_t].reshape(dec_b, dec_t, N_HEADS, HEAD_DIM),
                                   vs[:, :dec_t].reshape(dec_b, dec_t, N_HEADS, HEAD_DIM),
                                   ap, as_, cp, cs, dp, ds)):
            lst.append(val)
    to_rows = lambda st: st.reshape(depth, bsz, N_HEADS, HEAD_DIM, seq).transpose(0, 1, 4, 2, 3)
    return (yp, ys[:, :dec_t], to_rows(kstack), to_rows(vstack)) + tuple(jnp.stack(o) for o in outs)
```
